```python
import math
import jax, jax.numpy as jnp
from jax import lax
import numpy as np

D_MODEL = 1024
BATCH = 4
SEQ = 8192
DEPTH = 1
DEC_BATCH = 32
DEC_SEQ = 4
PAST_LEN = 16384
PAGE_SIZE = 128

EXPAND = 2
D_MIX = EXPAND * D_MODEL
D_ATTN = D_MIX // 2
D_CONV = D_MIX - D_ATTN
HEAD_DIM = 64
N_HEADS = D_ATTN // HEAD_DIM
N_KV = 2
HPG = N_HEADS // N_KV
D_KV = N_KV * HEAD_DIM
BLK = 64
N_SEL = 16
WINDOW = 512
CMP_HID = 2 * HEAD_DIM
CONV_W = 3
N_BUCKETS = 32
MAX_DIST = 128
Q_BLOCK = 128
EPS = 1e-6
NEG = -1e30
PROJ_SIZES = (D_ATTN, D_KV, D_KV, D_KV, D_KV, D_KV, D_KV, 3 * N_HEADS, D_CONV, D_CONV, D_CONV, D_MIX)
D_IN = sum(PROJ_SIZES)

kernel_name = "hymba_nsa_shortconv_adaln_step"


def rmsnorm(x, g):
    xf = x.astype(jnp.float32)
    y = xf * lax.rsqrt(jnp.mean(xf * xf, axis=-1, keepdims=True) + EPS)
    return (y * g.astype(jnp.float32)).astype(x.dtype)


def t5_bucket(dist):
    n = jnp.maximum(dist, 0)
    max_exact = N_BUCKETS // 2
    nf = jnp.maximum(n, max_exact).astype(jnp.float32)
    large = max_exact + (jnp.log(nf / max_exact) / math.log(MAX_DIST / max_exact)
                         * (N_BUCKETS - max_exact)).astype(jnp.int32)
    return jnp.where(n < max_exact, n, jnp.minimum(large, N_BUCKETS - 1))


def masked_softmax(logits, mask):
    p = jax.nn.softmax(jnp.where(mask, logits.astype(jnp.float32), NEG), axis=-1)
    return p * jnp.any(mask, axis=-1, keepdims=True)


def split_proj(u):
    offs = np.cumsum((0,) + PROJ_SIZES)
    return [u[..., int(offs[i]):int(offs[i + 1])] for i in range(len(PROJ_SIZES))]


def modulate_project(x, c, ada_w, ada_b, norm_g, w_in):
    shift, scale, gate = jnp.split(c @ ada_w + ada_b, 3, axis=-1)
    h = rmsnorm(x, norm_g) * (1 + scale[:, None, :]) + shift[:, None, :]
    return split_proj(h @ w_in), gate


def kv_heads(t):
    return t.reshape(t.shape[0], t.shape[1], N_KV, HEAD_DIM)


def compress(kv, pe, w1, w2):
    b, l, g, d = kv.shape
    blocks = kv.reshape(b, l // BLK, BLK, g, d) + pe[None, None, :, None, :]
    flat = blocks.transpose(0, 1, 3, 2, 4).reshape(b, l // BLK, g, BLK * d)
    return jax.nn.silu(flat @ w1) @ w2


def to_blocks(k):
    b, l, g, d = k.shape
    return k.reshape(b, l // BLK, BLK, g, d).transpose(0, 3, 1, 2, 4)


def nsa_attend(q, gts, q_pos, k_cmp, v_cmp, k_sel, v_sel, k_win, v_win, win_pos, rel_bias):
    qs = q * (HEAD_DIM ** -0.5)
    nb = k_cmp.shape[1]
    blk_start = jnp.arange(nb, dtype=jnp.int32) * BLK
    blk_end = blk_start + BLK - 1
    tab = rel_bias.reshape(N_BUCKETS, N_KV, HPG)
    d_cmp = q_pos[:, None] - blk_end[None, :]
    m_cmp = d_cmp >= 0
    lg = jnp.einsum('bqghd,bngd->bghqn', qs, k_cmp) + tab[t5_bucket(d_cmp)].transpose(2, 3, 0, 1)
    p_cmp = masked_softmax(lg, m_cmp)
    o_cmp = jnp.einsum('bghqn,bngd->bqghd', p_cmp.astype(v_cmp.dtype), v_cmp)
    imp = p_cmp.sum(axis=2)
    score = jnp.where(m_cmp, imp, jnp.where(blk_start[None, :] <= q_pos[:, None], 2.0, -1.0))
    top_s, idx = lax.top_k(score, min(N_SEL, nb))
    gather = jax.vmap(jax.vmap(lambda kb, ix: kb[ix]))
    ks = gather(k_sel, idx)
    vs = gather(v_sel, idx)
    s_pos = idx[..., None] * BLK + jnp.arange(BLK, dtype=jnp.int32)
    d_sel = q_pos[None, None, :, None, None] - s_pos
    m_sel = (d_sel >= 0) & (top_s[..., None] >= 0)
    b_sel = jax.vmap(lambda tb, bk: tb[bk], in_axes=(1, 1), out_axes=1)(tab, t5_bucket(d_sel))
    lg = jnp.einsum('bqghd,bgqnkd->bghqnk', qs, ks) + b_sel.transpose(0, 1, 5, 2, 3, 4)
    shp = lg.shape
    p_sel = masked_softmax(lg.reshape(shp[:4] + (-1,)), m_sel.reshape(m_sel.shape[:3] + (-1,))[:, :, None])
    o_sel = jnp.einsum('bghqnk,bgqnkd->bqghd', p_sel.reshape(shp).astype(vs.dtype), vs)
    d_win = q_pos[:, None] - win_pos[None, :]
    m_win = (d_win >= 0) & (d_win <= WINDOW) & (win_pos[None, :] >= 0)
    lg = jnp.einsum('bqghd,bwgd->bghqw', qs, k_win) + tab[t5_bucket(d_win)].transpose(2, 3, 0, 1)
    p_win = masked_softmax(lg, m_win)
    o_win = jnp.einsum('bghqw,bwgd->bqghd', p_win.astype(v_win.dtype), v_win)
    g = jax.nn.sigmoid(gts.astype(jnp.float32)).astype(q.dtype)[..., None]
    return g[:, :, 0] * o_cmp + g[:, :, 1] * o_sel + g[:, :, 2] * o_win


def nsa_prompt(q, gts, kc, vc, ksl, vsl, kw, vw, cmpw, rel_bias):
    b, s = q.shape[:2]
    k_cmp, v_cmp = compress(kc, *cmpw[:3]), compress(vc, *cmpw[3:])
    k_sel, v_sel = to_blocks(ksl), to_blocks(vsl)
    pad = ((0, 0), (WINDOW, 0), (0, 0), (0, 0))
    kw_pad, vw_pad = jnp.pad(kw, pad), jnp.pad(vw, pad)

    def q_block(i):
        s0 = i * Q_BLOCK
        qb = lax.dynamic_slice_in_dim(q, s0, Q_BLOCK, axis=1)
        gb = lax.dynamic_slice_in_dim(gts, s0, Q_BLOCK, axis=1)
        kwb = lax.dynamic_slice_in_dim(kw_pad, s0, WINDOW + Q_BLOCK, axis=1)
        vwb = lax.dynamic_slice_in_dim(vw_pad, s0, WINDOW + Q_BLOCK, axis=1)
        q_pos = s0 + jnp.arange(Q_BLOCK, dtype=jnp.int32)
        w_pos = s0 - WINDOW + jnp.arange(WINDOW + Q_BLOCK, dtype=jnp.int32)
        return nsa_attend(qb, gb, q_pos, k_cmp, v_cmp, k_sel, v_sel, kwb, vwb, w_pos, rel_bias)

    o = lax.map(q_block, jnp.arange(s // Q_BLOCK, dtype=jnp.int32))
    return jnp.moveaxis(o, 0, 1).reshape(b, s, D_ATTN)


def nsa_sample(q, gts, kc, vc, ksl, vsl, kw, vw, past_kc, past_vc, past_ks, past_vs,
               buf_kw, buf_vw, cmpw, rel_bias):
    b, t = q.shape[:2]
    past = past_kc.shape[1]
    pad_len = (-(past + t)) % BLK

    def full(p, n):
        return jnp.pad(jnp.concatenate([p, n], axis=1), ((0, 0), (0, pad_len), (0, 0), (0, 0)))

    k_cmp, v_cmp = compress(full(past_kc, kc), *cmpw[:3]), compress(full(past_vc, vc), *cmpw[3:])
    k_sel, v_sel = to_blocks(full(past_ks, ksl)), to_blocks(full(past_vs, vsl))
    k_win = jnp.concatenate([buf_kw, kw], axis=1)
    v_win = jnp.concatenate([buf_vw, vw], axis=1)
    q_pos = past + jnp.arange(t, dtype=jnp.int32)
    w_pos = past - buf_kw.shape[1] + jnp.arange(buf_kw.shape[1] + t, dtype=jnp.int32)
    o = nsa_attend(q, gts, q_pos, k_cmp, v_cmp, k_sel, v_sel, k_win, v_win, w_pos, rel_bias)
    return o.reshape(b, t, D_ATTN)


def short_conv(u_pad, w):
    t = u_pad.shape[1] - (CONV_W - 1)
    return sum(w[k] * u_pad[:, k:k + t] for k in range(CONV_W))


def merge_output(x, gate, o_attn, y_conv, z, w_out):
    z_a, z_c = z[..., :D_ATTN], z[..., D_ATTN:]
    o = jnp.concatenate([o_attn * jax.nn.silu(z_a), y_conv * jax.nn.silu(z_c)], axis=-1) @ w_out
    return x + gate[:, None, :] * o


def gather_pages(cache, page_table):
    g = cache[page_table]
    return g.reshape(g.shape[0], g.shape[1] * g.shape[2], g.shape[3], g.shape[4])


def setup_inputs(seed: int = 0) -> dict:
    key = jax.random.key(seed)
    ks = jax.random.split(key, 32)
    n_pages = PAST_LEN // PAGE_SIZE
    n_used = DEC_BATCH * n_pages
    n_pool = n_used + n_used // 4
    win_buf = min(WINDOW, PAST_LEN)

    def nrm(k, shape, s=1.0):
        return s * jax.random.normal(k, shape, jnp.float32)

    page_table = jax.random.permutation(ks[0], n_pool)[:n_used].reshape(DEC_BATCH, n_pages).astype(jnp.int32)
    kv_page = (DEPTH, n_pool, PAGE_SIZE, N_KV, HEAD_DIM)
    kv_win = (DEPTH, DEC_BATCH, win_buf, N_KV, HEAD_DIM)
    return {
        "x_prompt": nrm(ks[1], (BATCH, SEQ, D_MODEL)),
        "x_sample": nrm(ks[2], (DEC_BATCH, DEC_SEQ, D_MODEL)),
        "cache_k_cmp": nrm(ks[3], kv_page),
        "cache_v_cmp": nrm(ks[4], kv_page),
        "cache_k_slc": nrm(ks[5], kv_page),
        "cache_v_slc": nrm(ks[6], kv_page),
        "cache_k_win": nrm(ks[7], kv_win),
        "cache_v_win": nrm(ks[8], kv_win),
        "state_conv": nrm(ks[9], (DEPTH, DEC_BATCH, CONV_W - 1, D_CONV)),
        "page_table": page_table,
        "c_prompt": nrm(ks[10], (BATCH, D_MODEL)),
        "c_sample": nrm(ks[11], (DEC_BATCH, D_MODEL)),
        "ada_w": nrm(ks[12], (DEPTH, D_MODEL, 3 * D_MODEL), 0.5 * D_MODEL ** -0.5),
        "ada_b": nrm(ks[13], (DEPTH, 3 * D_MODEL), 0.02),
        "norm_g": 1.0 + nrm(ks[14], (DEPTH, D_MODEL), 0.02),
        "w_in": nrm(ks[15], (DEPTH, D_MODEL, D_IN), D_MODEL ** -0.5),
        "cmp_pe_k": nrm(ks[16], (DEPTH, BLK, HEAD_DIM), 0.1),
        "cmp_w1_k": nrm(ks[17], (DEPTH, BLK * HEAD_DIM, CMP_HID), (BLK * HEAD_DIM) ** -0.5),
        "cmp_w2_k": nrm(ks[18], (DEPTH, CMP_HID, HEAD_DIM), CMP_HID ** -0.5),
        "cmp_pe_v": nrm(ks[19], (DEPTH, BLK, HEAD_DIM), 0.1),
        "cmp_w1_v": nrm(ks[20], (DEPTH, BLK * HEAD_DIM, CMP_HID), (BLK * HEAD_DIM) ** -0.5),
        "cmp_w2_v": nrm(ks[21], (DEPTH, CMP_HID, HEAD_DIM), CMP_HID ** -0.5),
        "conv_w": nrm(ks[22], (DEPTH, CONV_W, D_CONV), 0.5),
        "w_out": nrm(ks[23], (DEPTH, D_MIX, D_MODEL), D_MIX ** -0.5),
        "rel_bias": nrm(ks[24], (N_BUCKETS, N_HEADS), 0.5),
        "final_g": 1.0 + nrm(ks[25], (D_MODEL,), 0.02),
    }


def reference(x_prompt, x_sample, cache_k_cmp, cache_v_cmp, cache_k_slc, cache_v_slc,
              cache_k_win, cache_v_win, state_conv, page_table, c_prompt, c_sample,
              ada_w, ada_b, norm_g, w_in, cmp_pe_k, cmp_w1_k, cmp_w2_k, cmp_pe_v, cmp_w1_v,
              cmp_w2_v, conv_w, w_out, rel_bias, final_g):
    xp, xs = x_prompt, x_sample
    prompt_states, sample_states = [], []
    for l in range(DEPTH):
        cmpw = (cmp_pe_k[l], cmp_w1_k[l], cmp_w2_k[l], cmp_pe_v[l], cmp_w1_v[l], cmp_w2_v[l])
        (q, kc, vc, ksl, vsl, kw, vw, gts, hc, bc, cc, z), gate = modulate_project(
            xp, c_prompt, ada_w[l], ada_b[l], norm_g[l], w_in[l])
        b, s = q.shape[:2]
        q = q.reshape(b, s, N_KV, HPG, HEAD_DIM)
        gts = gts.reshape(b, s, 3, N_KV, HPG)
        kc, vc, ksl, vsl, kw, vw = (kv_heads(t) for t in (kc, vc, ksl, vsl, kw, vw))
        o_attn = nsa_prompt(q, gts, kc, vc, ksl, vsl, kw, vw, cmpw, rel_bias)
        u = cc * hc
        y_conv = bc * short_conv(jnp.pad(u, ((0, 0), (CONV_W - 1, 0), (0, 0))), conv_w[l])
        xp = merge_output(xp, gate, o_attn, y_conv, z, w_out[l])
        w_keep = min(WINDOW, s)
        prompt_states.append((kc, vc, ksl, vsl, kw[:, -w_keep:], vw[:, -w_keep:], u[:, -(CONV_W - 1):]))
        (q, kc, vc, ksl, vsl, kw, vw, gts, hc, bc, cc, z), gate = modulate_project(
            xs, c_sample, ada_w[l], ada_b[l], norm_g[l], w_in[l])
        b, t = q.shape[:2]
        q = q.reshape(b, t, N_KV, HPG, HEAD_DIM)
        gts = gts.reshape(b, t, 3, N_KV, HPG)
        kc, vc, ksl, vsl, kw, vw = (kv_heads(a) for a in (kc, vc, ksl, vsl, kw, vw))
        buf_kw, buf_vw = cache_k_win[l], cache_v_win[l]
        o_attn = nsa_sample(q, gts, kc, vc, ksl, vsl, kw, vw,
                            gather_pages(cache_k_cmp[l], page_table), gather_pages(cache_v_cmp[l], page_table),
                            gather_pages(cache_k_slc[l], page_table), gather_pages(cache_v_slc[l], page_table),
                            buf_kw, buf_vw, cmpw, rel_bias)
        u = cc * hc
        u_pad = jnp.concatenate([state_conv[l].astype(u.dtype), u], axis=1)
        y_conv = bc * short_conv(u_pad, conv_w[l])
        xs = merge_output(xs, gate, o_attn, y_conv, z, w_out[l])
        nbuf = buf_kw.shape[1]
        new_kw = jnp.concatenate([buf_kw, kw], axis=1)[:, -nbuf:]
        new_vw = jnp.concatenate([buf_vw, vw], axis=1)[:, -nbuf:]
        sample_states.append((kc, vc, ksl, vsl, new_kw, new_vw, u_pad[:, -(CONV_W - 1):]))
    y_prompt = rmsnorm(xp, final_g)
    y_sample = rmsnorm(xs, final_g)
    p_kc, p_vc, p_ks, p_vs, p_kw, p_vw, p_cv = (jnp.stack(a) for a in zip(*prompt_states))
    s_kc, s_vc, s_ks, s_vs, s_kw, s_vw, s_cv = (jnp.stack(a) for a in zip(*sample_states))
    return (y_prompt, y_sample, p_kc, p_vc, p_ks, p_vs, p_kw, p_vw, p_cv,
            s_kc, s_vc, s_ks, s_vs, s_kw, s_vw, s_cv)
```

```python
import functools
import math

import numpy as np
import jax
import jax.numpy as jnp
from jax import lax
from jax.experimental import pallas as pl
from jax.experimental.pallas import tpu as pltpu

BF = jnp.bfloat16
F32 = jnp.float32
I32 = jnp.int32

HEAD_DIM = 64
N_KV = 2
HPG = 8
N_HEADS = N_KV * HPG
D_KV = N_KV * HEAD_DIM
BLK = 64
BLK_SHIFT = 6
TAIL_BLOCKS = 16
N_SEL = 16
WINDOW = 512
CMP_HID = 2 * HEAD_DIM
CONV_W = 3
N_BUCKETS = 32
MAX_DIST = 128
EPS = 1e-6
NEG = -1e30
M_INIT = -5e29
VALID_THRESH = -1e29
QT = 128
KT = 128
NCOL = N_HEADS * QT
GCOL = HPG * QT
LANES = 128
SAMPLE_KEY_TILE = 8192
V7X_VMEM_LIMIT = 56 * 1024 * 1024


def _dot(a, b):
    return jnp.dot(a, b, preferred_element_type=F32)


def _dot_nt(a, b):
    return lax.dot_general(a, b, (((1,), (1,)), ((), ())), preferred_element_type=F32)


def _sigmoid(x):
    return 1.0 / (1.0 + jnp.exp(-x))


def _silu(x):
    return x * _sigmoid(x)


def _ada_kernel(c_ref, w_ref, b_ref, o_ref):
    o_ref[...] = _dot(c_ref[...].astype(BF), w_ref[...].astype(BF)) + b_ref[...]


def _ada(c_all, ada_w, ada_b):
    rows, d = c_all.shape
    n = ada_w.shape[1]
    return pl.pallas_call(
        _ada_kernel,
        grid=(n // d,),
        in_specs=[pl.BlockSpec((rows, d), lambda j: (0, 0)),
                  pl.BlockSpec((d, d), lambda j: (0, j)),
                  pl.BlockSpec((1, d), lambda j: (0, j))],
        out_specs=pl.BlockSpec((rows, d), lambda j: (0, j)),
        out_shape=jax.ShapeDtypeStruct((rows, n), F32),
        name="ada_modulation",
    )(c_all, ada_w, ada_b.reshape(1, n))


def _proj_body(x_ref, sc_ref, sh_ref, ng_ref, wq_ref, wkv_ref, wg_ref, wc_ref, wz_ref, cw_ref):
    x = x_ref[...]
    ms = jnp.mean(x * x, axis=-1, keepdims=True)
    xn = x * lax.rsqrt(ms + EPS) * ng_ref[...]
    h = xn * sc_ref[0] + sh_ref[0]
    hb = h.astype(BF)
    q = _dot(hb, wq_ref[...]) * (HEAD_DIM ** -0.5)
    kv = _dot(hb, wkv_ref[...])
    gates = _sigmoid(_dot(hb, wg_ref[...]))
    c3 = _dot(hb, wc_ref[...])
    dc = c3.shape[1] // 3
    hc, bc, cc = c3[:, :dc], c3[:, dc:2 * dc], c3[:, 2 * dc:]
    u = cc * hc
    z = _dot(hb, wz_ref[...])
    da = z.shape[1] - dc
    sza = _silu(z[:, :da])
    szc = _silu(z[:, da:])
    return q, kv, gates, u, bc, sza, szc


def _proj_prompt_kernel(x_ref, sc_ref, sh_ref, ng_ref, wq_ref, wkv_ref, wg_ref, wc_ref, wz_ref, cw_ref,
                        qT_ref, kc_ref, vc_ref, ks_ref, vs_ref, kw_ref, vw_ref,
                        kslc_ref, vslcT_ref, kwin_ref, vwinT_ref, gates_ref, yc_ref, sza_ref, cv_ref,
                        uext_ref, *, tiles_per_batch, tm):
    t = pl.program_id(0)
    q, kv, gates, u, bc, sza, szc = _proj_body(
        x_ref, sc_ref, sh_ref, ng_ref, wq_ref, wkv_ref, wg_ref, wc_ref, wz_ref, cw_ref)
    qT_ref[0] = q.T.astype(BF)
    for n, r in enumerate((kc_ref, vc_ref, ks_ref, vs_ref, kw_ref, vw_ref)):
        r[...] = kv[:, n * D_KV:(n + 1) * D_KV]
    kslc_ref[0] = kv[:, 2 * D_KV:3 * D_KV].astype(BF)
    vslcT_ref[0] = kv[:, 3 * D_KV:4 * D_KV].T.astype(BF)
    kwin_ref[0] = kv[:, 4 * D_KV:5 * D_KV].astype(BF)
    vwinT_ref[0] = kv[:, 5 * D_KV:6 * D_KV].T.astype(BF)
    gates_ref[...] = gates

    first = (t % tiles_per_batch) == 0

    @pl.when(first)
    def _():
        uext_ref[0:8, :] = jnp.zeros((8, u.shape[1]), F32)

    @pl.when(jnp.logical_not(first))
    def _():
        uext_ref[0:8, :] = uext_ref[tm:tm + 8, :]

    uext_ref[8:8 + tm, :] = u
    y = (cw_ref[2:3, :] * u + cw_ref[1:2, :] * uext_ref[7:7 + tm, :]
         + cw_ref[0:1, :] * uext_ref[6:6 + tm, :])
    yc_ref[...] = (bc * y * szc).astype(BF)
    sza_ref[...] = sza.astype(BF)
    cv_ref[0] = u[tm - (CONV_W - 1):, :]


def _proj_sample_kernel(x_ref, sc_ref, sh_ref, ng_ref, wq_ref, wkv_ref, wg_ref, wc_ref, wz_ref, cw_ref,
                        st_ref, q_ref, kv_ref, gates_ref, yc_ref, sza_ref, cv_ref, uext_ref, *, nb, nt):
    q, kv, gates, u, bc, sza, szc = _proj_body(
        x_ref, sc_ref, sh_ref, ng_ref, wq_ref, wkv_ref, wg_ref, wc_ref, wz_ref, cw_ref)
    q_ref[...] = q
    kv_ref[...] = kv
    gates_ref[...] = gates
    ns = (CONV_W - 1) * nb
    uext_ref[0:ns, :] = st_ref[...]
    uext_ref[ns:ns + nt * nb, :] = u
    y = cw_ref[2:3, :] * u
    for k in range(CONV_W - 1):
        y = y + cw_ref[k:k + 1, :] * uext_ref[k * nb:k * nb + nt * nb, :]
    yc_ref[...] = (bc * y * szc).astype(BF)
    sza_ref[...] = sza.astype(BF)
    cv_ref[...] = uext_ref[nt * nb:nt * nb + ns, :]


def _const_spec(shape):
    nd = len(shape)
    return pl.BlockSpec(shape, lambda *_: (0,) * nd, pipeline_mode=pl.Buffered(1))


def _proj_prompt(x2d, sc, sh, ng, wts, cw, batch, seq, tm):
    tokens, d = x2d.shape
    tpb = seq // tm
    wq, wkv, wg, wc, wz = wts
    dc = wc.shape[1] // 3
    da = wz.shape[1] - dc
    tok_spec = lambda w: pl.BlockSpec((tm, w), lambda t: (t, 0))
    rowT_spec = lambda r: pl.BlockSpec((1, r, tm), lambda t: (t // tpb, 0, t % tpb))
    row_spec = lambda w: pl.BlockSpec((1, tm, w), lambda t: (t // tpb, t % tpb, 0))
    mod_spec = pl.BlockSpec((1, 1, d), lambda t: (t // tpb, 0, 0))
    out_shape = (
        [jax.ShapeDtypeStruct((batch, wq.shape[1], seq), BF)]
        + [jax.ShapeDtypeStruct((tokens, D_KV), F32)] * 6
        + [jax.ShapeDtypeStruct((batch, seq, D_KV), BF), jax.ShapeDtypeStruct((batch, D_KV, seq), BF),
           jax.ShapeDtypeStruct((batch, seq, D_KV), BF), jax.ShapeDtypeStruct((batch, D_KV, seq), BF),
           jax.ShapeDtypeStruct((tokens, LANES), F32),
           jax.ShapeDtypeStruct((tokens, dc), BF), jax.ShapeDtypeStruct((tokens, da), BF),
           jax.ShapeDtypeStruct((batch, CONV_W - 1, dc), F32)])
    out_specs = (
        [rowT_spec(wq.shape[1])] + [tok_spec(D_KV)] * 6
        + [row_spec(D_KV), rowT_spec(D_KV), row_spec(D_KV), rowT_spec(D_KV),
           tok_spec(LANES), tok_spec(dc), tok_spec(da),
           pl.BlockSpec((1, CONV_W - 1, dc), lambda t: (t // tpb, 0, 0))])
    return pl.pallas_call(
        functools.partial(_proj_prompt_kernel, tiles_per_batch=tpb, tm=tm),
        grid=(tokens // tm,),
        in_specs=[tok_spec(d), mod_spec, mod_spec, _const_spec((1, d)),
                  _const_spec(wq.shape), _const_spec(wkv.shape), _const_spec(wg.shape),
                  _const_spec(wc.shape), _const_spec(wz.shape), _const_spec(cw.shape)],
        out_specs=out_specs,
        out_shape=out_shape,
        scratch_shapes=[pltpu.VMEM((tm + 8, dc), F32)],
        compiler_params=pltpu.CompilerParams(dimension_semantics=("arbitrary",),
                                             vmem_limit_bytes=V7X_VMEM_LIMIT),
        name="proj_prompt",
    )(x2d, sc, sh, ng, wq, wkv, wg, wc, wz, cw)


def _proj_sample(x2d, sc, sh, ng, wts, cw, state2d, nb, nt):
    tokens, d = x2d.shape
    wq, wkv, wg, wc, wz = wts
    dc = wc.shape[1] // 3
    da = wz.shape[1] - dc
    ns = (CONV_W - 1) * nb
    full = lambda a: pl.BlockSpec(a.shape, lambda i: (0,) * a.ndim)
    ins = (x2d, sc, sh, ng, wq, wkv, wg, wc, wz, cw, state2d)
    out_shape = [jax.ShapeDtypeStruct((tokens, wq.shape[1]), F32),
                 jax.ShapeDtypeStruct((tokens, wkv.shape[1]), F32),
                 jax.ShapeDtypeStruct((tokens, LANES), F32),
                 jax.ShapeDtypeStruct((tokens, dc), BF), jax.ShapeDtypeStruct((tokens, da), BF),
                 jax.ShapeDtypeStruct((ns, dc), F32)]
    return pl.pallas_call(
        functools.partial(_proj_sample_kernel, nb=nb, nt=nt),
        grid=(1,),
        in_specs=[full(a) for a in ins],
        out_specs=[pl.BlockSpec(s.shape, lambda i: (0, 0)) for s in out_shape],
        out_shape=out_shape,
        scratch_shapes=[pltpu.VMEM((ns + tokens, dc), F32)],
        compiler_params=pltpu.CompilerParams(vmem_limit_bytes=V7X_VMEM_LIMIT),
        name="proj_sample",
    )(*ins)


def _compress(x_ref, nblk, pe_ref, w1_ref, w2_ref):
    def body(p, acc):
        xp = x_ref[pl.ds(p, nblk, stride=BLK), :] + pe_ref[pl.ds(p, 1), :]
        return acc + _dot(xp.astype(BF), w1_ref[p])
    acc = lax.fori_loop(0, BLK, body, jnp.zeros((nblk, N_KV * CMP_HID), F32))
    return _dot(_silu(acc).astype(BF), w2_ref[...])


def _compress_prompt_kernel(kc_ref, vc_ref, pek_ref, w1k_ref, w2k_ref, pev_ref, w1v_ref, w2v_ref,
                            kcmp_ref, vcmpT_ref, *, nblk):
    kcmp_ref[0] = _compress(kc_ref.at[0], nblk, pek_ref, w1k_ref, w2k_ref).astype(BF)
    vcmpT_ref[0] = _compress(vc_ref.at[0], nblk, pev_ref, w1v_ref, w2v_ref).T.astype(BF)


def _compress_prompt(kc, vc, cw, batch, seq):
    nblk = seq // BLK
    row = pl.BlockSpec((1, seq, D_KV), lambda b: (b, 0, 0))
    return pl.pallas_call(
        functools.partial(_compress_prompt_kernel, nblk=nblk),
        grid=(batch,),
        in_specs=[row, row] + [_const_spec(a.shape) for a in cw],
        out_specs=[pl.BlockSpec((1, nblk, D_KV), lambda b: (b, 0, 0)),
                   pl.BlockSpec((1, D_KV, nblk), lambda b: (b, 0, 0))],
        out_shape=[jax.ShapeDtypeStruct((batch, nblk, D_KV), BF),
                   jax.ShapeDtypeStruct((batch, D_KV, nblk), BF)],
        compiler_params=pltpu.CompilerParams(vmem_limit_bytes=V7X_VMEM_LIMIT),
        name="compress_prompt",
    )(kc, vc, *cw)


def _sortable_key(score):
    bits = pltpu.bitcast(score, I32)
    return jnp.where(bits < 0, bits ^ jnp.int32(0x7FFFFFFF), bits)


def _select_topk(score, tri, axis, k):
    key = _sortable_key(score)
    cnt = lambda m: jnp.sum(jnp.where(m, 1.0, 0.0), axis=axis, keepdims=True)
    int_min = jnp.int32(-2 ** 31)
    kf = float(k)
    thr = jnp.where(cnt(key >= 0) >= kf, jnp.int32(0), int_min)
    for b in range(30, -1, -1):
        cand = thr | jnp.int32(1 << b)
        thr = jnp.where(cnt(key >= cand) >= kf, cand, thr)
    gt = key > thr
    tie = key == thr
    tie_bf = jnp.where(tie, 1.0, 0.0).astype(BF)
    lower_ties = _dot(tri, tie_bf) if axis == 0 else _dot(tie_bf, tri)
    return jnp.logical_or(gt, jnp.logical_and(tie, lower_ties + cnt(gt) < kf))


def _online_update(br, s, vT, m_ref, l_ref, acc_ref):
    m_old = m_ref[br]
    m_new = jnp.maximum(m_old, jnp.max(s, axis=0, keepdims=True))
    alpha = jnp.exp(m_old - m_new)
    p = jnp.exp(s - m_new)
    l_ref[br] = alpha * l_ref[br] + jnp.sum(p, axis=0, keepdims=True)
    m_ref[br] = m_new
    pb = p.astype(BF)
    for g in range(N_KV):
        pv = _dot(vT[g * HEAD_DIM:(g + 1) * HEAD_DIM, :], pb[:, g * GCOL:(g + 1) * GCOL])
        acc_ref[br, g] = alpha[:, g * GCOL:(g + 1) * GCOL] * acc_ref[br, g] + pv


def _nsa_prompt_kernel(qT_ref, gates_ref, kslc_ref, vslcT_ref, kwin_ref, vwinT_ref, kcmp_ref, vcmpT_ref,
                       cb_ref, sb_ref, w0_ref, tri_ref, o_ref,
                       rhs_ref, m_ref, l_ref, acc_ref, *, nb):
    i = pl.program_id(1)

    for gh in range(N_HEADS):
        g = gh // HPG
        cols = slice(gh * QT, (gh + 1) * QT)
        rhs_ref[g * HEAD_DIM:(g + 1) * HEAD_DIM, cols] = qT_ref[0, gh * HEAD_DIM:(gh + 1) * HEAD_DIM, :]
        og = 1 - g
        rhs_ref[og * HEAD_DIM:(og + 1) * HEAD_DIM, cols] = jnp.zeros((HEAD_DIM, QT), BF)

    lg = _dot(kcmp_ref[0], rhs_ref[0:D_KV, :])
    rel = 2 * i - lax.broadcasted_iota(I32, (nb, NCOL), 0)
    bias = jnp.where(rel >= 3, 0.0, NEG)
    for rr in range(4):
        bias = jnp.where(rel == rr - 1, cb_ref[rr:rr + 1, :], bias)
    lgb = lg + bias
    mc = jnp.max(lgb, axis=0, keepdims=True)
    pc = jnp.exp(lgb - mc)
    lc = jnp.sum(pc, axis=0, keepdims=True)
    pn = pc * jnp.where(mc > VALID_THRESH, 1.0 / lc, 0.0)
    pnb = pn.astype(BF)
    for g in range(N_KV):
        acc_ref[0, g] = _dot(vcmpT_ref[0, g * HEAD_DIM:(g + 1) * HEAD_DIM, :],
                             pnb[:, g * GCOL:(g + 1) * GCOL])

    imp, valid = [], []
    for g in range(N_KV):
        a = pn[:, g * GCOL:g * GCOL + QT]
        for h in range(1, HPG):
            a = a + pn[:, g * GCOL + h * QT:g * GCOL + (h + 1) * QT]
        imp.append(a)
        valid.append(lgb[:, g * GCOL:g * GCOL + QT] > VALID_THRESH)
    imp = jnp.concatenate(imp, axis=1)
    valid = jnp.concatenate(valid, axis=1)
    jrow = lax.broadcasted_iota(I32, (nb, N_KV * QT), 0)
    tpos = i * QT + (lax.broadcasted_iota(I32, (nb, N_KV * QT), 1) & (QT - 1))
    score = jnp.where(valid, imp, jnp.where(jrow * BLK <= tpos, 2.0, -1.0))
    sel = _select_topk(score, tri_ref[...], 0, min(N_SEL, nb))
    selbias = jnp.where(jnp.logical_and(sel, score >= 0.0), 0.0, NEG).astype(BF)
    for gh in range(N_HEADS):
        g = gh // HPG
        rhs_ref[D_KV:D_KV + nb, gh * QT:(gh + 1) * QT] = selbias[:, g * QT:(g + 1) * QT]

    for br in (1, 2):
        m_ref[br] = jnp.full((1, NCOL), M_INIT, F32)
        l_ref[br] = jnp.zeros((1, NCOL), F32)
        acc_ref[br] = jnp.zeros((N_KV, HEAD_DIM, GCOL), F32)

    def sel_tile(kt, bias_tile):
        k0 = pl.multiple_of(kt * KT, KT)
        k = kslc_ref[0, pl.ds(k0, KT), :]
        jj = lax.broadcasted_iota(I32, (KT, nb), 1)
        kk = lax.broadcasted_iota(I32, (KT, nb), 0)
        onehot = jnp.where(jj == (KT // BLK) * kt + jnp.right_shift(kk, BLK_SHIFT), 1.0, 0.0).astype(BF)
        s = _dot(jnp.concatenate([k, onehot], axis=1), rhs_ref[...])
        if bias_tile is not None:
            s = s + bias_tile
        _online_update(1, s, vslcT_ref[0, :, pl.ds(k0, KT)], m_ref, l_ref, acc_ref)

    def far_body(kt, c):
        sel_tile(kt, None)
        return c
    lax.fori_loop(0, i - 1, far_body, 0)

    @pl.when(i >= 1)
    def _():
        sel_tile(i - 1, sb_ref[0:KT, :])
    sel_tile(i, sb_ref[KT:2 * KT, :])

    n_win = WINDOW // KT + 1
    for r in range(n_win):
        kt = i - (n_win - 1) + r

        @pl.when(kt >= 0)
        def _():
            k0 = pl.multiple_of(kt * KT, KT)
            s = _dot(kwin_ref[0, pl.ds(k0, KT), :], rhs_ref[0:D_KV, :])
            if r == 0:
                s = s + w0_ref[...]
            elif r >= n_win - 2:
                s = s + sb_ref[(r - (n_win - 2)) * KT:(r - (n_win - 3)) * KT, :]
            _online_update(2, s, vwinT_ref[0, :, pl.ds(k0, KT)], m_ref, l_ref, acc_ref)

    gts = gates_ref[0, 0]
    inv = [None] + [jnp.where(l_ref[br] > 0.0, 1.0 / l_ref[br], 0.0) for br in (1, 2)]
    for g in range(N_KV):
        cs = slice(g * GCOL, (g + 1) * GCOL)
        o = gts[0:1, cs] * acc_ref[0, g]
        for br in (1, 2):
            o = o + (gts[br:br + 1, cs] * inv[br][:, cs]) * acc_ref[br, g]
        for hp in range(HPG // 2):
            pair = jnp.concatenate([o[:, (2 * hp) * QT:(2 * hp + 1) * QT],
                                    o[:, (2 * hp + 1) * QT:(2 * hp + 2) * QT]], axis=0)
            c0 = g * HPG * HEAD_DIM + hp * 2 * HEAD_DIM
            o_ref[0, :, c0:c0 + 2 * HEAD_DIM] = pair.T.astype(BF)


def _nsa_prompt(qT, gates, kslc, vslcT, kwin, vwinT, kcmp, vcmpT, tables, batch, seq):
    nb = seq // BLK
    nq = seq // QT
    cb, sb, w0, tri = tables
    dq = qT.shape[1]
    per_b = lambda shape: pl.BlockSpec((1,) + shape, lambda b, i: (b, 0, 0))
    return pl.pallas_call(
        functools.partial(_nsa_prompt_kernel, nb=nb),
        grid=(batch, nq),
        in_specs=[pl.BlockSpec((1, dq, QT), lambda b, i: (b, 0, i)),
                  pl.BlockSpec((1, 1, 8, NCOL), lambda b, i: (b, i, 0, 0)),
                  per_b((seq, D_KV)), per_b((D_KV, seq)), per_b((seq, D_KV)), per_b((D_KV, seq)),
                  per_b((nb, D_KV)), per_b((D_KV, nb)),
                  _const_spec(cb.shape), _const_spec(sb.shape), _const_spec(w0.shape),
                  _const_spec(tri.shape)],
        out_specs=pl.BlockSpec((1, QT, dq), lambda b, i: (b, i, 0)),
        out_shape=jax.ShapeDtypeStruct((batch, seq, dq), BF),
        scratch_shapes=[pltpu.VMEM((D_KV + nb, NCOL), BF),
                        pltpu.VMEM((3, 1, NCOL), F32),
                        pltpu.VMEM((3, 1, NCOL), F32),
                        pltpu.VMEM((3, N_KV, HEAD_DIM, GCOL), F32)],
        compiler_params=pltpu.CompilerParams(dimension_semantics=("arbitrary", "arbitrary"),
                                             vmem_limit_bytes=V7X_VMEM_LIMIT),
        name="nsa_prompt",
    )(qT, gates, kslc, vslcT, kwin, vwinT, kcmp, vcmpT, cb, sb, w0, tri)


def _merge_kernel(x_ref, o_ref, sza_ref, yc_ref, gate_ref, woa_ref, woc_ref, fg_ref, y_ref):
    og = (o_ref[...].astype(F32) * sza_ref[...].astype(F32)).astype(BF)
    acc = _dot(og, woa_ref[...]) + _dot(yc_ref[...], woc_ref[...])
    xo = x_ref[...] + gate_ref[0] * acc
    ms = jnp.mean(xo * xo, axis=-1, keepdims=True)
    y_ref[...] = xo * lax.rsqrt(ms + EPS) * fg_ref[...]


def _merge(x2d, o2d, sza, yc, gate, woa, woc, fg, tm, rows_per_gate):
    tokens, d = x2d.shape
    tok = lambda w: pl.BlockSpec((tm, w), lambda t: (t, 0))
    gr = gate.shape[1]
    return pl.pallas_call(
        _merge_kernel,
        grid=(tokens // tm,),
        in_specs=[tok(d), tok(o2d.shape[1]), tok(sza.shape[1]), tok(yc.shape[1]),
                  pl.BlockSpec((1, gr, d), lambda t: ((t * tm) // rows_per_gate, 0, 0)),
                  _const_spec(woa.shape), _const_spec(woc.shape), _const_spec(fg.shape)],
        out_specs=tok(d),
        out_shape=jax.ShapeDtypeStruct((tokens, d), F32),
        compiler_params=pltpu.CompilerParams(dimension_semantics=("arbitrary",),
                                             vmem_limit_bytes=V7X_VMEM_LIMIT),
        name="merge_out",
    )(x2d, o2d, sza, yc, gate, woa, woc, fg)


def _softmax_rows(parts):
    m = parts[0].max(axis=1, keepdims=True)
    for s in parts[1:]:
        m = jnp.maximum(m, s.max(axis=1, keepdims=True))
    ps = [jnp.exp(s - m) for s in parts]
    l = ps[0].sum(axis=1, keepdims=True)
    for p in ps[1:]:
        l = l + p.sum(axis=1, keepdims=True)
    inv = jnp.where(jnp.logical_and(m > VALID_THRESH, l > 0.0), 1.0 / l, 0.0)
    return ps, inv


def _nsa_sample_kernel(pt_ref, ckc_ref, cvc_ref, cks_ref, cvs_ref,
                       qbd_ref, gates_ref, tails_ref, bkw_ref, bvw_ref,
                       cbs_ref, sbl_ref, sbt_ref, wbl_ref, wbt_ref, eloc_ref, tri_ref,
                       pek_ref, w1k_ref, w2k_ref, pev_ref, w1v_ref, w2v_ref,
                       o_ref, skw_ref, svw_ref,
                       buf_ref, sem, tailblk_ref, tailkv_ref, kcmp_ref, vcmp_ref, sbuf_ref, pbuf_ref,
                       *, n_pages, page, n_dec):
    b = pl.program_id(0)
    nbatch = pl.num_programs(0)
    past = n_pages * page
    nbp = past // BLK
    nbc = kcmp_ref.shape[0]
    n_tiles = past // SAMPLE_KEY_TILE
    nrow = qbd_ref.shape[1]
    caches = (ckc_ref, cvc_ref, cks_ref, cvs_ref)

    def page_copy(cache_ref, pg, j, slot):
        return pltpu.make_async_copy(cache_ref.at[pg],
                                     buf_ref.at[slot, pl.ds(pl.multiple_of(j * page, page), page)],
                                     sem.at[slot])

    def start_fetch(cache_ref, bb, slot):
        def body(j, c):
            page_copy(cache_ref, pt_ref[bb, j], j, slot).start()
            return c
        lax.fori_loop(0, n_pages, body, 0)

    def wait_fetch(cache_ref, slot):
        def body(j, c):
            page_copy(cache_ref, 0, j, slot).wait()
            return c
        lax.fori_loop(0, n_pages, body, 0)

    def stage(c):
        if c + 1 < len(caches):
            start_fetch(caches[c + 1], b, (c + 1) % 2)
        else:
            @pl.when(b + 1 < nbatch)
            def _():
                start_fetch(caches[0], b + 1, 0)
        wait_fetch(caches[c], c % 2)
        return buf_ref.at[c % 2]

    @pl.when(b == 0)
    def _():
        start_fetch(caches[0], 0, 0)

    qbd = qbd_ref[0]

    def tail_tile(n):
        tailkv_ref[...] = jnp.zeros(tailkv_ref.shape, F32)
        tailkv_ref[0:8, :] = tails_ref[0, n]
        return tailkv_ref[...].astype(BF)

    def compress_cache(c, slot_ref, pe_ref, w1_ref, w2_ref, dst_ref):
        main = _compress(slot_ref, nbp, pe_ref, w1_ref, w2_ref)
        tailblk_ref[...] = jnp.zeros(tailblk_ref.shape, F32)
        tailblk_ref[0:8, :] = tails_ref[0, c]
        tail = _compress(tailblk_ref, TAIL_BLOCKS, pe_ref, w1_ref, w2_ref)
        dst_ref[0:nbp, :] = main.astype(BF)
        dst_ref[nbp:nbp + TAIL_BLOCKS, :] = tail.astype(BF)
        dst_ref[nbp + TAIL_BLOCKS:nbc, :] = jnp.zeros((nbc - nbp - TAIL_BLOCKS, D_KV), BF)

    compress_cache(0, stage(0), pek_ref, w1k_ref, w2k_ref, kcmp_ref)
    compress_cache(1, stage(1), pev_ref, w1v_ref, w2v_ref, vcmp_ref)

    lgc = _dot_nt(qbd, kcmp_ref[...]) + cbs_ref[...]
    (pc,), invc = _softmax_rows([lgc])
    pn = pc * invc
    o_cmp = _dot(pn.astype(BF), vcmp_ref[...])
    imp = jnp.sum(pn.reshape(nrow // HPG, HPG, nbc), axis=1)
    jcol = lax.broadcasted_iota(I32, imp.shape, 1)
    score = jnp.where(jcol < nbp, imp, jnp.where(jcol == nbp, 2.0, -1.0))
    sel = _select_topk(score, tri_ref[...], 1, N_SEL)
    selbias = jnp.where(jnp.logical_and(sel, score >= 0.0), 0.0, NEG)
    selb = jnp.broadcast_to(selbias[:, None, :], (nrow // HPG, HPG, nbc)).reshape(nrow, nbc).astype(BF)

    kslot = stage(2)
    for t in range(n_tiles):
        kt = kslot[t * SAMPLE_KEY_TILE:(t + 1) * SAMPLE_KEY_TILE, :].astype(BF)
        sbuf_ref[t] = (_dot_nt(qbd, kt)
                       + _dot_nt(selb[:, t * LANES:(t + 1) * LANES], eloc_ref[...]))
    last = slice(SAMPLE_KEY_TILE - LANES, SAMPLE_KEY_TILE)
    sbuf_ref[n_tiles - 1, :, last] = sbuf_ref[n_tiles - 1, :, last] + sbl_ref[...]
    s_tail = (_dot_nt(qbd, tail_tile(2))
              + _dot_nt(selb[:, nbp:nbp + LANES], eloc_ref[0:LANES, :]) + sbt_ref[...])
    ps, inv_sel = _softmax_rows([sbuf_ref[t] for t in range(n_tiles)] + [s_tail])
    for t in range(n_tiles):
        pbuf_ref[t] = ps[t].astype(BF)
    p_tail = ps[n_tiles].astype(BF)

    vslot = stage(3)
    o_sel = _dot(p_tail, tail_tile(3))
    for t in range(n_tiles):
        o_sel = o_sel + _dot(pbuf_ref[t], vslot[t * SAMPLE_KEY_TILE:(t + 1) * SAMPLE_KEY_TILE, :].astype(BF))
    o_sel = o_sel * inv_sel

    s_w = _dot_nt(qbd, bkw_ref[0].astype(BF)) + wbl_ref[...]
    s_wt = _dot_nt(qbd, tail_tile(4)) + wbt_ref[...]
    (p_w, p_wt), inv_w = _softmax_rows([s_w, s_wt])
    o_win = (_dot(p_w.astype(BF), bvw_ref[0].astype(BF)) + _dot(p_wt.astype(BF), tail_tile(5))) * inv_w

    o_ref[0] = gates_ref[0, 0] * o_cmp + gates_ref[0, 1] * o_sel + gates_ref[0, 2] * o_win

    nbuf = bkw_ref.shape[1]
    for n, src, dst in ((4, bkw_ref, skw_ref), (5, bvw_ref, svw_ref)):
        dst[0, 0:nbuf - n_dec, :] = src[0, n_dec:nbuf, :]
        dst[0, nbuf - n_dec:nbuf, :] = tails_ref[0, n, 0:n_dec, :]


def _nsa_sample(page_table, caches, qbd, gates3, tails, bkw, bvw, tables, cw, n_dec):
    nbatch, n_pages = page_table.shape
    page = caches[0].shape[1]
    past = n_pages * page
    cbs, sbl, sbt, wbl, wbt, eloc, tri = tables
    nbc = cbs.shape[1]
    nrow = qbd.shape[1]
    nbuf = bkw.shape[1]
    n_tiles = past // SAMPLE_KEY_TILE
    assert past % SAMPLE_KEY_TILE == 0 and (past // BLK) % LANES == 0
    per_b = lambda a: pl.BlockSpec((1,) + a.shape[1:], lambda b, pt: (b,) + (0,) * (a.ndim - 1))
    const = lambda a: pl.BlockSpec(a.shape, lambda b, pt: (0,) * a.ndim, pipeline_mode=pl.Buffered(1))
    hbm = pl.BlockSpec(memory_space=pl.ANY)
    grid_spec = pltpu.PrefetchScalarGridSpec(
        num_scalar_prefetch=1,
        grid=(nbatch,),
        in_specs=[hbm] * 4 + [per_b(qbd), per_b(gates3), per_b(tails), per_b(bkw), per_b(bvw)]
                 + [const(a) for a in tables] + [const(a) for a in cw],
        out_specs=[pl.BlockSpec((1, nrow, D_KV), lambda b, pt: (b, 0, 0)),
                   pl.BlockSpec((1, nbuf, D_KV), lambda b, pt: (b, 0, 0)),
                   pl.BlockSpec((1, nbuf, D_KV), lambda b, pt: (b, 0, 0))],
        scratch_shapes=[pltpu.VMEM((2, past, D_KV), F32),
                        pltpu.SemaphoreType.DMA((2,)),
                        pltpu.VMEM((TAIL_BLOCKS * BLK, D_KV), F32),
                        pltpu.VMEM((LANES, D_KV), F32),
                        pltpu.VMEM((nbc, D_KV), BF),
                        pltpu.VMEM((nbc, D_KV), BF),
                        pltpu.VMEM((n_tiles, nrow, SAMPLE_KEY_TILE), F32),
                        pltpu.VMEM((n_tiles, nrow, SAMPLE_KEY_TILE), BF)])
    return pl.pallas_call(
        functools.partial(_nsa_sample_kernel, n_pages=n_pages, page=page, n_dec=n_dec),
        grid_spec=grid_spec,
        out_shape=[jax.ShapeDtypeStruct((nbatch, nrow, D_KV), F32),
                   jax.ShapeDtypeStruct((nbatch, nbuf, D_KV), F32),
                   jax.ShapeDtypeStruct((nbatch, nbuf, D_KV), F32)],
        compiler_params=pltpu.CompilerParams(dimension_semantics=("arbitrary",),
                                             vmem_limit_bytes=V7X_VMEM_LIMIT),
        name="nsa_sample",
    )(page_table, *caches, qbd, gates3, tails, bkw, bvw, *tables, *cw)


def _t5_bucket_np(n):
    d = np.maximum(np.arange(n), 0)
    max_exact = N_BUCKETS // 2
    nf = np.maximum(d, max_exact).astype(np.float32)
    large = max_exact + (np.log(nf / np.float32(max_exact)) / np.float32(math.log(MAX_DIST / max_exact))
                         * np.float32(N_BUCKETS - max_exact)).astype(np.int32)
    return np.where(d < max_exact, d, np.minimum(large, N_BUCKETS - 1))


N_NEAR = 256
_BUCKETS = _t5_bucket_np(N_NEAR)
assert np.all(_BUCKETS[QT + 1:] == N_BUCKETS - 1)


def _shifted_bias(rel_bias):
    return rel_bias[_BUCKETS[:N_NEAR]] - rel_bias[N_BUCKETS - 1][None, :]


def _bias_cols(tp, dist):
    val = tp[np.clip(dist, 0, N_NEAR - 1)]
    val = jnp.where((dist >= 0)[:, :, None], val, NEG)
    return jnp.transpose(val, (0, 2, 1)).reshape(dist.shape[0], -1)


def _prompt_tables(rel_bias, nb):
    tp = _shifted_bias(rel_bias)
    q = np.arange(QT)[None, :]
    r = np.arange(-1, 7)[:, None]
    d_cmp = np.where(r <= 2, BLK * r - (BLK - 1) + q, -1)
    cb = _bias_cols(tp, d_cmp)
    kk = np.arange(2 * KT)[:, None]
    sb = _bias_cols(tp, q + KT - kk)
    kk = np.arange(KT)[:, None]
    w0 = jnp.tile(jnp.asarray(np.where(kk >= q, 0.0, NEG), F32), (1, N_HEADS))
    idx = np.arange(nb)
    tri = jnp.asarray(idx[None, :] < idx[:, None], BF)
    return cb, sb, w0, tri


def _sample_rows_bias(tp, dist):
    val = tp[np.clip(dist, 0, N_NEAR - 1)]
    val = jnp.where((dist >= 0)[:, :, None], val, NEG)
    val = val.reshape(dist.shape[0], dist.shape[1], N_KV, HPG)
    return jnp.transpose(val, (2, 0, 3, 1)).reshape(-1, dist.shape[1])


def _sample_tables(rel_bias, past, n_dec, nbuf, nbc):
    tp = _shifted_bias(rel_bias)
    q = np.arange(n_dec)[:, None]
    nbp = past // BLK
    j = np.arange(nbc)[None, :]
    cbs = _sample_rows_bias(tp, np.where(j < nbp, past + q - (BLK * j + BLK - 1), -1))
    kk = np.arange(LANES)[None, :]
    sbl = _sample_rows_bias(tp, q + LANES - kk)
    sbt = _sample_rows_bias(tp, q - kk)
    r = np.arange(nbuf)[None, :]
    d_w = nbuf + q - r
    wbl = _sample_rows_bias(tp, np.where(d_w <= WINDOW, d_w, -1))
    wbt = sbt
    kk = np.arange(SAMPLE_KEY_TILE)[:, None]
    eloc = jnp.asarray(kk // BLK == np.arange(LANES)[None, :], BF)
    idx = np.arange(nbc)
    tri = jnp.asarray(idx[:, None] < idx[None, :], BF)
    return cbs, sbl, sbt, wbl, wbt, eloc, tri


def _compress_weights(pe, w1, w2):
    pe2 = jnp.tile(pe, (1, N_KV))
    w1r = w1.reshape(BLK, HEAD_DIM, CMP_HID)
    eye = jnp.eye(N_KV, dtype=w1.dtype)
    w1bd = jnp.einsum("pdh,gk->pgdkh", w1r, eye).reshape(BLK, D_KV, N_KV * CMP_HID).astype(BF)
    w2bd = jnp.einsum("hd,gk->ghkd", w2, eye).reshape(N_KV * CMP_HID, D_KV).astype(BF)
    return pe2, w1bd, w2bd


def _split_w_in(w_in, d_attn, d_conv, d_mix):
    sizes = (d_attn,) + (D_KV,) * 6 + (3 * N_HEADS,) + (d_conv,) * 3 + (d_mix,)
    offs = np.cumsum((0,) + sizes)
    wq = w_in[:, offs[0]:offs[1]]
    wkv = w_in[:, offs[1]:offs[7]]
    wg = jnp.pad(w_in[:, offs[7]:offs[8]], ((0, 0), (0, LANES - 3 * N_HEADS)))
    wc = w_in[:, offs[8]:offs[11]]
    wz = w_in[:, offs[11]:offs[12]]
    return tuple(w.astype(BF) for w in (wq, wkv, wg, wc, wz))


def kernel(x_prompt, x_sample, cache_k_cmp, cache_v_cmp, cache_k_slc, cache_v_slc, cache_k_win,
           cache_v_win, state_conv, page_table, c_prompt, c_sample, ada_w, ada_b, norm_g, w_in,
           cmp_pe_k, cmp_w1_k, cmp_w2_k, cmp_pe_v, cmp_w1_v, cmp_w2_v, conv_w, w_out, rel_bias, final_g):
    depth = ada_w.shape[0]
    assert depth == 1
    batch, seq, d = x_prompt.shape
    nbatch, n_dec, _ = x_sample.shape
    d_attn = N_HEADS * HEAD_DIM
    d_conv = conv_w.shape[2]
    d_mix = d_attn + d_conv
    n_pages = page_table.shape[1]
    page = cache_k_cmp.shape[2]
    past = n_pages * page
    nbuf = cache_k_win.shape[2]
    tm = 512
    assert seq % tm == 0 and seq >= WINDOW and nbuf == WINDOW

    wts = _split_w_in(w_in[0], d_attn, d_conv, d_mix)
    cwk = _compress_weights(cmp_pe_k[0], cmp_w1_k[0], cmp_w2_k[0])
    cwv = _compress_weights(cmp_pe_v[0], cmp_w1_v[0], cmp_w2_v[0])
    woa = w_out[0, :d_attn].astype(BF)
    woc = w_out[0, d_attn:].astype(BF)
    ng = norm_g[0].reshape(1, d)
    fg = final_g.reshape(1, d)
    cw = conv_w[0]

    n_c = batch + nbatch
    c_all = jnp.pad(jnp.concatenate([c_prompt, c_sample], axis=0), ((0, (-n_c) % 8), (0, 0)))
    mod = _ada(c_all, ada_w[0], ada_b[0])
    shift, scale, gate = mod[:, :d], mod[:, d:2 * d], mod[:, 2 * d:]

    x2d = x_prompt.reshape(batch * seq, d)
    sc_p = (1.0 + scale[:batch]).reshape(batch, 1, d)
    sh_p = shift[:batch].reshape(batch, 1, d)
    (qT, kc, vc, ks, vs, kw, vw, kslc, vslcT, kwin, vwinT, gates_tok, yc, sza, cv_p) = _proj_prompt(
        x2d, sc_p, sh_p, ng, wts, cw, batch, seq, tm)
    kcmp, vcmpT = _compress_prompt(kc.reshape(batch, seq, D_KV), vc.reshape(batch, seq, D_KV),
                                   cwk + cwv, batch, seq)
    nq = seq // QT
    gates = gates_tok[:, :3 * N_HEADS].reshape(batch, nq, QT, 3, N_HEADS)
    gates = jnp.transpose(gates, (0, 1, 3, 4, 2)).reshape(batch, nq, 3, NCOL)
    gates = jnp.pad(gates, ((0, 0), (0, 0), (0, 5), (0, 0)))
    o_p = _nsa_prompt(qT, gates, kslc, vslcT, kwin, vwinT, kcmp, vcmpT,
                      _prompt_tables(rel_bias, seq // BLK), batch, seq)
    y_p = _merge(x2d, o_p.reshape(batch * seq, d_attn), sza, yc, gate[:batch].reshape(batch, 1, d),
                 woa, woc, fg, tm, seq)
    y_prompt = y_p.reshape(batch, seq, d)
    heads = lambda a: a.reshape(1, batch, seq, N_KV, HEAD_DIM)
    p_states = (heads(kc), heads(vc), heads(ks), heads(vs),
                heads(kw)[:, :, seq - WINDOW:], heads(vw)[:, :, seq - WINDOW:], cv_p[None])

    tok = n_dec * nbatch
    xs2d = jnp.transpose(x_sample, (1, 0, 2)).reshape(tok, d)
    rep = lambda a: jnp.tile(a, (n_dec, 1)).reshape(1, tok, d)
    st2d = jnp.transpose(state_conv[0], (1, 0, 2)).reshape((CONV_W - 1) * nbatch, d_conv)
    q_s, kv_s, gates_s, yc_s, sza_s, cv_s = _proj_sample(
        xs2d, rep(1.0 + scale[batch:n_c]), rep(shift[batch:n_c]), ng, wts, cw, st2d, nbatch, n_dec)

    q5 = jnp.transpose(q_s.reshape(n_dec, nbatch, N_KV, HPG, HEAD_DIM), (1, 2, 0, 3, 4))
    eye = jnp.eye(N_KV, dtype=F32)
    qbd = (q5[:, :, :, :, None, :] * eye[None, :, None, None, :, None]).reshape(
        nbatch, N_KV * n_dec * HPG, D_KV).astype(BF)
    g5 = gates_s[:, :3 * N_HEADS].reshape(n_dec, nbatch, 3, N_KV, HPG)
    g5 = jnp.transpose(g5, (1, 2, 3, 0, 4)).reshape(nbatch, 3, N_KV * n_dec * HPG, 1)
    gates3 = jnp.broadcast_to(g5, (nbatch, 3, N_KV * n_dec * HPG, D_KV))
    kv6 = jnp.transpose(kv_s.reshape(n_dec, nbatch, 6, D_KV), (1, 2, 0, 3))
    tails = jnp.pad(kv6, ((0, 0), (0, 0), (0, 8 - n_dec), (0, 0)))
    nbc = -(-(past // BLK + TAIL_BLOCKS) // LANES) * LANES
    caches = tuple(c[0].reshape(c.shape[1], page, D_KV)
                   for c in (cache_k_cmp, cache_v_cmp, cache_k_slc, cache_v_slc))
    o_s, s_kw, s_vw = _nsa_sample(
        page_table, caches, qbd, gates3, tails,
        cache_k_win[0].reshape(nbatch, nbuf, D_KV), cache_v_win[0].reshape(nbatch, nbuf, D_KV),
        _sample_tables(rel_bias, past, n_dec, nbuf, nbc), cwk + cwv, n_dec)
    o6 = o_s.reshape(nbatch, N_KV, n_dec, HPG, N_KV, HEAD_DIM)
    o_diag = jnp.stack([o6[:, g, :, :, g, :] for g in range(N_KV)], axis=1)
    o_tok = jnp.transpose(o_diag, (2, 0, 1, 3, 4)).reshape(tok, d_attn).astype(BF)
    y_s = _merge(xs2d, o_tok, sza_s, yc_s, rep(gate[batch:n_c]), woa, woc, fg, tok, tok)
    y_sample = jnp.transpose(y_s.reshape(n_dec, nbatch, d), (1, 0, 2))
    new = lambda n: kv6[:, n].reshape(1, nbatch, n_dec, N_KV, HEAD_DIM)
    win = lambda a: a.reshape(1, nbatch, nbuf, N_KV, HEAD_DIM)
    cv_out = jnp.transpose(cv_s.reshape(CONV_W - 1, nbatch, d_conv), (1, 0, 2))[None]
    s_states = (new(0), new(1), new(2), new(3), win(s_kw), win(s_vw), cv_out)

    return (y_prompt, y_sample) + p_states + s_states
```

```python
import functools
import math

import numpy as np
import jax
import jax.numpy as jnp
from jax import lax
from jax.experimental import pallas as pl
from jax.experimental.pallas import tpu as pltpu

BF = jnp.bfloat16
F32 = jnp.float32
I32 = jnp.int32

HEAD_DIM = 64
N_KV = 2
HPG = 8
N_HEADS = N_KV * HPG
D_KV = N_KV * HEAD_DIM
BLK = 64
BLK_SHIFT = 6
TAIL_BLOCKS = 16
N_SEL = 16
WINDOW = 512
CMP_HID = 2 * HEAD_DIM
CONV_W = 3
N_BUCKETS = 32
MAX_DIST = 128
EPS = 1e-6
NEG = -1e30
M_INIT = -5e29
VALID_THRESH = -1e29
LOG2E = math.log2(math.e)
QT = 128
KT = 128
FAR_GROUP = 4
ACC_ROWS = HEAD_DIM + 16
NCOL = N_HEADS * QT
GCOL = HPG * QT
LANES = 128
SAMPLE_KEY_TILE = 8192
V7X_VMEM_LIMIT = 56 * 1024 * 1024


def _dot(a, b):
    return jnp.dot(a, b, preferred_element_type=F32)


def _dot_nt(a, b):
    return lax.dot_general(a, b, (((1,), (1,)), ((), ())), preferred_element_type=F32)


def _sigmoid(x):
    return 1.0 / (1.0 + jnp.exp(-x))


def _silu(x):
    return x * _sigmoid(x)


def _ada_kernel(c_ref, w_ref, b_ref, o_ref):
    o_ref[...] = _dot(c_ref[...].astype(BF), w_ref[...].astype(BF)) + b_ref[...]


def _ada(c_all, ada_w, ada_b):
    rows, d = c_all.shape
    n = ada_w.shape[1]
    return pl.pallas_call(
        _ada_kernel,
        grid=(n // d,),
        in_specs=[pl.BlockSpec((rows, d), lambda j: (0, 0)),
                  pl.BlockSpec((d, d), lambda j: (0, j)),
                  pl.BlockSpec((1, d), lambda j: (0, j))],
        out_specs=pl.BlockSpec((rows, d), lambda j: (0, j)),
        out_shape=jax.ShapeDtypeStruct((rows, n), F32),
        name="ada_modulation",
    )(c_all, ada_w, ada_b.reshape(1, n))


def _proj_body(x_ref, sc_ref, sh_ref, ng_ref, wq_ref, wkv_ref, wg_ref, wc_ref, wz_ref, cw_ref):
    x = x_ref[...]
    ms = jnp.mean(x * x, axis=-1, keepdims=True)
    xn = x * lax.rsqrt(ms + EPS) * ng_ref[...]
    h = xn * sc_ref[0] + sh_ref[0]
    hb = h.astype(BF)
    q = _dot(hb, wq_ref[...]) * (HEAD_DIM ** -0.5 * LOG2E)
    kv = _dot(hb, wkv_ref[...])
    gates = _sigmoid(_dot(hb, wg_ref[...]))
    c3 = _dot(hb, wc_ref[...])
    dc = c3.shape[1] // 3
    hc, bc, cc = c3[:, :dc], c3[:, dc:2 * dc], c3[:, 2 * dc:]
    u = cc * hc
    z = _dot(hb, wz_ref[...])
    da = z.shape[1] - dc
    sza = _silu(z[:, :da])
    szc = _silu(z[:, da:])
    return q, kv, gates, u, bc, sza, szc


def _proj_prompt_kernel(x_ref, sc_ref, sh_ref, ng_ref, wq_ref, wkv_ref, wg_ref, wc_ref, wz_ref, cw_ref,
                        qT_ref, kc_ref, vc_ref, ks_ref, vs_ref, kw_ref, vw_ref,
                        kslc_ref, vslcT_ref, kwin_ref, vwinT_ref, gates_ref, yc_ref, sza_ref, cv_ref,
                        uext_ref, *, tiles_per_batch, tm):
    t = pl.program_id(0)
    q, kv, gates, u, bc, sza, szc = _proj_body(
        x_ref, sc_ref, sh_ref, ng_ref, wq_ref, wkv_ref, wg_ref, wc_ref, wz_ref, cw_ref)
    qT_ref[0] = q.T.astype(BF)
    for n, r in enumerate((kc_ref, vc_ref, ks_ref, vs_ref, kw_ref, vw_ref)):
        r[...] = kv[:, n * D_KV:(n + 1) * D_KV]
    kslc_ref[0] = kv[:, 2 * D_KV:3 * D_KV].astype(BF)
    vslcT_ref[0] = kv[:, 3 * D_KV:4 * D_KV].T.astype(BF)
    kwin_ref[0] = kv[:, 4 * D_KV:5 * D_KV].astype(BF)
    vwinT_ref[0] = kv[:, 5 * D_KV:6 * D_KV].T.astype(BF)
    gates_ref[...] = gates

    first = (t % tiles_per_batch) == 0

    @pl.when(first)
    def _():
        uext_ref[0:8, :] = jnp.zeros((8, u.shape[1]), F32)

    @pl.when(jnp.logical_not(first))
    def _():
        uext_ref[0:8, :] = uext_ref[tm:tm + 8, :]

    uext_ref[8:8 + tm, :] = u
    y = (cw_ref[2:3, :] * u + cw_ref[1:2, :] * uext_ref[7:7 + tm, :]
         + cw_ref[0:1, :] * uext_ref[6:6 + tm, :])
    yc_ref[...] = (bc * y * szc).astype(BF)
    sza_ref[...] = sza.astype(BF)
    cv_ref[0] = u[tm - (CONV_W - 1):, :]


def _proj_sample_kernel(x_ref, sc_ref, sh_ref, ng_ref, wq_ref, wkv_ref, wg_ref, wc_ref, wz_ref, cw_ref,
                        st_ref, q_ref, kv_ref, gates_ref, yc_ref, sza_ref, cv_ref, uext_ref, *, nb, nt):
    q, kv, gates, u, bc, sza, szc = _proj_body(
        x_ref, sc_ref, sh_ref, ng_ref, wq_ref, wkv_ref, wg_ref, wc_ref, wz_ref, cw_ref)
    q_ref[...] = q
    kv_ref[...] = kv
    gates_ref[...] = gates
    ns = (CONV_W - 1) * nb
    uext_ref[0:ns, :] = st_ref[...]
    uext_ref[ns:ns + nt * nb, :] = u
    y = cw_ref[2:3, :] * u
    for k in range(CONV_W - 1):
        y = y + cw_ref[k:k + 1, :] * uext_ref[k * nb:k * nb + nt * nb, :]
    yc_ref[...] = (bc * y * szc).astype(BF)
    sza_ref[...] = sza.astype(BF)
    cv_ref[...] = uext_ref[nt * nb:nt * nb + ns, :]


def _const_spec(shape):
    nd = len(shape)
    return pl.BlockSpec(shape, lambda *_: (0,) * nd, pipeline_mode=pl.Buffered(1))


def _proj_prompt(x2d, sc, sh, ng, wts, cw, batch, seq, tm):
    tokens, d = x2d.shape
    tpb = seq // tm
    wq, wkv, wg, wc, wz = wts
    dc = wc.shape[1] // 3
    da = wz.shape[1] - dc
    tok_spec = lambda w: pl.BlockSpec((tm, w), lambda t: (t, 0))
    rowT_spec = lambda r: pl.BlockSpec((1, r, tm), lambda t: (t // tpb, 0, t % tpb))
    row_spec = lambda w: pl.BlockSpec((1, tm, w), lambda t: (t // tpb, t % tpb, 0))
    mod_spec = pl.BlockSpec((1, 1, d), lambda t: (t // tpb, 0, 0))
    out_shape = (
        [jax.ShapeDtypeStruct((batch, wq.shape[1], seq), BF)]
        + [jax.ShapeDtypeStruct((tokens, D_KV), F32)] * 6
        + [jax.ShapeDtypeStruct((batch, seq, D_KV), BF), jax.ShapeDtypeStruct((batch, D_KV, seq), BF),
           jax.ShapeDtypeStruct((batch, seq, D_KV), BF), jax.ShapeDtypeStruct((batch, D_KV, seq), BF),
           jax.ShapeDtypeStruct((tokens, LANES), F32),
           jax.ShapeDtypeStruct((tokens, dc), BF), jax.ShapeDtypeStruct((tokens, da), BF),
           jax.ShapeDtypeStruct((batch, CONV_W - 1, dc), F32)])
    out_specs = (
        [rowT_spec(wq.shape[1])] + [tok_spec(D_KV)] * 6
        + [row_spec(D_KV), rowT_spec(D_KV), row_spec(D_KV), rowT_spec(D_KV),
           tok_spec(LANES), tok_spec(dc), tok_spec(da),
           pl.BlockSpec((1, CONV_W - 1, dc), lambda t: (t // tpb, 0, 0))])
    return pl.pallas_call(
        functools.partial(_proj_prompt_kernel, tiles_per_batch=tpb, tm=tm),
        grid=(tokens // tm,),
        in_specs=[tok_spec(d), mod_spec, mod_spec, _const_spec((1, d)),
                  _const_spec(wq.shape), _const_spec(wkv.shape), _const_spec(wg.shape),
                  _const_spec(wc.shape), _const_spec(wz.shape), _const_spec(cw.shape)],
        out_specs=out_specs,
        out_shape=out_shape,
        scratch_shapes=[pltpu.VMEM((tm + 8, dc), F32)],
        compiler_params=pltpu.CompilerParams(dimension_semantics=("arbitrary",),
                                             vmem_limit_bytes=V7X_VMEM_LIMIT),
        name="proj_prompt",
    )(x2d, sc, sh, ng, wq, wkv, wg, wc, wz, cw)


def _proj_sample(x2d, sc, sh, ng, wts, cw, state2d, nb, nt):
    tokens, d = x2d.shape
    wq, wkv, wg, wc, wz = wts
    dc = wc.shape[1] // 3
    da = wz.shape[1] - dc
    ns = (CONV_W - 1) * nb
    full = lambda a: pl.BlockSpec(a.shape, lambda i: (0,) * a.ndim)
    ins = (x2d, sc, sh, ng, wq, wkv, wg, wc, wz, cw, state2d)
    out_shape = [jax.ShapeDtypeStruct((tokens, wq.shape[1]), F32),
                 jax.ShapeDtypeStruct((tokens, wkv.shape[1]), F32),
                 jax.ShapeDtypeStruct((tokens, LANES), F32),
                 jax.ShapeDtypeStruct((tokens, dc), BF), jax.ShapeDtypeStruct((tokens, da), BF),
                 jax.ShapeDtypeStruct((ns, dc), F32)]
    return pl.pallas_call(
        functools.partial(_proj_sample_kernel, nb=nb, nt=nt),
        grid=(1,),
        in_specs=[full(a) for a in ins],
        out_specs=[pl.BlockSpec(s.shape, lambda i: (0, 0)) for s in out_shape],
        out_shape=out_shape,
        scratch_shapes=[pltpu.VMEM((ns + tokens, dc), F32)],
        compiler_params=pltpu.CompilerParams(vmem_limit_bytes=V7X_VMEM_LIMIT),
        name="proj_sample",
    )(*ins)


def _compress(x_ref, nblk, pe_ref, w1_ref, w2_ref):
    def body(p, acc):
        xp = x_ref[pl.ds(p, nblk, stride=BLK), :] + pe_ref[pl.ds(p, 1), :]
        return acc + _dot(xp.astype(BF), w1_ref[p])
    acc = lax.fori_loop(0, BLK, body, jnp.zeros((nblk, N_KV * CMP_HID), F32))
    return _dot(_silu(acc).astype(BF), w2_ref[...])


def _compress_prompt_kernel(kc_ref, vc_ref, pek_ref, w1k_ref, w2k_ref, pev_ref, w1v_ref, w2v_ref,
                            kcmp_ref, vcmpT_ref, *, nblk):
    kcmp_ref[0] = _compress(kc_ref.at[0], nblk, pek_ref, w1k_ref, w2k_ref).astype(BF)
    vcmpT_ref[0] = _compress(vc_ref.at[0], nblk, pev_ref, w1v_ref, w2v_ref).T.astype(BF)


def _compress_prompt(kc, vc, cw, batch, seq):
    nblk = seq // BLK
    row = pl.BlockSpec((1, seq, D_KV), lambda b: (b, 0, 0))
    return pl.pallas_call(
        functools.partial(_compress_prompt_kernel, nblk=nblk),
        grid=(batch,),
        in_specs=[row, row] + [_const_spec(a.shape) for a in cw],
        out_specs=[pl.BlockSpec((1, nblk, D_KV), lambda b: (b, 0, 0)),
                   pl.BlockSpec((1, D_KV, nblk), lambda b: (b, 0, 0))],
        out_shape=[jax.ShapeDtypeStruct((batch, nblk, D_KV), BF),
                   jax.ShapeDtypeStruct((batch, D_KV, nblk), BF)],
        compiler_params=pltpu.CompilerParams(vmem_limit_bytes=V7X_VMEM_LIMIT),
        name="compress_prompt",
    )(kc, vc, *cw)


def _sortable_key(score):
    bits = pltpu.bitcast(score, I32)
    return jnp.where(bits < 0, bits ^ jnp.int32(0x7FFFFFFF), bits)


def _select_topk(score, tri, axis, k):
    key = _sortable_key(score)
    cnt = lambda m: jnp.sum(jnp.where(m, 1.0, 0.0), axis=axis, keepdims=True)
    int_min = jnp.int32(-2 ** 31)
    kf = float(k)
    thr = jnp.where(cnt(key >= 0) >= kf, jnp.int32(0), int_min)
    for b in range(30, -1, -1):
        cand = thr | jnp.int32(1 << b)
        thr = jnp.where(cnt(key >= cand) >= kf, cand, thr)
    gt = key > thr
    tie = key == thr
    tie_bf = jnp.where(tie, 1.0, 0.0).astype(BF)
    lower_ties = _dot(tri, tie_bf) if axis == 0 else _dot(tie_bf, tri)
    return jnp.logical_or(gt, jnp.logical_and(tie, lower_ties + cnt(gt) < kf))


def _attend(br, lhs, rhs_ref, krows, bias, vT, m_ref, acc_ref):
    s = _dot(lhs, rhs_ref[0:krows, :])
    if bias is not None:
        s = s + bias
    m_old = m_ref[br]
    m_new = jnp.maximum(m_old, jnp.max(s, axis=0, keepdims=True))
    alpha = jnp.exp2(m_old - m_new)
    pb = jnp.exp2(s - m_new).astype(BF)
    m_ref[br] = m_new
    ones = jnp.ones((ACC_ROWS - HEAD_DIM, lhs.shape[0]), BF)
    for g in range(N_KV):
        gs = slice(g * GCOL, (g + 1) * GCOL)
        v_aug = jnp.concatenate([vT[g * HEAD_DIM:(g + 1) * HEAD_DIM, :], ones], axis=0)
        acc_ref[br, g] = alpha[:, gs] * acc_ref[br, g] + _dot(v_aug, pb[:, gs])


def _nsa_prompt_kernel(qT_ref, gates_ref, kslc_ref, vslcT_ref, kwin_ref, vwinT_ref, kcmp_ref, vcmpT_ref,
                       cb_ref, sb_ref, w0_ref, tri_ref, o_ref,
                       rhs_ref, m_ref, acc_ref, *, nb):
    i = pl.program_id(1)

    for gh in range(N_HEADS):
        g = gh // HPG
        cols = slice(gh * QT, (gh + 1) * QT)
        rhs_ref[g * HEAD_DIM:(g + 1) * HEAD_DIM, cols] = qT_ref[0, gh * HEAD_DIM:(gh + 1) * HEAD_DIM, :]
        og = 1 - g
        rhs_ref[og * HEAD_DIM:(og + 1) * HEAD_DIM, cols] = jnp.zeros((HEAD_DIM, QT), BF)

    lg = _dot(kcmp_ref[0], rhs_ref[0:D_KV, :])
    rel = 2 * i - lax.broadcasted_iota(I32, (nb, NCOL), 0)
    bias = jnp.where(rel >= 3, 0.0, NEG)
    for rr in range(4):
        bias = jnp.where(rel == rr - 1, cb_ref[rr:rr + 1, :], bias)
    lgb = lg + bias
    mc = jnp.max(lgb, axis=0, keepdims=True)
    pc = jnp.exp2(lgb - mc)
    lc = jnp.sum(pc, axis=0, keepdims=True)
    pn = pc * jnp.where(mc > VALID_THRESH, 1.0 / lc, 0.0)
    pnb = pn.astype(BF)
    for g in range(N_KV):
        acc_ref[0, g, 0:HEAD_DIM, :] = _dot(vcmpT_ref[0, g * HEAD_DIM:(g + 1) * HEAD_DIM, :],
                                            pnb[:, g * GCOL:(g + 1) * GCOL])

    imp, valid = [], []
    for g in range(N_KV):
        a = pn[:, g * GCOL:g * GCOL + QT]
        for h in range(1, HPG):
            a = a + pn[:, g * GCOL + h * QT:g * GCOL + (h + 1) * QT]
        imp.append(a)
        valid.append(lgb[:, g * GCOL:g * GCOL + QT] > VALID_THRESH)
    imp = jnp.concatenate(imp, axis=1)
    valid = jnp.concatenate(valid, axis=1)
    jrow = lax.broadcasted_iota(I32, (nb, N_KV * QT), 0)
    tpos = i * QT + (lax.broadcasted_iota(I32, (nb, N_KV * QT), 1) & (QT - 1))
    score = jnp.where(valid, imp, jnp.where(jrow * BLK <= tpos, 2.0, -1.0))
    sel = _select_topk(score, tri_ref[...], 0, min(N_SEL, nb))
    selbias = jnp.where(jnp.logical_and(sel, score >= 0.0), 0.0, NEG).astype(BF)
    for gh in range(N_HEADS):
        g = gh // HPG
        rhs_ref[D_KV:D_KV + nb, gh * QT:(gh + 1) * QT] = selbias[:, g * QT:(g + 1) * QT]

    for br in (1, 2):
        m_ref[br] = jnp.full((1, NCOL), M_INIT, F32)
        acc_ref[br] = jnp.zeros((N_KV, ACC_ROWS, GCOL), F32)

    def sel_tile(kt, tk, bias_tile):
        k0 = pl.multiple_of(kt * KT, KT)
        k = kslc_ref[0, pl.ds(k0, tk), :]
        jj = lax.broadcasted_iota(I32, (tk, nb), 1)
        kk = lax.broadcasted_iota(I32, (tk, nb), 0)
        onehot = jnp.where(jj == (KT // BLK) * kt + jnp.right_shift(kk, BLK_SHIFT), 1.0, 0.0).astype(BF)
        _attend(1, jnp.concatenate([k, onehot], axis=1), rhs_ref, D_KV + nb, bias_tile,
                vslcT_ref[0, :, pl.ds(k0, tk)], m_ref, acc_ref)

    n_far = jnp.maximum(i - 1, 0)
    n_big = n_far // FAR_GROUP

    def far_big(j, c):
        sel_tile(j * FAR_GROUP, FAR_GROUP * KT, None)
        return c
    lax.fori_loop(0, n_big, far_big, 0)

    def far_small(kt, c):
        sel_tile(kt, KT, None)
        return c
    lax.fori_loop(n_big * FAR_GROUP, n_far, far_small, 0)

    @pl.when(i >= 1)
    def _():
        sel_tile(i - 1, 2 * KT, sb_ref[...])

    @pl.when(i == 0)
    def _():
        sel_tile(0, KT, sb_ref[KT:2 * KT, :])

    n_win = WINDOW // KT + 1

    @pl.when(i >= n_win - 1)
    def _():
        k0 = pl.multiple_of((i - (n_win - 1)) * KT, KT)
        bias = jnp.concatenate(
            [w0_ref[...], jnp.zeros(((n_win - 3) * KT, NCOL), F32), sb_ref[...]], axis=0)
        _attend(2, kwin_ref[0, pl.ds(k0, n_win * KT), :], rhs_ref, D_KV, bias,
                vwinT_ref[0, :, pl.ds(k0, n_win * KT)], m_ref, acc_ref)

    @pl.when(i < n_win - 1)
    def _():
        for r in range(1, n_win):
            kt = i - (n_win - 1) + r

            @pl.when(kt >= 0)
            def _():
                k0 = pl.multiple_of(kt * KT, KT)
                r0 = (r - (n_win - 2)) * KT
                bias = sb_ref[r0:r0 + KT, :] if r >= n_win - 2 else None
                _attend(2, kwin_ref[0, pl.ds(k0, KT), :], rhs_ref, D_KV, bias,
                        vwinT_ref[0, :, pl.ds(k0, KT)], m_ref, acc_ref)

    gts = gates_ref[0, 0]
    for g in range(N_KV):
        cs = slice(g * GCOL, (g + 1) * GCOL)
        o = gts[0:1, cs] * acc_ref[0, g, 0:HEAD_DIM, :]
        for br in (1, 2):
            l = acc_ref[br, g, HEAD_DIM:HEAD_DIM + 1, :]
            o = o + (gts[br:br + 1, cs] * jnp.where(l > 0.0, 1.0 / l, 0.0)) * acc_ref[br, g, 0:HEAD_DIM, :]
        for hp in range(HPG // 2):
            pair = jnp.concatenate([o[:, (2 * hp) * QT:(2 * hp + 1) * QT],
                                    o[:, (2 * hp + 1) * QT:(2 * hp + 2) * QT]], axis=0)
            c0 = g * HPG * HEAD_DIM + hp * 2 * HEAD_DIM
            o_ref[0, :, c0:c0 + 2 * HEAD_DIM] = pair.T.astype(BF)


def _nsa_prompt(qT, gates, kslc, vslcT, kwin, vwinT, kcmp, vcmpT, tables, batch, seq):
    nb = seq // BLK
    nq = seq // QT
    cb, sb, w0, tri = tables
    dq = qT.shape[1]
    per_b = lambda shape: pl.BlockSpec((1,) + shape, lambda b, i: (b, 0, 0))
    return pl.pallas_call(
        functools.partial(_nsa_prompt_kernel, nb=nb),
        grid=(batch, nq),
        in_specs=[pl.BlockSpec((1, dq, QT), lambda b, i: (b, 0, i)),
                  pl.BlockSpec((1, 1, 8, NCOL), lambda b, i: (b, i, 0, 0)),
                  per_b((seq, D_KV)), per_b((D_KV, seq)), per_b((seq, D_KV)), per_b((D_KV, seq)),
                  per_b((nb, D_KV)), per_b((D_KV, nb)),
                  _const_spec(cb.shape), _const_spec(sb.shape), _const_spec(w0.shape),
                  _const_spec(tri.shape)],
        out_specs=pl.BlockSpec((1, QT, dq), lambda b, i: (b, i, 0)),
        out_shape=jax.ShapeDtypeStruct((batch, seq, dq), BF),
        scratch_shapes=[pltpu.VMEM((D_KV + nb, NCOL), BF),
                        pltpu.VMEM((3, 1, NCOL), F32),
                        pltpu.VMEM((3, N_KV, ACC_ROWS, GCOL), F32)],
        compiler_params=pltpu.CompilerParams(dimension_semantics=("arbitrary", "arbitrary"),
                                             vmem_limit_bytes=V7X_VMEM_LIMIT),
        name="nsa_prompt",
    )(qT, gates, kslc, vslcT, kwin, vwinT, kcmp, vcmpT, cb, sb, w0, tri)


def _merge_kernel(x_ref, o_ref, sza_ref, yc_ref, gate_ref, woa_ref, woc_ref, fg_ref, y_ref):
    og = (o_ref[...].astype(F32) * sza_ref[...].astype(F32)).astype(BF)
    acc = _dot(og, woa_ref[...]) + _dot(yc_ref[...], woc_ref[...])
    xo = x_ref[...] + gate_ref[0] * acc
    ms = jnp.mean(xo * xo, axis=-1, keepdims=True)
    y_ref[...] = xo * lax.rsqrt(ms + EPS) * fg_ref[...]


def _merge(x2d, o2d, sza, yc, gate, woa, woc, fg, tm, rows_per_gate):
    tokens, d = x2d.shape
    tok = lambda w: pl.BlockSpec((tm, w), lambda t: (t, 0))
    gr = gate.shape[1]
    return pl.pallas_call(
        _merge_kernel,
        grid=(tokens // tm,),
        in_specs=[tok(d), tok(o2d.shape[1]), tok(sza.shape[1]), tok(yc.shape[1]),
                  pl.BlockSpec((1, gr, d), lambda t: ((t * tm) // rows_per_gate, 0, 0)),
                  _const_spec(woa.shape), _const_spec(woc.shape), _const_spec(fg.shape)],
        out_specs=tok(d),
        out_shape=jax.ShapeDtypeStruct((tokens, d), F32),
        compiler_params=pltpu.CompilerParams(dimension_semantics=("arbitrary",),
                                             vmem_limit_bytes=V7X_VMEM_LIMIT),
        name="merge_out",
    )(x2d, o2d, sza, yc, gate, woa, woc, fg)


def _softmax_rows(parts):
    m = parts[0].max(axis=1, keepdims=True)
    for s in parts[1:]:
        m = jnp.maximum(m, s.max(axis=1, keepdims=True))
    ps = [jnp.exp2(s - m) for s in parts]
    l = ps[0].sum(axis=1, keepdims=True)
    for p in ps[1:]:
        l = l + p.sum(axis=1, keepdims=True)
    inv = jnp.where(jnp.logical_and(m > VALID_THRESH, l > 0.0), 1.0 / l, 0.0)
    return ps, inv


def _nsa_sample_kernel(pt_ref, ckc_ref, cvc_ref, cks_ref, cvs_ref,
                       qbd_ref, gates_ref, tails_ref, bkw_ref, bvw_ref,
                       cbs_ref, sbl_ref, sbt_ref, wbl_ref, wbt_ref, eloc_ref, tri_ref,
                       pek_ref, w1k_ref, w2k_ref, pev_ref, w1v_ref, w2v_ref,
                       o_ref, skw_ref, svw_ref,
                       buf_ref, sem, tailblk_ref, tailkv_ref, kcmp_ref, vcmp_ref, sbuf_ref, pbuf_ref,
                       *, n_pages, page, n_dec):
    b = pl.program_id(0)
    nbatch = pl.num_programs(0)
    past = n_pages * page
    nbp = past // BLK
    nbc = kcmp_ref.shape[0]
    n_tiles = past // SAMPLE_KEY_TILE
    nrow = qbd_ref.shape[1]
    caches = (ckc_ref, cvc_ref, cks_ref, cvs_ref)

    def page_copy(cache_ref, pg, j, slot):
        return pltpu.make_async_copy(cache_ref.at[pg],
                                     buf_ref.at[slot, pl.ds(pl.multiple_of(j * page, page), page)],
                                     sem.at[slot])

    def start_fetch(cache_ref, bb, slot):
        def body(j, c):
            page_copy(cache_ref, pt_ref[bb, j], j, slot).start()
            return c
        lax.fori_loop(0, n_pages, body, 0)

    def wait_fetch(cache_ref, slot):
        def body(j, c):
            page_copy(cache_ref, 0, j, slot).wait()
            return c
        lax.fori_loop(0, n_pages, body, 0)

    def stage(c):
        if c + 1 < len(caches):
            start_fetch(caches[c + 1], b, (c + 1) % 2)
        else:
            @pl.when(b + 1 < nbatch)
            def _():
                start_fetch(caches[0], b + 1, 0)
        wait_fetch(caches[c], c % 2)
        return buf_ref.at[c % 2]

    @pl.when(b == 0)
    def _():
        start_fetch(caches[0], 0, 0)

    qbd = qbd_ref[0]

    def tail_tile(n):
        tailkv_ref[...] = jnp.zeros(tailkv_ref.shape, F32)
        tailkv_ref[0:8, :] = tails_ref[0, n]
        return tailkv_ref[...].astype(BF)

    def compress_cache(c, slot_ref, pe_ref, w1_ref, w2_ref, dst_ref):
        main = _compress(slot_ref, nbp, pe_ref, w1_ref, w2_ref)
        tailblk_ref[...] = jnp.zeros(tailblk_ref.shape, F32)
        tailblk_ref[0:8, :] = tails_ref[0, c]
        tail = _compress(tailblk_ref, TAIL_BLOCKS, pe_ref, w1_ref, w2_ref)
        dst_ref[0:nbp, :] = main.astype(BF)
        dst_ref[nbp:nbp + TAIL_BLOCKS, :] = tail.astype(BF)
        dst_ref[nbp + TAIL_BLOCKS:nbc, :] = jnp.zeros((nbc - nbp - TAIL_BLOCKS, D_KV), BF)

    compress_cache(0, stage(0), pek_ref, w1k_ref, w2k_ref, kcmp_ref)
    compress_cache(1, stage(1), pev_ref, w1v_ref, w2v_ref, vcmp_ref)

    lgc = _dot_nt(qbd, kcmp_ref[...]) + cbs_ref[...]
    (pc,), invc = _softmax_rows([lgc])
    pn = pc * invc
    o_cmp = _dot(pn.astype(BF), vcmp_ref[...])
    imp = jnp.sum(pn.reshape(nrow // HPG, HPG, nbc), axis=1)
    jcol = lax.broadcasted_iota(I32, imp.shape, 1)
    score = jnp.where(jcol < nbp, imp, jnp.where(jcol == nbp, 2.0, -1.0))
    sel = _select_topk(score, tri_ref[...], 1, N_SEL)
    selbias = jnp.where(jnp.logical_and(sel, score >= 0.0), 0.0, NEG)
    selb = jnp.broadcast_to(selbias[:, None, :], (nrow // HPG, HPG, nbc)).reshape(nrow, nbc).astype(BF)

    kslot = stage(2)
    for t in range(n_tiles):
        kt = kslot[t * SAMPLE_KEY_TILE:(t + 1) * SAMPLE_KEY_TILE, :].astype(BF)
        sbuf_ref[t] = (_dot_nt(qbd, kt)
                       + _dot_nt(selb[:, t * LANES:(t + 1) * LANES], eloc_ref[...]))
    last = slice(SAMPLE_KEY_TILE - LANES, SAMPLE_KEY_TILE)
    sbuf_ref[n_tiles - 1, :, last] = sbuf_ref[n_tiles - 1, :, last] + sbl_ref[...]
    s_tail = (_dot_nt(qbd, tail_tile(2))
              + _dot_nt(selb[:, nbp:nbp + LANES], eloc_ref[0:LANES, :]) + sbt_ref[...])
    ps, inv_sel = _softmax_rows([sbuf_ref[t] for t in range(n_tiles)] + [s_tail])
    for t in range(n_tiles):
        pbuf_ref[t] = ps[t].astype(BF)
    p_tail = ps[n_tiles].astype(BF)

    vslot = stage(3)
    o_sel = _dot(p_tail, tail_tile(3))
    for t in range(n_tiles):
        o_sel = o_sel + _dot(pbuf_ref[t], vslot[t * SAMPLE_KEY_TILE:(t + 1) * SAMPLE_KEY_TILE, :].astype(BF))
    o_sel = o_sel * inv_sel

    s_w = _dot_nt(qbd, bkw_ref[0].astype(BF)) + wbl_ref[...]
    s_wt = _dot_nt(qbd, tail_tile(4)) + wbt_ref[...]
    (p_w, p_wt), inv_w = _softmax_rows([s_w, s_wt])
    o_win = (_dot(p_w.astype(BF), bvw_ref[0].astype(BF)) + _dot(p_wt.astype(BF), tail_tile(5))) * inv_w

    o_ref[0] = gates_ref[0, 0] * o_cmp + gates_ref[0, 1] * o_sel + gates_ref[0, 2] * o_win

    nbuf = bkw_ref.shape[1]
    for n, src, dst in ((4, bkw_ref, skw_ref), (5, bvw_ref, svw_ref)):
        dst[0, 0:nbuf - n_dec, :] = src[0, n_dec:nbuf, :]
        dst[0, nbuf - n_dec:nbuf, :] = tails_ref[0, n, 0:n_dec, :]


def _nsa_sample(page_table, caches, qbd, gates3, tails, bkw, bvw, tables, cw, n_dec):
    nbatch, n_pages = page_table.shape
    page = caches[0].shape[1]
    past = n_pages * page
    cbs, sbl, sbt, wbl, wbt, eloc, tri = tables
    nbc = cbs.shape[1]
    nrow = qbd.shape[1]
    nbuf = bkw.shape[1]
    n_tiles = past // SAMPLE_KEY_TILE
    assert past % SAMPLE_KEY_TILE == 0 and (past // BLK) % LANES == 0
    per_b = lambda a: pl.BlockSpec((1,) + a.shape[1:], lambda b, pt: (b,) + (0,) * (a.ndim - 1))
    const = lambda a: pl.BlockSpec(a.shape, lambda b, pt: (0,) * a.ndim, pipeline_mode=pl.Buffered(1))
    hbm = pl.BlockSpec(memory_space=pl.ANY)
    grid_spec = pltpu.PrefetchScalarGridSpec(
        num_scalar_prefetch=1,
        grid=(nbatch,),
        in_specs=[hbm] * 4 + [per_b(qbd), per_b(gates3), per_b(tails), per_b(bkw), per_b(bvw)]
                 + [const(a) for a in tables] + [const(a) for a in cw],
        out_specs=[pl.BlockSpec((1, nrow, D_KV), lambda b, pt: (b, 0, 0)),
                   pl.BlockSpec((1, nbuf, D_KV), lambda b, pt: (b, 0, 0)),
                   pl.BlockSpec((1, nbuf, D_KV), lambda b, pt: (b, 0, 0))],
        scratch_shapes=[pltpu.VMEM((2, past, D_KV), F32),
                        pltpu.SemaphoreType.DMA((2,)),
                        pltpu.VMEM((TAIL_BLOCKS * BLK, D_KV), F32),
                        pltpu.VMEM((LANES, D_KV), F32),
                        pltpu.VMEM((nbc, D_KV), BF),
                        pltpu.VMEM((nbc, D_KV), BF),
                        pltpu.VMEM((n_tiles, nrow, SAMPLE_KEY_TILE), F32),
                        pltpu.VMEM((n_tiles, nrow, SAMPLE_KEY_TILE), BF)])
    return pl.pallas_call(
        functools.partial(_nsa_sample_kernel, n_pages=n_pages, page=page, n_dec=n_dec),
        grid_spec=grid_spec,
        out_shape=[jax.ShapeDtypeStruct((nbatch, nrow, D_KV), F32),
                   jax.ShapeDtypeStruct((nbatch, nbuf, D_KV), F32),
                   jax.ShapeDtypeStruct((nbatch, nbuf, D_KV), F32)],
        compiler_params=pltpu.CompilerParams(dimension_semantics=("arbitrary",),
                                             vmem_limit_bytes=V7X_VMEM_LIMIT),
        name="nsa_sample",
    )(page_table, *caches, qbd, gates3, tails, bkw, bvw, *tables, *cw)


def _t5_bucket_np(n):
    d = np.maximum(np.arange(n), 0)
    max_exact = N_BUCKETS // 2
    nf = np.maximum(d, max_exact).astype(np.float32)
    large = max_exact + (np.log(nf / np.float32(max_exact)) / np.float32(math.log(MAX_DIST / max_exact))
                         * np.float32(N_BUCKETS - max_exact)).astype(np.int32)
    return np.where(d < max_exact, d, np.minimum(large, N_BUCKETS - 1))


N_NEAR = 256
_BUCKETS = _t5_bucket_np(N_NEAR)
assert np.all(_BUCKETS[QT + 1:] == N_BUCKETS - 1)


def _shifted_bias(rel_bias):
    return (rel_bias[_BUCKETS[:N_NEAR]] - rel_bias[N_BUCKETS - 1][None, :]) * LOG2E


def _bias_cols(tp, dist):
    val = tp[np.clip(dist, 0, N_NEAR - 1)]
    val = jnp.where((dist >= 0)[:, :, None], val, NEG)
    return jnp.transpose(val, (0, 2, 1)).reshape(dist.shape[0], -1)


def _prompt_tables(rel_bias, nb):
    tp = _shifted_bias(rel_bias)
    q = np.arange(QT)[None, :]
    r = np.arange(-1, 7)[:, None]
    d_cmp = np.where(r <= 2, BLK * r - (BLK - 1) + q, -1)
    cb = _bias_cols(tp, d_cmp)
    kk = np.arange(2 * KT)[:, None]
    sb = _bias_cols(tp, q + KT - kk)
    kk = np.arange(KT)[:, None]
    w0 = jnp.tile(jnp.asarray(np.where(kk >= q, 0.0, NEG), F32), (1, N_HEADS))
    idx = np.arange(nb)
    tri = jnp.asarray(idx[None, :] < idx[:, None], BF)
    return cb, sb, w0, tri


def _sample_rows_bias(tp, dist):
    val = tp[np.clip(dist, 0, N_NEAR - 1)]
    val = jnp.where((dist >= 0)[:, :, None], val, NEG)
    val = val.reshape(dist.shape[0], dist.shape[1], N_KV, HPG)
    return jnp.transpose(val, (2, 0, 3, 1)).reshape(-1, dist.shape[1])


def _sample_tables(rel_bias, past, n_dec, nbuf, nbc):
    tp = _shifted_bias(rel_bias)
    q = np.arange(n_dec)[:, None]
    nbp = past // BLK
    j = np.arange(nbc)[None, :]
    cbs = _sample_rows_bias(tp, np.where(j < nbp, past + q - (BLK * j + BLK - 1), -1))
    kk = np.arange(LANES)[None, :]
    sbl = _sample_rows_bias(tp, q + LANES - kk)
    sbt = _sample_rows_bias(tp, q - kk)
    r = np.arange(nbuf)[None, :]
    d_w = nbuf + q - r
    wbl = _sample_rows_bias(tp, np.where(d_w <= WINDOW, d_w, -1))
    wbt = sbt
    kk = np.arange(SAMPLE_KEY_TILE)[:, None]
    eloc = jnp.asarray(kk // BLK == np.arange(LANES)[None, :], BF)
    idx = np.arange(nbc)
    tri = jnp.asarray(idx[:, None] < idx[None, :], BF)
    return cbs, sbl, sbt, wbl, wbt, eloc, tri


def _compress_weights(pe, w1, w2):
    pe2 = jnp.tile(pe, (1, N_KV))
    w1r = w1.reshape(BLK, HEAD_DIM, CMP_HID)
    eye = jnp.eye(N_KV, dtype=w1.dtype)
    w1bd = jnp.einsum("pdh,gk->pgdkh", w1r, eye).reshape(BLK, D_KV, N_KV * CMP_HID).astype(BF)
    w2bd = jnp.einsum("hd,gk->ghkd", w2, eye).reshape(N_KV * CMP_HID, D_KV).astype(BF)
    return pe2, w1bd, w2bd


def _split_w_in(w_in, d_attn, d_conv, d_mix):
    sizes = (d_attn,) + (D_KV,) * 6 + (3 * N_HEADS,) + (d_conv,) * 3 + (d_mix,)
    offs = np.cumsum((0,) + sizes)
    wq = w_in[:, offs[0]:offs[1]]
    wkv = w_in[:, offs[1]:offs[7]]
    wg = jnp.pad(w_in[:, offs[7]:offs[8]], ((0, 0), (0, LANES - 3 * N_HEADS)))
    wc = w_in[:, offs[8]:offs[11]]
    wz = w_in[:, offs[11]:offs[12]]
    return tuple(w.astype(BF) for w in (wq, wkv, wg, wc, wz))


def kernel(x_prompt, x_sample, cache_k_cmp, cache_v_cmp, cache_k_slc, cache_v_slc, cache_k_win,
           cache_v_win, state_conv, page_table, c_prompt, c_sample, ada_w, ada_b, norm_g, w_in,
           cmp_pe_k, cmp_w1_k, cmp_w2_k, cmp_pe_v, cmp_w1_v, cmp_w2_v, conv_w, w_out, rel_bias, final_g):
    depth = ada_w.shape[0]
    assert depth == 1
    batch, seq, d = x_prompt.shape
    nbatch, n_dec, _ = x_sample.shape
    d_attn = N_HEADS * HEAD_DIM
    d_conv = conv_w.shape[2]
    d_mix = d_attn + d_conv
    n_pages = page_table.shape[1]
    page = cache_k_cmp.shape[2]
    past = n_pages * page
    nbuf = cache_k_win.shape[2]
    tm = 512
    assert seq % tm == 0 and seq >= WINDOW and nbuf == WINDOW

    wts = _split_w_in(w_in[0], d_attn, d_conv, d_mix)
    cwk = _compress_weights(cmp_pe_k[0], cmp_w1_k[0], cmp_w2_k[0])
    cwv = _compress_weights(cmp_pe_v[0], cmp_w1_v[0], cmp_w2_v[0])
    woa = w_out[0, :d_attn].astype(BF)
    woc = w_out[0, d_attn:].astype(BF)
    ng = norm_g[0].reshape(1, d)
    fg = final_g.reshape(1, d)
    cw = conv_w[0]

    n_c = batch + nbatch
    c_all = jnp.pad(jnp.concatenate([c_prompt, c_sample], axis=0), ((0, (-n_c) % 8), (0, 0)))
    mod = _ada(c_all, ada_w[0], ada_b[0])
    shift, scale, gate = mod[:, :d], mod[:, d:2 * d], mod[:, 2 * d:]

    x2d = x_prompt.reshape(batch * seq, d)
    sc_p = (1.0 + scale[:batch]).reshape(batch, 1, d)
    sh_p = shift[:batch].reshape(batch, 1, d)
    (qT, kc, vc, ks, vs, kw, vw, kslc, vslcT, kwin, vwinT, gates_tok, yc, sza, cv_p) = _proj_prompt(
        x2d, sc_p, sh_p, ng, wts, cw, batch, seq, tm)
    kcmp, vcmpT = _compress_prompt(kc.reshape(batch, seq, D_KV), vc.reshape(batch, seq, D_KV),
                                   cwk + cwv, batch, seq)
    nq = seq // QT
    gates = gates_tok[:, :3 * N_HEADS].reshape(batch, nq, QT, 3, N_HEADS)
    gates = jnp.transpose(gates, (0, 1, 3, 4, 2)).reshape(batch, nq, 3, NCOL)
    gates = jnp.pad(gates, ((0, 0), (0, 0), (0, 5), (0, 0)))
    o_p = _nsa_prompt(qT, gates, kslc, vslcT, kwin, vwinT, kcmp, vcmpT,
                      _prompt_tables(rel_bias, seq // BLK), batch, seq)
    y_p = _merge(x2d, o_p.reshape(batch * seq, d_attn), sza, yc, gate[:batch].reshape(batch, 1, d),
                 woa, woc, fg, tm, seq)
    y_prompt = y_p.reshape(batch, seq, d)
    heads = lambda a: a.reshape(1, batch, seq, N_KV, HEAD_DIM)
    p_states = (heads(kc), heads(vc), heads(ks), heads(vs),
                heads(kw)[:, :, seq - WINDOW:], heads(vw)[:, :, seq - WINDOW:], cv_p[None])

    tok = n_dec * nbatch
    xs2d = jnp.transpose(x_sample, (1, 0, 2)).reshape(tok, d)
    rep = lambda a: jnp.tile(a, (n_dec, 1)).reshape(1, tok, d)
    st2d = jnp.transpose(state_conv[0], (1, 0, 2)).reshape((CONV_W - 1) * nbatch, d_conv)
    q_s, kv_s, gates_s, yc_s, sza_s, cv_s = _proj_sample(
        xs2d, rep(1.0 + scale[batch:n_c]), rep(shift[batch:n_c]), ng, wts, cw, st2d, nbatch, n_dec)

    q5 = jnp.transpose(q_s.reshape(n_dec, nbatch, N_KV, HPG, HEAD_DIM), (1, 2, 0, 3, 4))
    eye = jnp.eye(N_KV, dtype=F32)
    qbd = (q5[:, :, :, :, None, :] * eye[None, :, None, None, :, None]).reshape(
        nbatch, N_KV * n_dec * HPG, D_KV).astype(BF)
    g5 = gates_s[:, :3 * N_HEADS].reshape(n_dec, nbatch, 3, N_KV, HPG)
    g5 = jnp.transpose(g5, (1, 2, 3, 0, 4)).reshape(nbatch, 3, N_KV * n_dec * HPG, 1)
    gates3 = jnp.broadcast_to(g5, (nbatch, 3, N_KV * n_dec * HPG, D_KV))
    kv6 = jnp.transpose(kv_s.reshape(n_dec, nbatch, 6, D_KV), (1, 2, 0, 3))
    tails = jnp.pad(kv6, ((0, 0), (0, 0), (0, 8 - n_dec), (0, 0)))
    nbc = -(-(past // BLK + TAIL_BLOCKS) // LANES) * LANES
    caches = tuple(c[0].reshape(c.shape[1], page, D_KV)
                   for c in (cache_k_cmp, cache_v_cmp, cache_k_slc, cache_v_slc))
    o_s, s_kw, s_vw = _nsa_sample(
        page_table, caches, qbd, gates3, tails,
        cache_k_win[0].reshape(nbatch, nbuf, D_KV), cache_v_win[0].reshape(nbatch, nbuf, D_KV),
        _sample_tables(rel_bias, past, n_dec, nbuf, nbc), cwk + cwv, n_dec)
    o6 = o_s.reshape(nbatch, N_KV, n_dec, HPG, N_KV, HEAD_DIM)
    o_diag = jnp.stack([o6[:, g, :, :, g, :] for g in range(N_KV)], axis=1)
    o_tok = jnp.transpose(o_diag, (2, 0, 1, 3, 4)).reshape(tok, d_attn).astype(BF)
    y_s = _merge(xs2d, o_tok, sza_s, yc_s, rep(gate[batch:n_c]), woa, woc, fg, tok, tok)
    y_sample = jnp.transpose(y_s.reshape(n_dec, nbatch, d), (1, 0, 2))
    new = lambda n: kv6[:, n].reshape(1, nbatch, n_dec, N_KV, HEAD_DIM)
    win = lambda a: a.reshape(1, nbatch, nbuf, N_KV, HEAD_DIM)
    cv_out = jnp.transpose(cv_s.reshape(CONV_W - 1, nbatch, d_conv), (1, 0, 2))[None]
    s_states = (new(0), new(1), new(2), new(3), win(s_kw), win(s_vw), cv_out)

    return (y_prompt, y_sample) + p_states + s_states
```

```python
import functools
import math

import numpy as np
import jax
import jax.numpy as jnp
from jax import lax
from jax.experimental import pallas as pl
from jax.experimental.pallas import tpu as pltpu

BF = jnp.bfloat16
F32 = jnp.float32
I32 = jnp.int32

HEAD_DIM = 64
N_KV = 2
HPG = 8
N_HEADS = N_KV * HPG
D_KV = N_KV * HEAD_DIM
BLK = 64
BLK_SHIFT = 6
TAIL_BLOCKS = 16
XPITCH = 136
XTAIL_PAGES = 8
N_SEL = 16
WINDOW = 512
CMP_HID = 2 * HEAD_DIM
CONV_W = 3
N_BUCKETS = 32
MAX_DIST = 128
EPS = 1e-6
NEG = -1e30
M_INIT = -5e29
VALID_THRESH = -1e29
LOG2E = math.log2(math.e)
QT = 128
KT = 128
FAR_GROUP = 4
ACC_ROWS = HEAD_DIM + 16
NCOL = N_HEADS * QT
GCOL = HPG * QT
LANES = 128
SAMPLE_KEY_TILE = 8192
V7X_VMEM_LIMIT = 56 * 1024 * 1024


def _dot(a, b):
    return jnp.dot(a, b, preferred_element_type=F32)


def _dot_nt(a, b):
    return lax.dot_general(a, b, (((1,), (1,)), ((), ())), preferred_element_type=F32)


def _sigmoid(x):
    return 1.0 / (1.0 + jnp.exp(-x))


def _silu(x):
    return x * _sigmoid(x)


def _ada_kernel(c_ref, w_ref, b_ref, o_ref):
    o_ref[...] = _dot(c_ref[...].astype(BF), w_ref[...].astype(BF)) + b_ref[...]


def _ada(c_all, ada_w, ada_b):
    rows, d = c_all.shape
    n = ada_w.shape[1]
    return pl.pallas_call(
        _ada_kernel,
        grid=(n // d,),
        in_specs=[pl.BlockSpec((rows, d), lambda j: (0, 0)),
                  pl.BlockSpec((d, d), lambda j: (0, j)),
                  pl.BlockSpec((1, d), lambda j: (0, j))],
        out_specs=pl.BlockSpec((rows, d), lambda j: (0, j)),
        out_shape=jax.ShapeDtypeStruct((rows, n), F32),
        name="ada_modulation",
    )(c_all, ada_w, ada_b.reshape(1, n))


def _proj_body(x_ref, sc_ref, sh_ref, ng_ref, wq_ref, wkv_ref, wg_ref, wc_ref, wz_ref, cw_ref):
    x = x_ref[...]
    ms = jnp.mean(x * x, axis=-1, keepdims=True)
    xn = x * lax.rsqrt(ms + EPS) * ng_ref[...]
    h = xn * sc_ref[0] + sh_ref[0]
    hb = h.astype(BF)
    q = _dot(hb, wq_ref[...]) * (HEAD_DIM ** -0.5 * LOG2E)
    kv = _dot(hb, wkv_ref[...])
    gates = _sigmoid(_dot(hb, wg_ref[...]))
    c3 = _dot(hb, wc_ref[...])
    dc = c3.shape[1] // 3
    hc, bc, cc = c3[:, :dc], c3[:, dc:2 * dc], c3[:, 2 * dc:]
    u = cc * hc
    z = _dot(hb, wz_ref[...])
    da = z.shape[1] - dc
    sza = _silu(z[:, :da])
    szc = _silu(z[:, da:])
    return q, kv, gates, u, bc, sza, szc


def _proj_prompt_kernel(x_ref, sc_ref, sh_ref, ng_ref, wq_ref, wkv_ref, wg_ref, wc_ref, wz_ref, cw_ref,
                        qT_ref, kcT_ref, vcT_ref, ksT_ref, vsT_ref, kwT_ref, vwT_ref, kc_ref, vc_ref,
                        kslc_ref, vslcT_ref, kwin_ref, vwinT_ref, gates_ref, yc_ref, sza_ref, cv_ref,
                        uext_ref, *, tiles_per_batch, tm):
    t = pl.program_id(0)
    q, kv, gates, u, bc, sza, szc = _proj_body(
        x_ref, sc_ref, sh_ref, ng_ref, wq_ref, wkv_ref, wg_ref, wc_ref, wz_ref, cw_ref)
    qT_ref[0] = q.T.astype(BF)
    kvT = [kv[:, n * D_KV:(n + 1) * D_KV].T for n in range(6)]
    for n, r in enumerate((kcT_ref, vcT_ref, ksT_ref, vsT_ref, kwT_ref, vwT_ref)):
        r[0] = kvT[n]
    kc_ref[...] = kv[:, 0:D_KV]
    vc_ref[...] = kv[:, D_KV:2 * D_KV]
    kslc_ref[0] = kv[:, 2 * D_KV:3 * D_KV].astype(BF)
    vslcT_ref[0] = kvT[3].astype(BF)
    kwin_ref[0] = kv[:, 4 * D_KV:5 * D_KV].astype(BF)
    vwinT_ref[0] = kvT[5].astype(BF)
    gates_ref[...] = gates

    first = (t % tiles_per_batch) == 0

    @pl.when(first)
    def _():
        uext_ref[0:8, :] = jnp.zeros((8, u.shape[1]), F32)

    @pl.when(jnp.logical_not(first))
    def _():
        uext_ref[0:8, :] = uext_ref[tm:tm + 8, :]

    uext_ref[8:8 + tm, :] = u
    y = (cw_ref[2:3, :] * u + cw_ref[1:2, :] * uext_ref[7:7 + tm, :]
         + cw_ref[0:1, :] * uext_ref[6:6 + tm, :])
    yc_ref[...] = (bc * y * szc).astype(BF)
    sza_ref[...] = sza.astype(BF)
    cv_ref[0] = u[tm - (CONV_W - 1):, :]


def _proj_sample_kernel(x_ref, sc_ref, sh_ref, ng_ref, wq_ref, wkv_ref, wg_ref, wc_ref, wz_ref, cw_ref,
                        st_ref, q_ref, kv_ref, gates_ref, yc_ref, sza_ref, cv_ref, uext_ref, *, nb, nt):
    q, kv, gates, u, bc, sza, szc = _proj_body(
        x_ref, sc_ref, sh_ref, ng_ref, wq_ref, wkv_ref, wg_ref, wc_ref, wz_ref, cw_ref)
    q_ref[...] = q
    kv_ref[...] = kv
    gates_ref[...] = gates
    ns = (CONV_W - 1) * nb
    uext_ref[0:ns, :] = st_ref[...]
    uext_ref[ns:ns + nt * nb, :] = u
    y = cw_ref[2:3, :] * u
    for k in range(CONV_W - 1):
        y = y + cw_ref[k:k + 1, :] * uext_ref[k * nb:k * nb + nt * nb, :]
    yc_ref[...] = (bc * y * szc).astype(BF)
    sza_ref[...] = sza.astype(BF)
    cv_ref[...] = uext_ref[nt * nb:nt * nb + ns, :]


def _const_spec(shape):
    nd = len(shape)
    return pl.BlockSpec(shape, lambda *_: (0,) * nd, pipeline_mode=pl.Buffered(1))


def _proj_prompt(x2d, sc, sh, ng, wts, cw, batch, seq, tm):
    tokens, d = x2d.shape
    tpb = seq // tm
    wq, wkv, wg, wc, wz = wts
    dc = wc.shape[1] // 3
    da = wz.shape[1] - dc
    tok_spec = lambda w: pl.BlockSpec((tm, w), lambda t: (t, 0))
    rowT_spec = lambda r: pl.BlockSpec((1, r, tm), lambda t: (t // tpb, 0, t % tpb))
    row_spec = lambda w: pl.BlockSpec((1, tm, w), lambda t: (t // tpb, t % tpb, 0))
    mod_spec = pl.BlockSpec((1, 1, d), lambda t: (t // tpb, 0, 0))
    out_shape = (
        [jax.ShapeDtypeStruct((batch, wq.shape[1], seq), BF)]
        + [jax.ShapeDtypeStruct((batch, D_KV, seq), F32)] * 6
        + [jax.ShapeDtypeStruct((tokens, D_KV), F32)] * 2
        + [jax.ShapeDtypeStruct((batch, seq, D_KV), BF), jax.ShapeDtypeStruct((batch, D_KV, seq), BF),
           jax.ShapeDtypeStruct((batch, seq, D_KV), BF), jax.ShapeDtypeStruct((batch, D_KV, seq), BF),
           jax.ShapeDtypeStruct((tokens, LANES), F32),
           jax.ShapeDtypeStruct((tokens, dc), BF), jax.ShapeDtypeStruct((tokens, da), BF),
           jax.ShapeDtypeStruct((batch, CONV_W - 1, dc), F32)])
    out_specs = (
        [rowT_spec(wq.shape[1])] + [rowT_spec(D_KV)] * 6 + [tok_spec(D_KV)] * 2
        + [row_spec(D_KV), rowT_spec(D_KV), row_spec(D_KV), rowT_spec(D_KV),
           tok_spec(LANES), tok_spec(dc), tok_spec(da),
           pl.BlockSpec((1, CONV_W - 1, dc), lambda t: (t // tpb, 0, 0))])
    return pl.pallas_call(
        functools.partial(_proj_prompt_kernel, tiles_per_batch=tpb, tm=tm),
        grid=(tokens // tm,),
        in_specs=[tok_spec(d), mod_spec, mod_spec, _const_spec((1, d)),
                  _const_spec(wq.shape), _const_spec(wkv.shape), _const_spec(wg.shape),
                  _const_spec(wc.shape), _const_spec(wz.shape), _const_spec(cw.shape)],
        out_specs=out_specs,
        out_shape=out_shape,
        scratch_shapes=[pltpu.VMEM((tm + 8, dc), F32)],
        compiler_params=pltpu.CompilerParams(dimension_semantics=("arbitrary",),
                                             vmem_limit_bytes=V7X_VMEM_LIMIT),
        name="proj_prompt",
    )(x2d, sc, sh, ng, wq, wkv, wg, wc, wz, cw)


def _proj_sample(x2d, sc, sh, ng, wts, cw, state2d, nb, nt):
    tokens, d = x2d.shape
    wq, wkv, wg, wc, wz = wts
    dc = wc.shape[1] // 3
    da = wz.shape[1] - dc
    ns = (CONV_W - 1) * nb
    full = lambda a: pl.BlockSpec(a.shape, lambda i: (0,) * a.ndim)
    ins = (x2d, sc, sh, ng, wq, wkv, wg, wc, wz, cw, state2d)
    out_shape = [jax.ShapeDtypeStruct((tokens, wq.shape[1]), F32),
                 jax.ShapeDtypeStruct((tokens, wkv.shape[1]), F32),
                 jax.ShapeDtypeStruct((tokens, LANES), F32),
                 jax.ShapeDtypeStruct((tokens, dc), BF), jax.ShapeDtypeStruct((tokens, da), BF),
                 jax.ShapeDtypeStruct((ns, dc), F32)]
    return pl.pallas_call(
        functools.partial(_proj_sample_kernel, nb=nb, nt=nt),
        grid=(1,),
        in_specs=[full(a) for a in ins],
        out_specs=[pl.BlockSpec(s.shape, lambda i: (0, 0)) for s in out_shape],
        out_shape=out_shape,
        scratch_shapes=[pltpu.VMEM((ns + tokens, dc), F32)],
        compiler_params=pltpu.CompilerParams(vmem_limit_bytes=V7X_VMEM_LIMIT),
        name="proj_sample",
    )(*ins)


def _compress(x_ref, nblk, pe_ref, w1_ref, w2_ref):
    def body(p, acc):
        xp = x_ref[pl.ds(p, nblk, stride=BLK), :] + pe_ref[pl.ds(p, 1), :]
        return acc + _dot(xp.astype(BF), w1_ref[p])
    acc = lax.fori_loop(0, BLK, body, jnp.zeros((nblk, N_KV * CMP_HID), F32))
    return _dot(_silu(acc).astype(BF), w2_ref[...])


def _compress_prompt_kernel(kc_ref, vc_ref, pek_ref, w1k_ref, w2k_ref, pev_ref, w1v_ref, w2v_ref,
                            kcmp_ref, vcmpT_ref, *, nblk):
    kcmp_ref[0] = _compress(kc_ref.at[0], nblk, pek_ref, w1k_ref, w2k_ref).astype(BF)
    vcmpT_ref[0] = _compress(vc_ref.at[0], nblk, pev_ref, w1v_ref, w2v_ref).T.astype(BF)


def _compress_prompt(kc, vc, cw, batch, seq):
    nblk = seq // BLK
    row = pl.BlockSpec((1, seq, D_KV), lambda b: (b, 0, 0))
    return pl.pallas_call(
        functools.partial(_compress_prompt_kernel, nblk=nblk),
        grid=(batch,),
        in_specs=[row, row] + [_const_spec(a.shape) for a in cw],
        out_specs=[pl.BlockSpec((1, nblk, D_KV), lambda b: (b, 0, 0)),
                   pl.BlockSpec((1, D_KV, nblk), lambda b: (b, 0, 0))],
        out_shape=[jax.ShapeDtypeStruct((batch, nblk, D_KV), BF),
                   jax.ShapeDtypeStruct((batch, D_KV, nblk), BF)],
        compiler_params=pltpu.CompilerParams(vmem_limit_bytes=V7X_VMEM_LIMIT),
        name="compress_prompt",
    )(kc, vc, *cw)


def _sortable_key(score):
    bits = pltpu.bitcast(score, I32)
    return jnp.where(bits < 0, bits ^ jnp.int32(0x7FFFFFFF), bits)


def _select_topk(score, tri, axis, k):
    key = _sortable_key(score)
    cnt = lambda m: jnp.sum(jnp.where(m, 1.0, 0.0), axis=axis, keepdims=True)
    int_min = jnp.int32(-2 ** 31)
    kf = float(k)
    thr = jnp.where(cnt(key >= 0) >= kf, jnp.int32(0), int_min)
    for b in range(30, -1, -1):
        cand = thr | jnp.int32(1 << b)
        thr = jnp.where(cnt(key >= cand) >= kf, cand, thr)
    gt = key > thr
    tie = key == thr
    tie_bf = jnp.where(tie, 1.0, 0.0).astype(BF)
    lower_ties = _dot(tri, tie_bf) if axis == 0 else _dot(tie_bf, tri)
    return jnp.logical_or(gt, jnp.logical_and(tie, lower_ties + cnt(gt) < kf))


def _attend(br, lhs, rhs_ref, krows, bias, vT, m_ref, acc_ref):
    s = _dot(lhs, rhs_ref[0:krows, :])
    if bias is not None:
        s = s + bias
    _consume(br, s, vT, m_ref, acc_ref)


def _consume(br, s, vT, m_ref, acc_ref):
    m_old = m_ref[br]
    m_new = jnp.maximum(m_old, jnp.max(s, axis=0, keepdims=True))
    alpha = jnp.exp2(m_old - m_new)
    pb = jnp.exp2(s - m_new).astype(BF)
    m_ref[br] = m_new
    ones = jnp.ones((ACC_ROWS - HEAD_DIM, s.shape[0]), BF)
    for g in range(N_KV):
        gs = slice(g * GCOL, (g + 1) * GCOL)
        v_aug = jnp.concatenate([vT[g * HEAD_DIM:(g + 1) * HEAD_DIM, :], ones], axis=0)
        acc_ref[br, g] = alpha[:, gs] * acc_ref[br, g] + _dot(v_aug, pb[:, gs])


def _nsa_prompt_kernel(qT_ref, gates_ref, kslc_ref, vslcT_ref, kwin_ref, vwinT_ref, kcmp_ref, vcmpT_ref,
                       cb_ref, sb_ref, w0_ref, tri_ref, o_ref,
                       rhs_ref, m_ref, acc_ref, s0_ref, s1_ref, *, nb):
    i = pl.program_id(1)

    for gh in range(N_HEADS):
        g = gh // HPG
        cols = slice(gh * QT, (gh + 1) * QT)
        rhs_ref[g * HEAD_DIM:(g + 1) * HEAD_DIM, cols] = qT_ref[0, gh * HEAD_DIM:(gh + 1) * HEAD_DIM, :]
        og = 1 - g
        rhs_ref[og * HEAD_DIM:(og + 1) * HEAD_DIM, cols] = jnp.zeros((HEAD_DIM, QT), BF)

    lg = _dot(kcmp_ref[0], rhs_ref[0:D_KV, :])
    rel = 2 * i - lax.broadcasted_iota(I32, (nb, NCOL), 0)
    bias = jnp.where(rel >= 3, 0.0, NEG)
    for rr in range(4):
        bias = jnp.where(rel == rr - 1, cb_ref[rr:rr + 1, :], bias)
    lgb = lg + bias
    mc = jnp.max(lgb, axis=0, keepdims=True)
    pc = jnp.exp2(lgb - mc)
    lc = jnp.sum(pc, axis=0, keepdims=True)
    pn = pc * jnp.where(mc > VALID_THRESH, 1.0 / lc, 0.0)
    pnb = pn.astype(BF)
    for g in range(N_KV):
        acc_ref[0, g, 0:HEAD_DIM, :] = _dot(vcmpT_ref[0, g * HEAD_DIM:(g + 1) * HEAD_DIM, :],
                                            pnb[:, g * GCOL:(g + 1) * GCOL])

    imp, valid = [], []
    for g in range(N_KV):
        a = pn[:, g * GCOL:g * GCOL + QT]
        for h in range(1, HPG):
            a = a + pn[:, g * GCOL + h * QT:g * GCOL + (h + 1) * QT]
        imp.append(a)
        valid.append(lgb[:, g * GCOL:g * GCOL + QT] > VALID_THRESH)
    imp = jnp.concatenate(imp, axis=1)
    valid = jnp.concatenate(valid, axis=1)
    jrow = lax.broadcasted_iota(I32, (nb, N_KV * QT), 0)
    tpos = i * QT + (lax.broadcasted_iota(I32, (nb, N_KV * QT), 1) & (QT - 1))
    score = jnp.where(valid, imp, jnp.where(jrow * BLK <= tpos, 2.0, -1.0))
    sel = _select_topk(score, tri_ref[...], 0, min(N_SEL, nb))
    selbias = jnp.where(jnp.logical_and(sel, score >= 0.0), 0.0, NEG).astype(BF)
    for gh in range(N_HEADS):
        g = gh // HPG
        rhs_ref[D_KV:D_KV + nb, gh * QT:(gh + 1) * QT] = selbias[:, g * QT:(g + 1) * QT]

    for br in (1, 2):
        m_ref[br] = jnp.full((1, NCOL), M_INIT, F32)
        acc_ref[br] = jnp.zeros((N_KV, ACC_ROWS, GCOL), F32)

    def sel_keys(kt, tk):
        k0 = pl.multiple_of(kt * KT, KT)
        k = kslc_ref[0, pl.ds(k0, tk), :]
        jj = lax.broadcasted_iota(I32, (tk, nb), 1)
        kk = lax.broadcasted_iota(I32, (tk, nb), 0)
        onehot = jnp.where(jj == (KT // BLK) * kt + jnp.right_shift(kk, BLK_SHIFT), 1.0, 0.0).astype(BF)
        return jnp.concatenate([k, onehot], axis=1), vslcT_ref[0, :, pl.ds(k0, tk)]

    def sel_tile(kt, tk, bias_tile):
        ka, vT = sel_keys(kt, tk)
        _attend(1, ka, rhs_ref, D_KV + nb, bias_tile, vT, m_ref, acc_ref)

    n_far = jnp.maximum(i - 1, 0)
    n_big = n_far // FAR_GROUP
    n_pairs = n_big // 2
    big = FAR_GROUP * KT

    def far_scores(j):
        ka, _ = sel_keys(j * FAR_GROUP, big)
        return _dot(ka, rhs_ref[...])

    def far_step(cur_ref, nxt_ref, j_cur, j_nxt):
        m_cur = m_ref[1]
        pb = jnp.exp2(cur_ref[...] - m_cur).astype(BF)
        s_nxt = far_scores(j_nxt)
        nxt_ref[...] = s_nxt
        m_nxt = jnp.maximum(m_cur, jnp.max(s_nxt, axis=0, keepdims=True))
        alpha = jnp.exp2(m_cur - m_nxt)
        vT = vslcT_ref[0, :, pl.ds(pl.multiple_of(j_cur * big, big), big)]
        ones = jnp.ones((ACC_ROWS - HEAD_DIM, big), BF)
        for g in range(N_KV):
            gs = slice(g * GCOL, (g + 1) * GCOL)
            v_aug = jnp.concatenate([vT[g * HEAD_DIM:(g + 1) * HEAD_DIM, :], ones], axis=0)
            acc_ref[1, g] = (acc_ref[1, g] + _dot(v_aug, pb[:, gs])) * alpha[:, gs]
        m_ref[1] = m_nxt

    @pl.when(n_pairs > 0)
    def _():
        s = far_scores(0)
        s0_ref[...] = s
        m_ref[1] = jnp.maximum(m_ref[1], jnp.max(s, axis=0, keepdims=True))

    def far_pair(jp, c):
        @pl.when(jp < n_pairs)
        def _():
            far_step(s0_ref, s1_ref, 2 * jp, 2 * jp + 1)

        @pl.when(2 * jp + 1 < n_big)
        def _():
            far_step(s1_ref, s0_ref, 2 * jp + 1, jnp.minimum(2 * jp + 2, 2 * n_pairs - 1))
        return c
    lax.fori_loop(0, n_pairs, far_pair, 0)

    @pl.when(n_big > 2 * n_pairs)
    def _():
        sel_tile((n_big - 1) * FAR_GROUP, big, None)

    def far_small(kt, c):
        sel_tile(kt, KT, None)
        return c
    lax.fori_loop(n_big * FAR_GROUP, n_far, far_small, 0)

    @pl.when(i >= 1)
    def _():
        sel_tile(i - 1, 2 * KT, sb_ref[...])

    @pl.when(i == 0)
    def _():
        sel_tile(0, KT, sb_ref[KT:2 * KT, :])

    n_win = WINDOW // KT + 1

    @pl.when(i >= n_win - 1)
    def _():
        k0 = pl.multiple_of((i - (n_win - 1)) * KT, KT)
        bias = jnp.concatenate(
            [w0_ref[...], jnp.zeros(((n_win - 3) * KT, NCOL), F32), sb_ref[...]], axis=0)
        _attend(2, kwin_ref[0, pl.ds(k0, n_win * KT), :], rhs_ref, D_KV, bias,
                vwinT_ref[0, :, pl.ds(k0, n_win * KT)], m_ref, acc_ref)

    @pl.when(i < n_win - 1)
    def _():
        for r in range(1, n_win):
            kt = i - (n_win - 1) + r

            @pl.when(kt >= 0)
            def _():
                k0 = pl.multiple_of(kt * KT, KT)
                r0 = (r - (n_win - 2)) * KT
                bias = sb_ref[r0:r0 + KT, :] if r >= n_win - 2 else None
                _attend(2, kwin_ref[0, pl.ds(k0, KT), :], rhs_ref, D_KV, bias,
                        vwinT_ref[0, :, pl.ds(k0, KT)], m_ref, acc_ref)

    gts = gates_ref[0, 0]
    for g in range(N_KV):
        cs = slice(g * GCOL, (g + 1) * GCOL)
        o = gts[0:1, cs] * acc_ref[0, g, 0:HEAD_DIM, :]
        for br in (1, 2):
            l = acc_ref[br, g, HEAD_DIM:HEAD_DIM + 1, :]
            o = o + (gts[br:br + 1, cs] * jnp.where(l > 0.0, 1.0 / l, 0.0)) * acc_ref[br, g, 0:HEAD_DIM, :]
        for hp in range(HPG // 2):
            pair = jnp.concatenate([o[:, (2 * hp) * QT:(2 * hp + 1) * QT],
                                    o[:, (2 * hp + 1) * QT:(2 * hp + 2) * QT]], axis=0)
            c0 = g * HPG * HEAD_DIM + hp * 2 * HEAD_DIM
            o_ref[0, :, c0:c0 + 2 * HEAD_DIM] = pair.T.astype(BF)


def _nsa_prompt(qT, gates, kslc, vslcT, kwin, vwinT, kcmp, vcmpT, tables, batch, seq):
    nb = seq // BLK
    nq = seq // QT
    cb, sb, w0, tri = tables
    dq = qT.shape[1]
    per_b = lambda shape: pl.BlockSpec((1,) + shape, lambda b, i: (b, 0, 0))
    return pl.pallas_call(
        functools.partial(_nsa_prompt_kernel, nb=nb),
        grid=(batch, nq),
        in_specs=[pl.BlockSpec((1, dq, QT), lambda b, i: (b, 0, i)),
                  pl.BlockSpec((1, 1, 8, NCOL), lambda b, i: (b, i, 0, 0)),
                  per_b((seq, D_KV)), per_b((D_KV, seq)), per_b((seq, D_KV)), per_b((D_KV, seq)),
                  per_b((nb, D_KV)), per_b((D_KV, nb)),
                  _const_spec(cb.shape), _const_spec(sb.shape), _const_spec(w0.shape),
                  _const_spec(tri.shape)],
        out_specs=pl.BlockSpec((1, QT, dq), lambda b, i: (b, i, 0)),
        out_shape=jax.ShapeDtypeStruct((batch, seq, dq), BF),
        scratch_shapes=[pltpu.VMEM((D_KV + nb, NCOL), BF),
                        pltpu.VMEM((3, 1, NCOL), F32),
                        pltpu.VMEM((3, N_KV, ACC_ROWS, GCOL), F32),
                        pltpu.VMEM((FAR_GROUP * KT, NCOL), F32),
                        pltpu.VMEM((FAR_GROUP * KT, NCOL), F32)],
        compiler_params=pltpu.CompilerParams(dimension_semantics=("arbitrary", "arbitrary"),
                                             vmem_limit_bytes=V7X_VMEM_LIMIT),
        name="nsa_prompt",
    )(qT, gates, kslc, vslcT, kwin, vwinT, kcmp, vcmpT, cb, sb, w0, tri)


def _merge_kernel(x_ref, o_ref, sza_ref, yc_ref, gate_ref, woa_ref, woc_ref, fg_ref, y_ref):
    og = (o_ref[...].astype(F32) * sza_ref[...].astype(F32)).astype(BF)
    acc = _dot(og, woa_ref[...]) + _dot(yc_ref[...], woc_ref[...])
    xo = x_ref[...] + gate_ref[0] * acc
    ms = jnp.mean(xo * xo, axis=-1, keepdims=True)
    y_ref[...] = xo * lax.rsqrt(ms + EPS) * fg_ref[...]


def _merge(x2d, o2d, sza, yc, gate, woa, woc, fg, tm, rows_per_gate):
    tokens, d = x2d.shape
    tok = lambda w: pl.BlockSpec((tm, w), lambda t: (t, 0))
    gr = gate.shape[1]
    return pl.pallas_call(
        _merge_kernel,
        grid=(tokens // tm,),
        in_specs=[tok(d), tok(o2d.shape[1]), tok(sza.shape[1]), tok(yc.shape[1]),
                  pl.BlockSpec((1, gr, d), lambda t: ((t * tm) // rows_per_gate, 0, 0)),
                  _const_spec(woa.shape), _const_spec(woc.shape), _const_spec(fg.shape)],
        out_specs=tok(d),
        out_shape=jax.ShapeDtypeStruct((tokens, d), F32),
        compiler_params=pltpu.CompilerParams(dimension_semantics=("arbitrary",),
                                             vmem_limit_bytes=V7X_VMEM_LIMIT),
        name="merge_out",
    )(x2d, o2d, sza, yc, gate, woa, woc, fg)


def _softmax_rows(parts):
    m = parts[0].max(axis=1, keepdims=True)
    for s in parts[1:]:
        m = jnp.maximum(m, s.max(axis=1, keepdims=True))
    ps = [jnp.exp2(s - m) for s in parts]
    l = ps[0].sum(axis=1, keepdims=True)
    for p in ps[1:]:
        l = l + p.sum(axis=1, keepdims=True)
    inv = jnp.where(jnp.logical_and(m > VALID_THRESH, l > 0.0), 1.0 / l, 0.0)
    return ps, inv


def _nsa_sample_kernel(pt_ref, ckc_ref, cvc_ref, cks_ref, cvs_ref,
                       qbd_ref, gates_ref, tails_ref, tailsT_ref, bkwT_ref, bvwT_ref,
                       cbs_ref, sbl_ref, sbt_ref, wbl_ref, wbt_ref, elocT_ref, tri_ref,
                       pek_ref, w1k_ref, w2k_ref, pev_ref, w1v_ref, w2v_ref,
                       o_ref, skwT_ref, svwT_ref,
                       buf_ref, sem, xbuf_ref, cmpf_ref, kcmp_ref, vcmp_ref, sbuf_ref, pbuf_ref,
                       *, n_pages, page, n_dec):
    b = pl.program_id(0)
    nbatch = pl.num_programs(0)
    past = n_pages * page
    nbp = past // BLK
    nbc = kcmp_ref.shape[0]
    n_tiles = past // SAMPLE_KEY_TILE
    nrow = qbd_ref.shape[1]
    caches = (ckc_ref, cvc_ref, cks_ref, cvs_ref)

    def page_copy(cache_ref, pg, j, slot):
        return pltpu.make_async_copy(cache_ref.at[pg],
                                     buf_ref.at[slot, :, pl.ds(pl.multiple_of(j * page, page), page)],
                                     sem.at[slot])

    def start_fetch(cache_ref, bb, slot):
        def body(j, c):
            page_copy(cache_ref, pt_ref[bb, j], j, slot).start()
            return c
        lax.fori_loop(0, n_pages, body, 0)

    def wait_fetch(cache_ref, slot):
        def body(j, c):
            page_copy(cache_ref, 0, j, slot).wait()
            return c
        lax.fori_loop(0, n_pages, body, 0)

    def stage(c):
        if c + 1 < len(caches):
            start_fetch(caches[c + 1], b, (c + 1) % 2)
        else:
            @pl.when(b + 1 < nbatch)
            def _():
                start_fetch(caches[0], b + 1, 0)
        wait_fetch(caches[c], c % 2)
        return buf_ref.at[c % 2]

    @pl.when(b == 0)
    def _():
        start_fetch(caches[0], 0, 0)

    qbd = qbd_ref[0]

    def compress_cache(c, slot_ref, pe_ref, w1_ref, w2_ref, dst_ref):
        def xpose(j, carry):
            src = slot_ref[:, pl.ds(pl.multiple_of(j * page, page), page)]
            xbuf_ref[pl.ds(pl.multiple_of(j * XPITCH, 8), page), :] = src.T
            return carry
        lax.fori_loop(0, n_pages, xpose, 0, unroll=8)
        n_extra = XTAIL_PAGES * XPITCH
        xbuf_ref[pl.ds(n_pages * XPITCH, n_extra), :] = jnp.zeros((n_extra, D_KV), F32)
        xbuf_ref[pl.ds(n_pages * XPITCH, 8), :] = tails_ref[0, c]
        n_even = n_pages + XTAIL_PAGES

        def body(p, acc):
            acc_e, acc_o = acc
            pe = pe_ref[pl.ds(p, 1), :]
            xe = xbuf_ref[pl.ds(p, n_even, stride=XPITCH), :] + pe
            xo = xbuf_ref[pl.ds(p + BLK, n_pages, stride=XPITCH), :] + pe
            w = w1_ref[p]
            return acc_e + _dot(xe.astype(BF), w), acc_o + _dot(xo.astype(BF), w)
        acc_e, acc_o = lax.fori_loop(
            0, BLK, body, (jnp.zeros((n_even, N_KV * CMP_HID), F32), jnp.zeros((n_pages, N_KV * CMP_HID), F32)),
            unroll=2)
        out_e = _dot(_silu(acc_e).astype(BF), w2_ref[...])
        out_o = _dot(_silu(acc_o).astype(BF), w2_ref[...])
        cmpf_ref[pl.ds(0, n_pages, stride=2), :] = out_e[0:n_pages]
        cmpf_ref[pl.ds(1, n_pages, stride=2), :] = out_o
        cmpf_ref[nbp:nbp + XTAIL_PAGES, :] = out_e[n_pages:n_even]
        cmpf_ref[nbp + XTAIL_PAGES:nbp + TAIL_BLOCKS, :] = jnp.zeros((TAIL_BLOCKS - XTAIL_PAGES, D_KV), F32)
        dst_ref[0:nbp + TAIL_BLOCKS, :] = cmpf_ref[...].astype(BF)
        dst_ref[nbp + TAIL_BLOCKS:nbc, :] = jnp.zeros((nbc - nbp - TAIL_BLOCKS, D_KV), BF)

    compress_cache(0, stage(0), pek_ref, w1k_ref, w2k_ref, kcmp_ref)
    compress_cache(1, stage(1), pev_ref, w1v_ref, w2v_ref, vcmp_ref)

    lgc = _dot_nt(qbd, kcmp_ref[...]) + cbs_ref[...]
    (pc,), invc = _softmax_rows([lgc])
    pn = pc * invc
    o_cmp = _dot(pn.astype(BF), vcmp_ref[...])
    imp = jnp.sum(pn.reshape(nrow // HPG, HPG, nbc), axis=1)
    jcol = lax.broadcasted_iota(I32, imp.shape, 1)
    score = jnp.where(jcol < nbp, imp, jnp.where(jcol == nbp, 2.0, -1.0))
    sel = _select_topk(score, tri_ref[...], 1, N_SEL)
    selbias = jnp.where(jnp.logical_and(sel, score >= 0.0), 0.0, NEG)
    selb = jnp.broadcast_to(selbias[:, None, :], (nrow // HPG, HPG, nbc)).reshape(nrow, nbc).astype(BF)

    kslot = stage(2)
    tile = lambda t: slice(t * SAMPLE_KEY_TILE, (t + 1) * SAMPLE_KEY_TILE)
    for t in range(n_tiles):
        sbuf_ref[t] = (_dot(qbd, kslot[:, tile(t)].astype(BF))
                       + _dot(selb[:, t * LANES:(t + 1) * LANES], elocT_ref[...]))
    last = slice(SAMPLE_KEY_TILE - LANES, SAMPLE_KEY_TILE)
    sbuf_ref[n_tiles - 1, :, last] = sbuf_ref[n_tiles - 1, :, last] + sbl_ref[...]
    s_tail = (_dot(qbd, tailsT_ref[0, 0].astype(BF))
              + _dot(selb[:, nbp:nbp + LANES], elocT_ref[:, 0:LANES]) + sbt_ref[...])
    ps, inv_sel = _softmax_rows([sbuf_ref[t] for t in range(n_tiles)] + [s_tail])
    for t in range(n_tiles):
        pbuf_ref[t] = ps[t].astype(BF)
    p_tail = ps[n_tiles].astype(BF)

    vslot = stage(3)
    o_sel = _dot_nt(p_tail, tailsT_ref[0, 1].astype(BF))
    for t in range(n_tiles):
        o_sel = o_sel + _dot_nt(pbuf_ref[t], vslot[:, tile(t)].astype(BF))
    o_sel = o_sel * inv_sel

    s_w = _dot(qbd, bkwT_ref[0].astype(BF)) + wbl_ref[...]
    s_wt = _dot(qbd, tailsT_ref[0, 2].astype(BF)) + wbt_ref[...]
    (p_w, p_wt), inv_w = _softmax_rows([s_w, s_wt])
    o_win = (_dot_nt(p_w.astype(BF), bvwT_ref[0].astype(BF))
             + _dot_nt(p_wt.astype(BF), tailsT_ref[0, 3].astype(BF))) * inv_w

    o_ref[0] = gates_ref[0, 0] * o_cmp + gates_ref[0, 1] * o_sel + gates_ref[0, 2] * o_win

    nbuf = bkwT_ref.shape[2]
    lane = lax.broadcasted_iota(I32, (D_KV, LANES), 1)
    for n, src, dst in ((2, bkwT_ref, skwT_ref), (3, bvwT_ref, svwT_ref)):
        rolled = pltpu.roll(src[0], nbuf - n_dec, axis=1)
        new = pltpu.roll(tailsT_ref[0, n], LANES - n_dec, axis=1)
        dst[0, :, 0:nbuf - LANES] = rolled[:, 0:nbuf - LANES]
        dst[0, :, nbuf - LANES:nbuf] = jnp.where(lane >= LANES - n_dec, new, rolled[:, nbuf - LANES:nbuf])


def _nsa_sample(page_table, caches, qbd, gates3, tails, tailsT, bkwT, bvwT, tables, cw, n_dec):
    nbatch, n_pages = page_table.shape
    page = caches[0].shape[2]
    past = n_pages * page
    cbs, sbl, sbt, wbl, wbt, elocT, tri = tables
    nbc = cbs.shape[1]
    nrow = qbd.shape[1]
    nbuf = bkwT.shape[2]
    nbp = past // BLK
    n_tiles = past // SAMPLE_KEY_TILE
    assert past % SAMPLE_KEY_TILE == 0 and nbp % LANES == 0 and page == 2 * BLK and nbuf % LANES == 0
    per_b = lambda a: pl.BlockSpec((1,) + a.shape[1:], lambda b, pt: (b,) + (0,) * (a.ndim - 1))
    const = lambda a: pl.BlockSpec(a.shape, lambda b, pt: (0,) * a.ndim, pipeline_mode=pl.Buffered(1))
    hbm = pl.BlockSpec(memory_space=pl.ANY)
    grid_spec = pltpu.PrefetchScalarGridSpec(
        num_scalar_prefetch=1,
        grid=(nbatch,),
        in_specs=[hbm] * 4 + [per_b(a) for a in (qbd, gates3, tails, tailsT, bkwT, bvwT)]
                 + [const(a) for a in tables] + [const(a) for a in cw],
        out_specs=[pl.BlockSpec((1, nrow, D_KV), lambda b, pt: (b, 0, 0)),
                   pl.BlockSpec((1, D_KV, nbuf), lambda b, pt: (b, 0, 0)),
                   pl.BlockSpec((1, D_KV, nbuf), lambda b, pt: (b, 0, 0))],
        scratch_shapes=[pltpu.VMEM((2, D_KV, past), F32),
                        pltpu.SemaphoreType.DMA((2,)),
                        pltpu.VMEM(((n_pages + XTAIL_PAGES) * XPITCH, D_KV), F32),
                        pltpu.VMEM((nbp + TAIL_BLOCKS, D_KV), F32),
                        pltpu.VMEM((nbc, D_KV), BF),
                        pltpu.VMEM((nbc, D_KV), BF),
                        pltpu.VMEM((n_tiles, nrow, SAMPLE_KEY_TILE), F32),
                        pltpu.VMEM((n_tiles, nrow, SAMPLE_KEY_TILE), BF)])
    return pl.pallas_call(
        functools.partial(_nsa_sample_kernel, n_pages=n_pages, page=page, n_dec=n_dec),
        grid_spec=grid_spec,
        out_shape=[jax.ShapeDtypeStruct((nbatch, nrow, D_KV), F32),
                   jax.ShapeDtypeStruct((nbatch, D_KV, nbuf), F32),
                   jax.ShapeDtypeStruct((nbatch, D_KV, nbuf), F32)],
        compiler_params=pltpu.CompilerParams(dimension_semantics=("arbitrary",),
                                             vmem_limit_bytes=V7X_VMEM_LIMIT),
        name="nsa_sample",
    )(page_table, *caches, qbd, gates3, tails, tailsT, bkwT, bvwT, *tables, *cw)


def _t5_bucket_np(n):
    d = np.maximum(np.arange(n), 0)
    max_exact = N_BUCKETS // 2
    nf = np.maximum(d, max_exact).astype(np.float32)
    large = max_exact + (np.log(nf / np.float32(max_exact)) / np.float32(math.log(MAX_DIST / max_exact))
                         * np.float32(N_BUCKETS - max_exact)).astype(np.int32)
    return np.where(d < max_exact, d, np.minimum(large, N_BUCKETS - 1))


N_NEAR = 256
_BUCKETS = _t5_bucket_np(N_NEAR)
assert np.all(_BUCKETS[QT + 1:] == N_BUCKETS - 1)


def _shifted_bias(rel_bias):
    return (rel_bias[_BUCKETS[:N_NEAR]] - rel_bias[N_BUCKETS - 1][None, :]) * LOG2E


def _bias_cols(tp, dist):
    val = tp[np.clip(dist, 0, N_NEAR - 1)]
    val = jnp.where((dist >= 0)[:, :, None], val, NEG)
    return jnp.transpose(val, (0, 2, 1)).reshape(dist.shape[0], -1)


def _prompt_tables(rel_bias, nb):
    tp = _shifted_bias(rel_bias)
    q = np.arange(QT)[None, :]
    r = np.arange(-1, 7)[:, None]
    d_cmp = np.where(r <= 2, BLK * r - (BLK - 1) + q, -1)
    cb = _bias_cols(tp, d_cmp)
    kk = np.arange(2 * KT)[:, None]
    sb = _bias_cols(tp, q + KT - kk)
    kk = np.arange(KT)[:, None]
    w0 = jnp.tile(jnp.asarray(np.where(kk >= q, 0.0, NEG), F32), (1, N_HEADS))
    idx = np.arange(nb)
    tri = jnp.asarray(idx[None, :] < idx[:, None], BF)
    return cb, sb, w0, tri


def _sample_rows_bias(tp, dist):
    val = tp[np.clip(dist, 0, N_NEAR - 1)]
    val = jnp.where((dist >= 0)[:, :, None], val, NEG)
    val = val.reshape(dist.shape[0], dist.shape[1], N_KV, HPG)
    return jnp.transpose(val, (2, 0, 3, 1)).reshape(-1, dist.shape[1])


def _sample_tables(rel_bias, past, n_dec, nbuf, nbc):
    tp = _shifted_bias(rel_bias)
    q = np.arange(n_dec)[:, None]
    nbp = past // BLK
    j = np.arange(nbc)[None, :]
    cbs = _sample_rows_bias(tp, np.where(j < nbp, past + q - (BLK * j + BLK - 1), -1))
    kk = np.arange(LANES)[None, :]
    sbl = _sample_rows_bias(tp, q + LANES - kk)
    sbt = _sample_rows_bias(tp, q - kk)
    r = np.arange(nbuf)[None, :]
    d_w = nbuf + q - r
    wbl = _sample_rows_bias(tp, np.where(d_w <= WINDOW, d_w, -1))
    wbt = sbt
    kk = np.arange(SAMPLE_KEY_TILE)[None, :]
    elocT = jnp.asarray(kk // BLK == np.arange(LANES)[:, None], BF)
    idx = np.arange(nbc)
    tri = jnp.asarray(idx[:, None] < idx[None, :], BF)
    return cbs, sbl, sbt, wbl, wbt, elocT, tri


def _compress_weights(pe, w1, w2):
    pe2 = jnp.tile(pe, (1, N_KV))
    w1r = w1.reshape(BLK, HEAD_DIM, CMP_HID)
    eye = jnp.eye(N_KV, dtype=w1.dtype)
    w1bd = jnp.einsum("pdh,gk->pgdkh", w1r, eye).reshape(BLK, D_KV, N_KV * CMP_HID).astype(BF)
    w2bd = jnp.einsum("hd,gk->ghkd", w2, eye).reshape(N_KV * CMP_HID, D_KV).astype(BF)
    return pe2, w1bd, w2bd


def _split_w_in(w_in, d_attn, d_conv, d_mix):
    sizes = (d_attn,) + (D_KV,) * 6 + (3 * N_HEADS,) + (d_conv,) * 3 + (d_mix,)
    offs = np.cumsum((0,) + sizes)
    wq = w_in[:, offs[0]:offs[1]]
    wkv = w_in[:, offs[1]:offs[7]]
    wg = jnp.pad(w_in[:, offs[7]:offs[8]], ((0, 0), (0, LANES - 3 * N_HEADS)))
    wc = w_in[:, offs[8]:offs[11]]
    wz = w_in[:, offs[11]:offs[12]]
    return tuple(w.astype(BF) for w in (wq, wkv, wg, wc, wz))


def kernel(x_prompt, x_sample, cache_k_cmp, cache_v_cmp, cache_k_slc, cache_v_slc, cache_k_win,
           cache_v_win, state_conv, page_table, c_prompt, c_sample, ada_w, ada_b, norm_g, w_in,
           cmp_pe_k, cmp_w1_k, cmp_w2_k, cmp_pe_v, cmp_w1_v, cmp_w2_v, conv_w, w_out, rel_bias, final_g):
    depth = ada_w.shape[0]
    assert depth == 1
    batch, seq, d = x_prompt.shape
    nbatch, n_dec, _ = x_sample.shape
    d_attn = N_HEADS * HEAD_DIM
    d_conv = conv_w.shape[2]
    d_mix = d_attn + d_conv
    n_pages = page_table.shape[1]
    page = cache_k_cmp.shape[2]
    past = n_pages * page
    nbuf = cache_k_win.shape[2]
    tm = 512
    assert seq % tm == 0 and seq >= WINDOW and nbuf == WINDOW

    wts = _split_w_in(w_in[0], d_attn, d_conv, d_mix)
    cwk = _compress_weights(cmp_pe_k[0], cmp_w1_k[0], cmp_w2_k[0])
    cwv = _compress_weights(cmp_pe_v[0], cmp_w1_v[0], cmp_w2_v[0])
    woa = w_out[0, :d_attn].astype(BF)
    woc = w_out[0, d_attn:].astype(BF)
    ng = norm_g[0].reshape(1, d)
    fg = final_g.reshape(1, d)
    cw = conv_w[0]

    n_c = batch + nbatch
    c_all = jnp.pad(jnp.concatenate([c_prompt, c_sample], axis=0), ((0, (-n_c) % 8), (0, 0)))
    mod = _ada(c_all, ada_w[0], ada_b[0])
    shift, scale, gate = mod[:, :d], mod[:, d:2 * d], mod[:, 2 * d:]

    x2d = x_prompt.reshape(batch * seq, d)
    sc_p = (1.0 + scale[:batch]).reshape(batch, 1, d)
    sh_p = shift[:batch].reshape(batch, 1, d)
    (qT, kcT, vcT, ksT, vsT, kwT, vwT, kc, vc, kslc, vslcT, kwin, vwinT, gates_tok, yc, sza, cv_p) = _proj_prompt(
        x2d, sc_p, sh_p, ng, wts, cw, batch, seq, tm)
    kcmp, vcmpT = _compress_prompt(kc.reshape(batch, seq, D_KV), vc.reshape(batch, seq, D_KV),
                                   cwk + cwv, batch, seq)
    nq = seq // QT
    gates = gates_tok[:, :3 * N_HEADS].reshape(batch, nq, QT, 3, N_HEADS)
    gates = jnp.transpose(gates, (0, 1, 3, 4, 2)).reshape(batch, nq, 3, NCOL)
    gates = jnp.pad(gates, ((0, 0), (0, 0), (0, 5), (0, 0)))
    o_p = _nsa_prompt(qT, gates, kslc, vslcT, kwin, vwinT, kcmp, vcmpT,
                      _prompt_tables(rel_bias, seq // BLK), batch, seq)
    y_p = _merge(x2d, o_p.reshape(batch * seq, d_attn), sza, yc, gate[:batch].reshape(batch, 1, d),
                 woa, woc, fg, tm, seq)
    y_prompt = y_p.reshape(batch, seq, d)
    heads = lambda aT: jnp.transpose(aT.reshape(aT.shape[0], N_KV, HEAD_DIM, aT.shape[2]), (0, 3, 1, 2))[None]
    p_states = (heads(kcT), heads(vcT), heads(ksT), heads(vsT),
                heads(kwT[:, :, seq - WINDOW:]), heads(vwT[:, :, seq - WINDOW:]), cv_p[None])

    tok = n_dec * nbatch
    xs2d = jnp.transpose(x_sample, (1, 0, 2)).reshape(tok, d)
    rep = lambda a: jnp.tile(a, (n_dec, 1)).reshape(1, tok, d)
    st2d = jnp.transpose(state_conv[0], (1, 0, 2)).reshape((CONV_W - 1) * nbatch, d_conv)
    q_s, kv_s, gates_s, yc_s, sza_s, cv_s = _proj_sample(
        xs2d, rep(1.0 + scale[batch:n_c]), rep(shift[batch:n_c]), ng, wts, cw, st2d, nbatch, n_dec)

    q5 = jnp.transpose(q_s.reshape(n_dec, nbatch, N_KV, HPG, HEAD_DIM), (1, 2, 0, 3, 4))
    eye = jnp.eye(N_KV, dtype=F32)
    qbd = (q5[:, :, :, :, None, :] * eye[None, :, None, None, :, None]).reshape(
        nbatch, N_KV * n_dec * HPG, D_KV).astype(BF)
    g5 = gates_s[:, :3 * N_HEADS].reshape(n_dec, nbatch, 3, N_KV, HPG)
    g5 = jnp.transpose(g5, (1, 2, 3, 0, 4)).reshape(nbatch, 3, N_KV * n_dec * HPG, 1)
    gates3 = jnp.broadcast_to(g5, (nbatch, 3, N_KV * n_dec * HPG, D_KV))
    kv6 = jnp.transpose(kv_s.reshape(n_dec, nbatch, 6, D_KV), (1, 2, 0, 3))
    tails = jnp.pad(kv6[:, 0:2], ((0, 0), (0, 0), (0, 8 - n_dec), (0, 0)))
    tailsT = jnp.pad(jnp.transpose(kv6[:, 2:6], (0, 1, 3, 2)), ((0, 0), (0, 0), (0, 0), (0, LANES - n_dec)))
    nbc = -(-(past // BLK + TAIL_BLOCKS) // LANES) * LANES
    posT = lambda c: jnp.transpose(c, (0, 2, 3, 1)).reshape(c.shape[0], D_KV, c.shape[1])
    caches = tuple(posT(c[0]) for c in (cache_k_cmp, cache_v_cmp, cache_k_slc, cache_v_slc))
    o_s, s_kwT, s_vwT = _nsa_sample(
        page_table, caches, qbd, gates3, tails, tailsT, posT(cache_k_win[0]), posT(cache_v_win[0]),
        _sample_tables(rel_bias, past, n_dec, nbuf, nbc), cwk + cwv, n_dec)
    o6 = o_s.reshape(nbatch, N_KV, n_dec, HPG, N_KV, HEAD_DIM)
    o_diag = jnp.stack([o6[:, g, :, :, g, :] for g in range(N_KV)], axis=1)
    o_tok = jnp.transpose(o_diag, (2, 0, 1, 3, 4)).reshape(tok, d_attn).astype(BF)
    y_s = _merge(xs2d, o_tok, sza_s, yc_s, rep(gate[batch:n_c]), woa, woc, fg, tok, tok)
    y_sample = jnp.transpose(y_s.reshape(n_dec, nbatch, d), (1, 0, 2))
    new = lambda n: kv6[:, n].reshape(1, nbatch, n_dec, N_KV, HEAD_DIM)
    cv_out = jnp.transpose(cv_s.reshape(CONV_W - 1, nbatch, d_conv), (1, 0, 2))[None]
    s_states = (new(0), new(1), new(2), new(3), heads(s_kwT), heads(s_vwT), cv_out)

    return (y_prompt, y_sample) + p_states + s_states
```

```python
import functools
import math

import numpy as np
import jax
import jax.numpy as jnp
from jax import lax
from jax.experimental import pallas as pl
from jax.experimental.pallas import tpu as pltpu

BF = jnp.bfloat16
F32 = jnp.float32
I32 = jnp.int32

HEAD_DIM = 64
N_KV = 2
HPG = 8
N_HEADS = N_KV * HPG
D_KV = N_KV * HEAD_DIM
BLK = 64
BLK_SHIFT = 6
TAIL_BLOCKS = 16
XPITCH = 136
XTAIL_PAGES = 8
N_SEL = 16
WINDOW = 512
CMP_HID = 2 * HEAD_DIM
CONV_W = 3
N_BUCKETS = 32
MAX_DIST = 128
EPS = 1e-6
NEG = -1e30
M_INIT = -5e29
VALID_THRESH = -1e29
LOG2E = math.log2(math.e)
QT = 128
KT = 128
FAR_GROUP = 4
ACC_ROWS = HEAD_DIM + 16
NCOL = N_HEADS * QT
GCOL = HPG * QT
LANES = 128
SAMPLE_KEY_TILE = 8192
V7X_VMEM_LIMIT = 56 * 1024 * 1024


def _dot(a, b):
    return jnp.dot(a, b, preferred_element_type=F32)


def _dot_nt(a, b):
    return lax.dot_general(a, b, (((1,), (1,)), ((), ())), preferred_element_type=F32)


def _sigmoid(x):
    return 1.0 / (1.0 + jnp.exp(-x))


def _silu(x):
    return x * _sigmoid(x)


def _ada_kernel(c_ref, w_ref, b_ref, o_ref):
    o_ref[...] = _dot(c_ref[...].astype(BF), w_ref[...].astype(BF)) + b_ref[...]


def _ada(c_all, ada_w, ada_b):
    rows, d = c_all.shape
    n = ada_w.shape[1]
    return pl.pallas_call(
        _ada_kernel,
        grid=(n // d,),
        in_specs=[pl.BlockSpec((rows, d), lambda j: (0, 0)),
                  pl.BlockSpec((d, d), lambda j: (0, j)),
                  pl.BlockSpec((1, d), lambda j: (0, j))],
        out_specs=pl.BlockSpec((rows, d), lambda j: (0, j)),
        out_shape=jax.ShapeDtypeStruct((rows, n), F32),
        name="ada_modulation",
    )(c_all, ada_w, ada_b.reshape(1, n))


def _proj_body(x_ref, sc_ref, sh_ref, ng_ref, wq_ref, wkv_ref, wg_ref, wc_ref, wz_ref, cw_ref):
    x = x_ref[...]
    ms = jnp.mean(x * x, axis=-1, keepdims=True)
    xn = x * lax.rsqrt(ms + EPS) * ng_ref[...]
    h = xn * sc_ref[0] + sh_ref[0]
    hb = h.astype(BF)
    q = _dot(hb, wq_ref[...]) * (HEAD_DIM ** -0.5 * LOG2E)
    kv = _dot(hb, wkv_ref[...])
    gates = _sigmoid(_dot(hb, wg_ref[...]))
    c3 = _dot(hb, wc_ref[...])
    dc = c3.shape[1] // 3
    hc, bc, cc = c3[:, :dc], c3[:, dc:2 * dc], c3[:, 2 * dc:]
    u = cc * hc
    z = _dot(hb, wz_ref[...])
    da = z.shape[1] - dc
    sza = _silu(z[:, :da])
    szc = _silu(z[:, da:])
    return q, kv, gates, u, bc, sza, szc


def _proj_prompt_kernel(x_ref, sc_ref, sh_ref, ng_ref, wq_ref, wkv_ref, wg_ref, wc_ref, wz_ref, cw_ref,
                        qT_ref, kcT_ref, vcT_ref, ksT_ref, vsT_ref, kwT_ref, vwT_ref, kc_ref, vc_ref,
                        kslc_ref, vslcT_ref, kwin_ref, vwinT_ref, gates_ref, yc_ref, sza_ref, cv_ref,
                        uext_ref, *, tiles_per_batch, tm):
    t = pl.program_id(0)
    q, kv, gates, u, bc, sza, szc = _proj_body(
        x_ref, sc_ref, sh_ref, ng_ref, wq_ref, wkv_ref, wg_ref, wc_ref, wz_ref, cw_ref)
    qT_ref[0] = q.T.astype(BF)
    kvT = [kv[:, n * D_KV:(n + 1) * D_KV].T for n in range(6)]
    for n, r in enumerate((kcT_ref, vcT_ref, ksT_ref, vsT_ref, kwT_ref, vwT_ref)):
        r[0] = kvT[n]
    kc_ref[...] = kv[:, 0:D_KV]
    vc_ref[...] = kv[:, D_KV:2 * D_KV]
    kslc_ref[0] = kv[:, 2 * D_KV:3 * D_KV].astype(BF)
    vslcT_ref[0] = kvT[3].astype(BF)
    kwin_ref[0] = kv[:, 4 * D_KV:5 * D_KV].astype(BF)
    vwinT_ref[0] = kvT[5].astype(BF)
    gates_ref[...] = gates

    first = (t % tiles_per_batch) == 0

    @pl.when(first)
    def _():
        uext_ref[0:8, :] = jnp.zeros((8, u.shape[1]), F32)

    @pl.when(jnp.logical_not(first))
    def _():
        uext_ref[0:8, :] = uext_ref[tm:tm + 8, :]

    uext_ref[8:8 + tm, :] = u
    y = (cw_ref[2:3, :] * u + cw_ref[1:2, :] * uext_ref[7:7 + tm, :]
         + cw_ref[0:1, :] * uext_ref[6:6 + tm, :])
    yc_ref[...] = (bc * y * szc).astype(BF)
    sza_ref[...] = sza.astype(BF)
    cv_ref[0] = u[tm - (CONV_W - 1):, :]


def _proj_sample_kernel(x_ref, sc_ref, sh_ref, ng_ref, wq_ref, wkv_ref, wg_ref, wc_ref, wz_ref, cw_ref,
                        st_ref, q_ref, kv_ref, gates_ref, yc_ref, sza_ref, cv_ref, uext_ref, *, nb, nt):
    q, kv, gates, u, bc, sza, szc = _proj_body(
        x_ref, sc_ref, sh_ref, ng_ref, wq_ref, wkv_ref, wg_ref, wc_ref, wz_ref, cw_ref)
    q_ref[...] = q
    kv_ref[...] = kv
    gates_ref[...] = gates
    ns = (CONV_W - 1) * nb
    uext_ref[0:ns, :] = st_ref[...]
    uext_ref[ns:ns + nt * nb, :] = u
    y = cw_ref[2:3, :] * u
    for k in range(CONV_W - 1):
        y = y + cw_ref[k:k + 1, :] * uext_ref[k * nb:k * nb + nt * nb, :]
    yc_ref[...] = (bc * y * szc).astype(BF)
    sza_ref[...] = sza.astype(BF)
    cv_ref[...] = uext_ref[nt * nb:nt * nb + ns, :]


def _const_spec(shape):
    nd = len(shape)
    return pl.BlockSpec(shape, lambda *_: (0,) * nd, pipeline_mode=pl.Buffered(1))


def _proj_prompt(x2d, sc, sh, ng, wts, cw, batch, seq, tm):
    tokens, d = x2d.shape
    tpb = seq // tm
    wq, wkv, wg, wc, wz = wts
    dc = wc.shape[1] // 3
    da = wz.shape[1] - dc
    tok_spec = lambda w: pl.BlockSpec((tm, w), lambda t: (t, 0))
    rowT_spec = lambda r: pl.BlockSpec((1, r, tm), lambda t: (t // tpb, 0, t % tpb))
    row_spec = lambda w: pl.BlockSpec((1, tm, w), lambda t: (t // tpb, t % tpb, 0))
    mod_spec = pl.BlockSpec((1, 1, d), lambda t: (t // tpb, 0, 0))
    out_shape = (
        [jax.ShapeDtypeStruct((batch, wq.shape[1], seq), BF)]
        + [jax.ShapeDtypeStruct((batch, D_KV, seq), F32)] * 6
        + [jax.ShapeDtypeStruct((tokens, D_KV), F32)] * 2
        + [jax.ShapeDtypeStruct((batch, seq, D_KV), BF), jax.ShapeDtypeStruct((batch, D_KV, seq), BF),
           jax.ShapeDtypeStruct((batch, seq, D_KV), BF), jax.ShapeDtypeStruct((batch, D_KV, seq), BF),
           jax.ShapeDtypeStruct((tokens, LANES), F32),
           jax.ShapeDtypeStruct((tokens, dc), BF), jax.ShapeDtypeStruct((tokens, da), BF),
           jax.ShapeDtypeStruct((batch, CONV_W - 1, dc), F32)])
    out_specs = (
        [rowT_spec(wq.shape[1])] + [rowT_spec(D_KV)] * 6 + [tok_spec(D_KV)] * 2
        + [row_spec(D_KV), rowT_spec(D_KV), row_spec(D_KV), rowT_spec(D_KV),
           tok_spec(LANES), tok_spec(dc), tok_spec(da),
           pl.BlockSpec((1, CONV_W - 1, dc), lambda t: (t // tpb, 0, 0))])
    return pl.pallas_call(
        functools.partial(_proj_prompt_kernel, tiles_per_batch=tpb, tm=tm),
        grid=(tokens // tm,),
        in_specs=[tok_spec(d), mod_spec, mod_spec, _const_spec((1, d)),
                  _const_spec(wq.shape), _const_spec(wkv.shape), _const_spec(wg.shape),
                  _const_spec(wc.shape), _const_spec(wz.shape), _const_spec(cw.shape)],
        out_specs=out_specs,
        out_shape=out_shape,
        scratch_shapes=[pltpu.VMEM((tm + 8, dc), F32)],
        compiler_params=pltpu.CompilerParams(dimension_semantics=("arbitrary",),
                                             vmem_limit_bytes=V7X_VMEM_LIMIT),
        name="proj_prompt",
    )(x2d, sc, sh, ng, wq, wkv, wg, wc, wz, cw)


def _proj_sample(x2d, sc, sh, ng, wts, cw, state2d, nb, nt):
    tokens, d = x2d.shape
    wq, wkv, wg, wc, wz = wts
    dc = wc.shape[1] // 3
    da = wz.shape[1] - dc
    ns = (CONV_W - 1) * nb
    full = lambda a: pl.BlockSpec(a.shape, lambda i: (0,) * a.ndim)
    ins = (x2d, sc, sh, ng, wq, wkv, wg, wc, wz, cw, state2d)
    out_shape = [jax.ShapeDtypeStruct((tokens, wq.shape[1]), F32),
                 jax.ShapeDtypeStruct((tokens, wkv.shape[1]), F32),
                 jax.ShapeDtypeStruct((tokens, LANES), F32),
                 jax.ShapeDtypeStruct((tokens, dc), BF), jax.ShapeDtypeStruct((tokens, da), BF),
                 jax.ShapeDtypeStruct((ns, dc), F32)]
    return pl.pallas_call(
        functools.partial(_proj_sample_kernel, nb=nb, nt=nt),
        grid=(1,),
        in_specs=[full(a) for a in ins],
        out_specs=[pl.BlockSpec(s.shape, lambda i: (0, 0)) for s in out_shape],
        out_shape=out_shape,
        scratch_shapes=[pltpu.VMEM((ns + tokens, dc), F32)],
        compiler_params=pltpu.CompilerParams(vmem_limit_bytes=V7X_VMEM_LIMIT),
        name="proj_sample",
    )(*ins)


def _compress(x_ref, nblk, pe_ref, w1_ref, w2_ref):
    def body(pp, acc):
        xp = jnp.concatenate([x_ref[pl.ds(2 * pp, nblk, stride=BLK), :],
                              x_ref[pl.ds(2 * pp + 1, nblk, stride=BLK), :]], axis=1) + pe_ref[pl.ds(pp, 1), :]
        return acc + _dot(xp.astype(BF), w1_ref[pp])
    acc = lax.fori_loop(0, BLK // 2, body, jnp.zeros((nblk, N_KV * CMP_HID), F32), unroll=2)
    return _dot(_silu(acc).astype(BF), w2_ref[...])


def _compress_prompt_kernel(kc_ref, vc_ref, pek_ref, w1k_ref, w2k_ref, pev_ref, w1v_ref, w2v_ref,
                            kcmp_ref, vcmpT_ref, *, nblk):
    kcmp_ref[0] = _compress(kc_ref.at[0], nblk, pek_ref, w1k_ref, w2k_ref).astype(BF)
    vcmpT_ref[0] = _compress(vc_ref.at[0], nblk, pev_ref, w1v_ref, w2v_ref).T.astype(BF)


def _compress_prompt(kc, vc, cw, batch, seq):
    nblk = seq // BLK
    row = pl.BlockSpec((1, seq, D_KV), lambda b: (b, 0, 0))
    return pl.pallas_call(
        functools.partial(_compress_prompt_kernel, nblk=nblk),
        grid=(batch,),
        in_specs=[row, row] + [_const_spec(a.shape) for a in cw],
        out_specs=[pl.BlockSpec((1, nblk, D_KV), lambda b: (b, 0, 0)),
                   pl.BlockSpec((1, D_KV, nblk), lambda b: (b, 0, 0))],
        out_shape=[jax.ShapeDtypeStruct((batch, nblk, D_KV), BF),
                   jax.ShapeDtypeStruct((batch, D_KV, nblk), BF)],
        compiler_params=pltpu.CompilerParams(vmem_limit_bytes=V7X_VMEM_LIMIT),
        name="compress_prompt",
    )(kc, vc, *cw)


def _sortable_key(score):
    bits = pltpu.bitcast(score, I32)
    return jnp.where(bits < 0, bits ^ jnp.int32(0x7FFFFFFF), bits)


def _select_topk(score, tri, axis, k):
    key = _sortable_key(score)
    cnt = lambda m: jnp.sum(jnp.where(m, 1.0, 0.0), axis=axis, keepdims=True)
    int_min = jnp.int32(-2 ** 31)
    kf = float(k)
    thr = jnp.where(cnt(key >= 0) >= kf, jnp.int32(0), int_min)
    for hi in range(30, 0, -2):
        c1 = thr | jnp.int32(1 << (hi - 1))
        c2 = thr | jnp.int32(1 << hi)
        c3 = c2 | jnp.int32(1 << (hi - 1))
        thr = jnp.where(cnt(key >= c3) >= kf, c3,
                        jnp.where(cnt(key >= c2) >= kf, c2, jnp.where(cnt(key >= c1) >= kf, c1, thr)))
    cand = thr | jnp.int32(1)
    thr = jnp.where(cnt(key >= cand) >= kf, cand, thr)
    gt = key > thr
    tie = key == thr
    tie_bf = jnp.where(tie, 1.0, 0.0).astype(BF)
    lower_ties = _dot(tri, tie_bf) if axis == 0 else _dot(tie_bf, tri)
    return jnp.logical_or(gt, jnp.logical_and(tie, lower_ties + cnt(gt) < kf))


def _attend(br, lhs, rhs_ref, krows, bias, vT, m_ref, acc_ref):
    s = _dot(lhs, rhs_ref[0:krows, :])
    if bias is not None:
        s = s + bias
    _consume(br, s, vT, m_ref, acc_ref)


SUB = 8


def _col_max(s):
    part = jnp.max(s.reshape(s.shape[0] // SUB, SUB, s.shape[1]), axis=0)
    return jnp.broadcast_to(jnp.max(part, axis=0, keepdims=True), part.shape)


def _rows_op(op, x, r):
    return op(x.reshape(x.shape[0] // SUB, SUB, x.shape[1]), r[None]).reshape(x.shape)


def _pv_update(br, pb, vT, scale_old, scale_new, acc_ref):
    ones = jnp.ones((ACC_ROWS - HEAD_DIM, pb.shape[0]), BF)
    for g in range(N_KV):
        gs = slice(g * GCOL, (g + 1) * GCOL)
        v_aug = jnp.concatenate([vT[g * HEAD_DIM:(g + 1) * HEAD_DIM, :], ones], axis=0)
        acc = acc_ref[br, g]
        if scale_old is not None:
            acc = _rows_op(jnp.multiply, acc, scale_old[:, gs])
        acc = acc + _dot(v_aug, pb[:, gs])
        if scale_new is not None:
            acc = _rows_op(jnp.multiply, acc, scale_new[:, gs])
        acc_ref[br, g] = acc


def _consume(br, s, vT, m_ref, acc_ref):
    m_old = m_ref[br]
    m_new = jnp.maximum(m_old, _col_max(s))
    alpha = jnp.exp2(m_old - m_new)
    pb = jnp.exp2(_rows_op(jnp.subtract, s, m_new)).astype(BF)
    m_ref[br] = m_new
    _pv_update(br, pb, vT, alpha, None, acc_ref)


def _nsa_prompt_kernel(qT_ref, gates_ref, kslc_ref, vslcT_ref, kwin_ref, vwinT_ref, kcmp_ref, vcmpT_ref,
                       cb_ref, sb_ref, w0_ref, tri_ref, o_ref,
                       rhs_ref, m_ref, acc_ref, s0_ref, s1_ref, *, nb):
    i = pl.program_id(1)

    for gh in range(N_HEADS):
        g = gh // HPG
        cols = slice(gh * QT, (gh + 1) * QT)
        rhs_ref[g * HEAD_DIM:(g + 1) * HEAD_DIM, cols] = qT_ref[0, gh * HEAD_DIM:(gh + 1) * HEAD_DIM, :]
        og = 1 - g
        rhs_ref[og * HEAD_DIM:(og + 1) * HEAD_DIM, cols] = jnp.zeros((HEAD_DIM, QT), BF)

    lg = _dot(kcmp_ref[0], rhs_ref[0:D_KV, :])
    rel = 2 * i - lax.broadcasted_iota(I32, (nb, NCOL), 0)
    bias = jnp.where(rel >= 3, 0.0, NEG)
    for rr in range(4):
        bias = jnp.where(rel == rr - 1, cb_ref[rr:rr + 1, :], bias)
    lgb = lg + bias
    mc = jnp.max(lgb, axis=0, keepdims=True)
    pc = jnp.exp2(lgb - mc)
    lc = jnp.sum(pc, axis=0, keepdims=True)
    pn = pc * jnp.where(mc > VALID_THRESH, 1.0 / lc, 0.0)
    pnb = pn.astype(BF)
    for g in range(N_KV):
        acc_ref[0, g, 0:HEAD_DIM, :] = _dot(vcmpT_ref[0, g * HEAD_DIM:(g + 1) * HEAD_DIM, :],
                                            pnb[:, g * GCOL:(g + 1) * GCOL])

    imp, valid = [], []
    for g in range(N_KV):
        a = pn[:, g * GCOL:g * GCOL + QT]
        for h in range(1, HPG):
            a = a + pn[:, g * GCOL + h * QT:g * GCOL + (h + 1) * QT]
        imp.append(a)
        valid.append(lgb[:, g * GCOL:g * GCOL + QT] > VALID_THRESH)
    imp = jnp.concatenate(imp, axis=1)
    valid = jnp.concatenate(valid, axis=1)
    jrow = lax.broadcasted_iota(I32, (nb, N_KV * QT), 0)
    tpos = i * QT + (lax.broadcasted_iota(I32, (nb, N_KV * QT), 1) & (QT - 1))
    score = jnp.where(valid, imp, jnp.where(jrow * BLK <= tpos, 2.0, -1.0))
    sel = _select_topk(score, tri_ref[...], 0, min(N_SEL, nb))
    selbias = jnp.where(jnp.logical_and(sel, score >= 0.0), 0.0, NEG).astype(BF)
    for gh in range(N_HEADS):
        g = gh // HPG
        rhs_ref[D_KV:D_KV + nb, gh * QT:(gh + 1) * QT] = selbias[:, g * QT:(g + 1) * QT]

    for br in (1, 2):
        m_ref[br] = jnp.full((SUB, NCOL), M_INIT, F32)
        acc_ref[br] = jnp.zeros((N_KV, ACC_ROWS, GCOL), F32)

    def sel_keys(kt, tk):
        k0 = pl.multiple_of(kt * KT, KT)
        k = kslc_ref[0, pl.ds(k0, tk), :]
        jj = lax.broadcasted_iota(I32, (tk, nb), 1)
        kk = lax.broadcasted_iota(I32, (tk, nb), 0)
        onehot = jnp.where(jj == (KT // BLK) * kt + jnp.right_shift(kk, BLK_SHIFT), 1.0, 0.0).astype(BF)
        return jnp.concatenate([k, onehot], axis=1), vslcT_ref[0, :, pl.ds(k0, tk)]

    def sel_tile(kt, tk, bias_tile):
        ka, vT = sel_keys(kt, tk)
        _attend(1, ka, rhs_ref, D_KV + nb, bias_tile, vT, m_ref, acc_ref)

    n_far = jnp.maximum(i - 1, 0)
    n_big = n_far // FAR_GROUP
    n_pairs = n_big // 2
    big = FAR_GROUP * KT

    def far_scores(j):
        ka, _ = sel_keys(j * FAR_GROUP, big)
        return _dot(ka, rhs_ref[...])

    def far_step(cur_ref, nxt_ref, j_cur, j_nxt):
        m_cur = m_ref[1]
        pb = jnp.exp2(_rows_op(jnp.subtract, cur_ref[...], m_cur)).astype(BF)
        s_nxt = far_scores(j_nxt)
        nxt_ref[...] = s_nxt
        m_nxt = jnp.maximum(m_cur, _col_max(s_nxt))
        vT = vslcT_ref[0, :, pl.ds(pl.multiple_of(j_cur * big, big), big)]
        _pv_update(1, pb, vT, None, jnp.exp2(m_cur - m_nxt), acc_ref)
        m_ref[1] = m_nxt

    @pl.when(n_pairs > 0)
    def _():
        s = far_scores(0)
        s0_ref[...] = s
        m_ref[1] = jnp.maximum(m_ref[1], _col_max(s))

    def far_pair(jp, c):
        @pl.when(jp < n_pairs)
        def _():
            far_step(s0_ref, s1_ref, 2 * jp, 2 * jp + 1)

        @pl.when(2 * jp + 1 < n_big)
        def _():
            far_step(s1_ref, s0_ref, 2 * jp + 1, jnp.minimum(2 * jp + 2, 2 * n_pairs - 1))
        return c
    lax.fori_loop(0, n_pairs, far_pair, 0)

    @pl.when(n_big > 2 * n_pairs)
    def _():
        sel_tile((n_big - 1) * FAR_GROUP, big, None)

    def far_small(kt, c):
        sel_tile(kt, KT, None)
        return c
    lax.fori_loop(n_big * FAR_GROUP, n_far, far_small, 0)

    @pl.when(i >= 1)
    def _():
        sel_tile(i - 1, 2 * KT, sb_ref[...])

    @pl.when(i == 0)
    def _():
        sel_tile(0, KT, sb_ref[KT:2 * KT, :])

    n_win = WINDOW // KT + 1

    @pl.when(i >= n_win - 1)
    def _():
        k0 = pl.multiple_of((i - (n_win - 1)) * KT, KT)
        bias = jnp.concatenate(
            [w0_ref[...], jnp.zeros(((n_win - 3) * KT, NCOL), F32), sb_ref[...]], axis=0)
        _attend(2, kwin_ref[0, pl.ds(k0, n_win * KT), :], rhs_ref, D_KV, bias,
                vwinT_ref[0, :, pl.ds(k0, n_win * KT)], m_ref, acc_ref)

    @pl.when(i < n_win - 1)
    def _():
        for r in range(1, n_win):
            kt = i - (n_win - 1) + r

            @pl.when(kt >= 0)
            def _():
                k0 = pl.multiple_of(kt * KT, KT)
                r0 = (r - (n_win - 2)) * KT
                bias = sb_ref[r0:r0 + KT, :] if r >= n_win - 2 else None
                _attend(2, kwin_ref[0, pl.ds(k0, KT), :], rhs_ref, D_KV, bias,
                        vwinT_ref[0, :, pl.ds(k0, KT)], m_ref, acc_ref)

    gts = gates_ref[0, 0]
    for g in range(N_KV):
        cs = slice(g * GCOL, (g + 1) * GCOL)
        o = gts[0:1, cs] * acc_ref[0, g, 0:HEAD_DIM, :]
        for br in (1, 2):
            l = acc_ref[br, g, HEAD_DIM:HEAD_DIM + 1, :]
            o = o + (gts[br:br + 1, cs] * jnp.where(l > 0.0, 1.0 / l, 0.0)) * acc_ref[br, g, 0:HEAD_DIM, :]
        for hp in range(HPG // 2):
            pair = jnp.concatenate([o[:, (2 * hp) * QT:(2 * hp + 1) * QT],
                                    o[:, (2 * hp + 1) * QT:(2 * hp + 2) * QT]], axis=0)
            c0 = g * HPG * HEAD_DIM + hp * 2 * HEAD_DIM
            o_ref[0, :, c0:c0 + 2 * HEAD_DIM] = pair.T.astype(BF)


def _nsa_prompt(qT, gates, kslc, vslcT, kwin, vwinT, kcmp, vcmpT, tables, batch, seq):
    nb = seq // BLK
    nq = seq // QT
    cb, sb, w0, tri = tables
    dq = qT.shape[1]
    per_b = lambda shape: pl.BlockSpec((1,) + shape, lambda b, i: (b, 0, 0))
    return pl.pallas_call(
        functools.partial(_nsa_prompt_kernel, nb=nb),
        grid=(batch, nq),
        in_specs=[pl.BlockSpec((1, dq, QT), lambda b, i: (b, 0, i)),
                  pl.BlockSpec((1, 1, 8, NCOL), lambda b, i: (b, i, 0, 0)),
                  per_b((seq, D_KV)), per_b((D_KV, seq)), per_b((seq, D_KV)), per_b((D_KV, seq)),
                  per_b((nb, D_KV)), per_b((D_KV, nb)),
                  _const_spec(cb.shape), _const_spec(sb.shape), _const_spec(w0.shape),
                  _const_spec(tri.shape)],
        out_specs=pl.BlockSpec((1, QT, dq), lambda b, i: (b, i, 0)),
        out_shape=jax.ShapeDtypeStruct((batch, seq, dq), BF),
        scratch_shapes=[pltpu.VMEM((D_KV + nb, NCOL), BF),
                        pltpu.VMEM((3, SUB, NCOL), F32),
                        pltpu.VMEM((3, N_KV, ACC_ROWS, GCOL), F32),
                        pltpu.VMEM((FAR_GROUP * KT, NCOL), F32),
                        pltpu.VMEM((FAR_GROUP * KT, NCOL), F32)],
        compiler_params=pltpu.CompilerParams(dimension_semantics=("arbitrary", "arbitrary"),
                                             vmem_limit_bytes=V7X_VMEM_LIMIT),
        name="nsa_prompt",
    )(qT, gates, kslc, vslcT, kwin, vwinT, kcmp, vcmpT, cb, sb, w0, tri)


def _merge_kernel(x_ref, o_ref, sza_ref, yc_ref, gate_ref, woa_ref, woc_ref, fg_ref, y_ref):
    og = (o_ref[...].astype(F32) * sza_ref[...].astype(F32)).astype(BF)
    acc = _dot(og, woa_ref[...]) + _dot(yc_ref[...], woc_ref[...])
    xo = x_ref[...] + gate_ref[0] * acc
    ms = jnp.mean(xo * xo, axis=-1, keepdims=True)
    y_ref[...] = xo * lax.rsqrt(ms + EPS) * fg_ref[...]


def _merge(x2d, o2d, sza, yc, gate, woa, woc, fg, tm, rows_per_gate):
    tokens, d = x2d.shape
    tok = lambda w: pl.BlockSpec((tm, w), lambda t: (t, 0))
    gr = gate.shape[1]
    return pl.pallas_call(
        _merge_kernel,
        grid=(tokens // tm,),
        in_specs=[tok(d), tok(o2d.shape[1]), tok(sza.shape[1]), tok(yc.shape[1]),
                  pl.BlockSpec((1, gr, d), lambda t: ((t * tm) // rows_per_gate, 0, 0)),
                  _const_spec(woa.shape), _const_spec(woc.shape), _const_spec(fg.shape)],
        out_specs=tok(d),
        out_shape=jax.ShapeDtypeStruct((tokens, d), F32),
        compiler_params=pltpu.CompilerParams(dimension_semantics=("arbitrary",),
                                             vmem_limit_bytes=V7X_VMEM_LIMIT),
        name="merge_out",
    )(x2d, o2d, sza, yc, gate, woa, woc, fg)


def _softmax_rows(parts):
    m = parts[0].max(axis=1, keepdims=True)
    for s in parts[1:]:
        m = jnp.maximum(m, s.max(axis=1, keepdims=True))
    ps = [jnp.exp2(s - m) for s in parts]
    l = ps[0].sum(axis=1, keepdims=True)
    for p in ps[1:]:
        l = l + p.sum(axis=1, keepdims=True)
    inv = jnp.where(jnp.logical_and(m > VALID_THRESH, l > 0.0), 1.0 / l, 0.0)
    return ps, inv


def _nsa_sample_kernel(pt_ref, ckc_ref, cvc_ref, cks_ref, cvs_ref,
                       qbd_ref, gates_ref, tails_ref, tailsT_ref, bkwT_ref, bvwT_ref,
                       cbs_ref, sbl_ref, sbt_ref, wbl_ref, wbt_ref, elocT_ref, tri_ref,
                       pek_ref, w1k_ref, w2k_ref, pev_ref, w1v_ref, w2v_ref,
                       o_ref, skwT_ref, svwT_ref,
                       buf_ref, sem, xbuf_ref, cmpf_ref, kcmp_ref, vcmp_ref, sbuf_ref, pbuf_ref,
                       *, n_pages, page, n_dec):
    b = pl.program_id(0)
    nbatch = pl.num_programs(0)
    past = n_pages * page
    nbp = past // BLK
    nbc = kcmp_ref.shape[0]
    n_tiles = past // SAMPLE_KEY_TILE
    nrow = qbd_ref.shape[1]
    caches = (ckc_ref, cvc_ref, cks_ref, cvs_ref)

    def page_copy(cache_ref, pg, j, slot):
        return pltpu.make_async_copy(cache_ref.at[pg],
                                     buf_ref.at[slot, :, pl.ds(pl.multiple_of(j * page, page), page)],
                                     sem.at[slot])

    def start_fetch(cache_ref, bb, slot):
        def body(j, c):
            page_copy(cache_ref, pt_ref[bb, j], j, slot).start()
            return c
        lax.fori_loop(0, n_pages, body, 0)

    def wait_fetch(cache_ref, slot):
        def body(j, c):
            page_copy(cache_ref, 0, j, slot).wait()
            return c
        lax.fori_loop(0, n_pages, body, 0)

    def stage(c):
        if c + 1 < len(caches):
            start_fetch(caches[c + 1], b, (c + 1) % 2)
        else:
            @pl.when(b + 1 < nbatch)
            def _():
                start_fetch(caches[0], b + 1, 0)
        wait_fetch(caches[c], c % 2)
        return buf_ref.at[c % 2]

    @pl.when(b == 0)
    def _():
        start_fetch(caches[0], 0, 0)

    qbd = qbd_ref[0]

    def compress_cache(c, slot_ref, pe_ref, w1_ref, w2_ref, dst_ref):
        def xpose(j, carry):
            src = slot_ref[:, pl.ds(pl.multiple_of(j * page, page), page)]
            xbuf_ref[pl.ds(pl.multiple_of(j * XPITCH, 8), page), :] = src.T
            return carry
        lax.fori_loop(0, n_pages, xpose, 0, unroll=8)
        n_extra = XTAIL_PAGES * XPITCH
        xbuf_ref[pl.ds(n_pages * XPITCH, n_extra), :] = jnp.zeros((n_extra, D_KV), F32)
        xbuf_ref[pl.ds(n_pages * XPITCH, 8), :] = tails_ref[0, c]
        n_even = n_pages + XTAIL_PAGES

        def rows(r0, n):
            return jnp.concatenate([xbuf_ref[pl.ds(r0, n, stride=XPITCH), :],
                                    xbuf_ref[pl.ds(r0 + 1, n, stride=XPITCH), :]], axis=1)

        def body(pp, acc):
            acc_e, acc_o = acc
            pe = pe_ref[pl.ds(pp, 1), :]
            w = w1_ref[pp]
            xe = rows(2 * pp, n_even) + pe
            xo = rows(2 * pp + BLK, n_pages) + pe
            return acc_e + _dot(xe.astype(BF), w), acc_o + _dot(xo.astype(BF), w)
        acc_e, acc_o = lax.fori_loop(
            0, BLK // 2, body,
            (jnp.zeros((n_even, N_KV * CMP_HID), F32), jnp.zeros((n_pages, N_KV * CMP_HID), F32)), unroll=2)
        out_e = _dot(_silu(acc_e).astype(BF), w2_ref[...])
        out_o = _dot(_silu(acc_o).astype(BF), w2_ref[...])
        cmpf_ref[pl.ds(0, n_pages, stride=2), :] = out_e[0:n_pages]
        cmpf_ref[pl.ds(1, n_pages, stride=2), :] = out_o
        cmpf_ref[nbp:nbp + XTAIL_PAGES, :] = out_e[n_pages:n_even]
        cmpf_ref[nbp + XTAIL_PAGES:nbp + TAIL_BLOCKS, :] = jnp.zeros((TAIL_BLOCKS - XTAIL_PAGES, D_KV), F32)
        dst_ref[0:nbp + TAIL_BLOCKS, :] = cmpf_ref[...].astype(BF)
        dst_ref[nbp + TAIL_BLOCKS:nbc, :] = jnp.zeros((nbc - nbp - TAIL_BLOCKS, D_KV), BF)

    compress_cache(0, stage(0), pek_ref, w1k_ref, w2k_ref, kcmp_ref)
    compress_cache(1, stage(1), pev_ref, w1v_ref, w2v_ref, vcmp_ref)

    lgc = _dot_nt(qbd, kcmp_ref[...]) + cbs_ref[...]
    (pc,), invc = _softmax_rows([lgc])
    pn = pc * invc
    o_cmp = _dot(pn.astype(BF), vcmp_ref[...])
    imp = jnp.sum(pn.reshape(nrow // HPG, HPG, nbc), axis=1)
    jcol = lax.broadcasted_iota(I32, imp.shape, 1)
    score = jnp.where(jcol < nbp, imp, jnp.where(jcol == nbp, 2.0, -1.0))
    sel = _select_topk(score, tri_ref[...], 1, N_SEL)
    selbias = jnp.where(jnp.logical_and(sel, score >= 0.0), 0.0, NEG)
    selb = jnp.broadcast_to(selbias[:, None, :], (nrow // HPG, HPG, nbc)).reshape(nrow, nbc).astype(BF)

    kslot = stage(2)
    tile = lambda t: slice(t * SAMPLE_KEY_TILE, (t + 1) * SAMPLE_KEY_TILE)
    for t in range(n_tiles):
        sbuf_ref[t] = (_dot(qbd, kslot[:, tile(t)].astype(BF))
                       + _dot(selb[:, t * LANES:(t + 1) * LANES], elocT_ref[...]))
    last = slice(SAMPLE_KEY_TILE - LANES, SAMPLE_KEY_TILE)
    sbuf_ref[n_tiles - 1, :, last] = sbuf_ref[n_tiles - 1, :, last] + sbl_ref[...]
    s_tail = (_dot(qbd, tailsT_ref[0, 0].astype(BF))
              + _dot(selb[:, nbp:nbp + LANES], elocT_ref[:, 0:LANES]) + sbt_ref[...])
    ps, inv_sel = _softmax_rows([sbuf_ref[t] for t in range(n_tiles)] + [s_tail])
    for t in range(n_tiles):
        pbuf_ref[t] = ps[t].astype(BF)
    p_tail = ps[n_tiles].astype(BF)

    vslot = stage(3)
    o_sel = _dot_nt(p_tail, tailsT_ref[0, 1].astype(BF))
    for t in range(n_tiles):
        o_sel = o_sel + _dot_nt(pbuf_ref[t], vslot[:, tile(t)].astype(BF))
    o_sel = o_sel * inv_sel

    s_w = _dot(qbd, bkwT_ref[0].astype(BF)) + wbl_ref[...]
    s_wt = _dot(qbd, tailsT_ref[0, 2].astype(BF)) + wbt_ref[...]
    (p_w, p_wt), inv_w = _softmax_rows([s_w, s_wt])
    o_win = (_dot_nt(p_w.astype(BF), bvwT_ref[0].astype(BF))
             + _dot_nt(p_wt.astype(BF), tailsT_ref[0, 3].astype(BF))) * inv_w

    o_ref[0] = gates_ref[0, 0] * o_cmp + gates_ref[0, 1] * o_sel + gates_ref[0, 2] * o_win

    nbuf = bkwT_ref.shape[2]
    lane = lax.broadcasted_iota(I32, (D_KV, LANES), 1)
    for n, src, dst in ((2, bkwT_ref, skwT_ref), (3, bvwT_ref, svwT_ref)):
        rolled = pltpu.roll(src[0], nbuf - n_dec, axis=1)
        new = pltpu.roll(tailsT_ref[0, n], LANES - n_dec, axis=1)
        dst[0, :, 0:nbuf - LANES] = rolled[:, 0:nbuf - LANES]
        dst[0, :, nbuf - LANES:nbuf] = jnp.where(lane >= LANES - n_dec, new, rolled[:, nbuf - LANES:nbuf])


def _nsa_sample(page_table, caches, qbd, gates3, tails, tailsT, bkwT, bvwT, tables, cw, n_dec):
    nbatch, n_pages = page_table.shape
    page = caches[0].shape[2]
    past = n_pages * page
    cbs, sbl, sbt, wbl, wbt, elocT, tri = tables
    nbc = cbs.shape[1]
    nrow = qbd.shape[1]
    nbuf = bkwT.shape[2]
    nbp = past // BLK
    n_tiles = past // SAMPLE_KEY_TILE
    assert past % SAMPLE_KEY_TILE == 0 and nbp % LANES == 0 and page == 2 * BLK and nbuf % LANES == 0
    per_b = lambda a: pl.BlockSpec((1,) + a.shape[1:], lambda b, pt: (b,) + (0,) * (a.ndim - 1))
    const = lambda a: pl.BlockSpec(a.shape, lambda b, pt: (0,) * a.ndim, pipeline_mode=pl.Buffered(1))
    hbm = pl.BlockSpec(memory_space=pl.ANY)
    grid_spec = pltpu.PrefetchScalarGridSpec(
        num_scalar_prefetch=1,
        grid=(nbatch,),
        in_specs=[hbm] * 4 + [per_b(a) for a in (qbd, gates3, tails, tailsT, bkwT, bvwT)]
                 + [const(a) for a in tables] + [const(a) for a in cw],
        out_specs=[pl.BlockSpec((1, nrow, D_KV), lambda b, pt: (b, 0, 0)),
                   pl.BlockSpec((1, D_KV, nbuf), lambda b, pt: (b, 0, 0)),
                   pl.BlockSpec((1, D_KV, nbuf), lambda b, pt: (b, 0, 0))],
        scratch_shapes=[pltpu.VMEM((2, D_KV, past), F32),
                        pltpu.SemaphoreType.DMA((2,)),
                        pltpu.VMEM(((n_pages + XTAIL_PAGES) * XPITCH, D_KV), F32),
                        pltpu.VMEM((nbp + TAIL_BLOCKS, D_KV), F32),
                        pltpu.VMEM((nbc, D_KV), BF),
                        pltpu.VMEM((nbc, D_KV), BF),
                        pltpu.VMEM((n_tiles, nrow, SAMPLE_KEY_TILE), F32),
                        pltpu.VMEM((n_tiles, nrow, SAMPLE_KEY_TILE), BF)])
    return pl.pallas_call(
        functools.partial(_nsa_sample_kernel, n_pages=n_pages, page=page, n_dec=n_dec),
        grid_spec=grid_spec,
        out_shape=[jax.ShapeDtypeStruct((nbatch, nrow, D_KV), F32),
                   jax.ShapeDtypeStruct((nbatch, D_KV, nbuf), F32),
                   jax.ShapeDtypeStruct((nbatch, D_KV, nbuf), F32)],
        compiler_params=pltpu.CompilerParams(dimension_semantics=("arbitrary",),
                                             vmem_limit_bytes=V7X_VMEM_LIMIT),
        name="nsa_sample",
    )(page_table, *caches, qbd, gates3, tails, tailsT, bkwT, bvwT, *tables, *cw)


def _t5_bucket_np(n):
    d = np.maximum(np.arange(n), 0)
    max_exact = N_BUCKETS // 2
    nf = np.maximum(d, max_exact).astype(np.float32)
    large = max_exact + (np.log(nf / np.float32(max_exact)) / np.float32(math.log(MAX_DIST / max_exact))
                         * np.float32(N_BUCKETS - max_exact)).astype(np.int32)
    return np.where(d < max_exact, d, np.minimum(large, N_BUCKETS - 1))


N_NEAR = 256
_BUCKETS = _t5_bucket_np(N_NEAR)
assert np.all(_BUCKETS[QT + 1:] == N_BUCKETS - 1)


def _shifted_bias(rel_bias):
    return (rel_bias[_BUCKETS[:N_NEAR]] - rel_bias[N_BUCKETS - 1][None, :]) * LOG2E


def _bias_cols(tp, dist):
    val = tp[np.clip(dist, 0, N_NEAR - 1)]
    val = jnp.where((dist >= 0)[:, :, None], val, NEG)
    return jnp.transpose(val, (0, 2, 1)).reshape(dist.shape[0], -1)


def _prompt_tables(rel_bias, nb):
    tp = _shifted_bias(rel_bias)
    q = np.arange(QT)[None, :]
    r = np.arange(-1, 7)[:, None]
    d_cmp = np.where(r <= 2, BLK * r - (BLK - 1) + q, -1)
    cb = _bias_cols(tp, d_cmp)
    n_neg, rows = 2 * KT, 2 * KT
    w = jnp.concatenate([jnp.full((N_HEADS, n_neg), NEG, F32), tp.T, jnp.zeros((N_HEADS, 1), F32)], axis=1)
    period = w.shape[1]
    skew = jnp.tile(w, (1, rows))[:, :rows * (period - 1)].reshape(N_HEADS, rows, period - 1)
    c0 = n_neg + KT
    sb = jnp.transpose(skew[:, :, c0:c0 + QT], (1, 0, 2)).reshape(rows, N_HEADS * QT)
    kk = np.arange(KT)[:, None]
    w0 = jnp.tile(jnp.asarray(np.where(kk >= q, 0.0, NEG), F32), (1, N_HEADS))
    idx = np.arange(nb)
    tri = jnp.asarray(idx[None, :] < idx[:, None], BF)
    return cb, sb, w0, tri


def _sample_rows_bias(tp, dist):
    val = tp[np.clip(dist, 0, N_NEAR - 1)]
    val = jnp.where((dist >= 0)[:, :, None], val, NEG)
    val = val.reshape(dist.shape[0], dist.shape[1], N_KV, HPG)
    return jnp.transpose(val, (2, 0, 3, 1)).reshape(-1, dist.shape[1])


def _sample_tables(rel_bias, past, n_dec, nbuf, nbc):
    tp = _shifted_bias(rel_bias)
    q = np.arange(n_dec)[:, None]
    nbp = past // BLK
    j = np.arange(nbc)[None, :]
    cbs = _sample_rows_bias(tp, np.where(j < nbp, past + q - (BLK * j + BLK - 1), -1))
    kk = np.arange(LANES)[None, :]
    sbl = _sample_rows_bias(tp, q + LANES - kk)
    sbt = _sample_rows_bias(tp, q - kk)
    r = np.arange(nbuf)[None, :]
    d_w = nbuf + q - r
    wbl = _sample_rows_bias(tp, np.where(d_w <= WINDOW, d_w, -1))
    wbt = sbt
    kk = np.arange(SAMPLE_KEY_TILE)[None, :]
    elocT = jnp.asarray(kk // BLK == np.arange(LANES)[:, None], BF)
    idx = np.arange(nbc)
    tri = jnp.asarray(idx[:, None] < idx[None, :], BF)
    return cbs, sbl, sbt, wbl, wbt, elocT, tri


def _compress_weights(pe, w1, w2):
    pe2 = jnp.tile(pe, (1, N_KV)).reshape(BLK // 2, 2 * D_KV)
    w1r = w1.reshape(BLK, HEAD_DIM, CMP_HID)
    eye = jnp.eye(N_KV, dtype=w1.dtype)
    w1bd = jnp.einsum("pdh,gk->pgdkh", w1r, eye).reshape(BLK // 2, 2 * D_KV, N_KV * CMP_HID).astype(BF)
    w2bd = jnp.einsum("hd,gk->ghkd", w2, eye).reshape(N_KV * CMP_HID, D_KV).astype(BF)
    return pe2, w1bd, w2bd


def _split_w_in(w_in, d_attn, d_conv, d_mix):
    sizes = (d_attn,) + (D_KV,) * 6 + (3 * N_HEADS,) + (d_conv,) * 3 + (d_mix,)
    offs = np.cumsum((0,) + sizes)
    wq = w_in[:, offs[0]:offs[1]]
    wkv = w_in[:, offs[1]:offs[7]]
    wg = jnp.pad(w_in[:, offs[7]:offs[8]], ((0, 0), (0, LANES - 3 * N_HEADS)))
    wc = w_in[:, offs[8]:offs[11]]
    wz = w_in[:, offs[11]:offs[12]]
    return tuple(w.astype(BF) for w in (wq, wkv, wg, wc, wz))


def kernel(x_prompt, x_sample, cache_k_cmp, cache_v_cmp, cache_k_slc, cache_v_slc, cache_k_win,
           cache_v_win, state_conv, page_table, c_prompt, c_sample, ada_w, ada_b, norm_g, w_in,
           cmp_pe_k, cmp_w1_k, cmp_w2_k, cmp_pe_v, cmp_w1_v, cmp_w2_v, conv_w, w_out, rel_bias, final_g):
    depth = ada_w.shape[0]
    assert depth == 1
    batch, seq, d = x_prompt.shape
    nbatch, n_dec, _ = x_sample.shape
    d_attn = N_HEADS * HEAD_DIM
    d_conv = conv_w.shape[2]
    d_mix = d_attn + d_conv
    n_pages = page_table.shape[1]
    page = cache_k_cmp.shape[2]
    past = n_pages * page
    nbuf = cache_k_win.shape[2]
    tm = 512
    assert seq % tm == 0 and seq >= WINDOW and nbuf == WINDOW

    wts = _split_w_in(w_in[0], d_attn, d_conv, d_mix)
    cwk = _compress_weights(cmp_pe_k[0], cmp_w1_k[0], cmp_w2_k[0])
    cwv = _compress_weights(cmp_pe_v[0], cmp_w1_v[0], cmp_w2_v[0])
    woa = w_out[0, :d_attn].astype(BF)
    woc = w_out[0, d_attn:].astype(BF)
    ng = norm_g[0].reshape(1, d)
    fg = final_g.reshape(1, d)
    cw = conv_w[0]

    n_c = batch + nbatch
    c_all = jnp.pad(jnp.concatenate([c_prompt, c_sample], axis=0), ((0, (-n_c) % 8), (0, 0)))
    mod = _ada(c_all, ada_w[0], ada_b[0])
    shift, scale, gate = mod[:, :d], mod[:, d:2 * d], mod[:, 2 * d:]

    x2d = x_prompt.reshape(batch * seq, d)
    sc_p = (1.0 + scale[:batch]).reshape(batch, 1, d)
    sh_p = shift[:batch].reshape(batch, 1, d)
    (qT, kcT, vcT, ksT, vsT, kwT, vwT, kc, vc, kslc, vslcT, kwin, vwinT, gates_tok, yc, sza, cv_p) = _proj_prompt(
        x2d, sc_p, sh_p, ng, wts, cw, batch, seq, tm)
    kcmp, vcmpT = _compress_prompt(kc.reshape(batch, seq, D_KV), vc.reshape(batch, seq, D_KV),
                                   cwk + cwv, batch, seq)
    nq = seq // QT
    gates = gates_tok[:, :3 * N_HEADS].reshape(batch, nq, QT, 3, N_HEADS)
    gates = jnp.transpose(gates, (0, 1, 3, 4, 2)).reshape(batch, nq, 3, NCOL)
    gates = jnp.pad(gates, ((0, 0), (0, 0), (0, 5), (0, 0)))
    o_p = _nsa_prompt(qT, gates, kslc, vslcT, kwin, vwinT, kcmp, vcmpT,
                      _prompt_tables(rel_bias, seq // BLK), batch, seq)
    y_p = _merge(x2d, o_p.reshape(batch * seq, d_attn), sza, yc, gate[:batch].reshape(batch, 1, d),
                 woa, woc, fg, tm, seq)
    y_prompt = y_p.reshape(batch, seq, d)
    heads = lambda aT: jnp.transpose(aT.reshape(aT.shape[0], N_KV, HEAD_DIM, aT.shape[2]), (0, 3, 1, 2))[None]
    p_states = (heads(kcT), heads(vcT), heads(ksT), heads(vsT),
                heads(kwT[:, :, seq - WINDOW:]), heads(vwT[:, :, seq - WINDOW:]), cv_p[None])

    tok = n_dec * nbatch
    xs2d = jnp.transpose(x_sample, (1, 0, 2)).reshape(tok, d)
    rep = lambda a: jnp.tile(a, (n_dec, 1)).reshape(1, tok, d)
    st2d = jnp.transpose(state_conv[0], (1, 0, 2)).reshape((CONV_W - 1) * nbatch, d_conv)
    q_s, kv_s, gates_s, yc_s, sza_s, cv_s = _proj_sample(
        xs2d, rep(1.0 + scale[batch:n_c]), rep(shift[batch:n_c]), ng, wts, cw, st2d, nbatch, n_dec)

    q5 = jnp.transpose(q_s.reshape(n_dec, nbatch, N_KV, HPG, HEAD_DIM), (1, 2, 0, 3, 4))
    eye = jnp.eye(N_KV, dtype=F32)
    qbd = (q5[:, :, :, :, None, :] * eye[None, :, None, None, :, None]).reshape(
        nbatch, N_KV * n_dec * HPG, D_KV).astype(BF)
    g5 = gates_s[:, :3 * N_HEADS].reshape(n_dec, nbatch, 3, N_KV, HPG)
    g5 = jnp.transpose(g5, (1, 2, 3, 0, 4)).reshape(nbatch, 3, N_KV * n_dec * HPG, 1)
    gates3 = jnp.broadcast_to(g5, (nbatch, 3, N_KV * n_dec * HPG, D_KV))
    kv6 = jnp.transpose(kv_s.reshape(n_dec, nbatch, 6, D_KV), (1, 2, 0, 3))
    tails = jnp.pad(kv6[:, 0:2], ((0, 0), (0, 0), (0, 8 - n_dec), (0, 0)))
    tailsT = jnp.pad(jnp.transpose(kv6[:, 2:6], (0, 1, 3, 2)), ((0, 0), (0, 0), (0, 0), (0, LANES - n_dec)))
    nbc = -(-(past // BLK + TAIL_BLOCKS) // LANES) * LANES
    posT = lambda c: jnp.transpose(c, (0, 2, 3, 1)).reshape(c.shape[0], D_KV, c.shape[1])
    caches = tuple(posT(c[0]) for c in (cache_k_cmp, cache_v_cmp, cache_k_slc, cache_v_slc))
    o_s, s_kwT, s_vwT = _nsa_sample(
        page_table, caches, qbd, gates3, tails, tailsT, posT(cache_k_win[0]), posT(cache_v_win[0]),
        _sample_tables(rel_bias, past, n_dec, nbuf, nbc), cwk + cwv, n_dec)
    o6 = o_s.reshape(nbatch, N_KV, n_dec, HPG, N_KV, HEAD_DIM)
    o_diag = jnp.stack([o6[:, g, :, :, g, :] for g in range(N_KV)], axis=1)
    o_tok = jnp.transpose(o_diag, (2, 0, 1, 3, 4)).reshape(tok, d_attn).astype(BF)
    y_s = _merge(xs2d, o_tok, sza_s, yc_s, rep(gate[batch:n_c]), woa, woc, fg, tok, tok)
    y_sample = jnp.transpose(y_s.reshape(n_dec, nbatch, d), (1, 0, 2))
    new = lambda n: kv6[:, n].reshape(1, nbatch, n_dec, N_KV, HEAD_DIM)
    cv_out = jnp.transpose(cv_s.reshape(CONV_W - 1, nbatch, d_conv), (1, 0, 2))[None]
    s_states = (new(0), new(1), new(2), new(3), heads(s_kwT), heads(s_vwT), cv_out)

    return (y_prompt, y_sample) + p_states + s_states
```

```python
import functools
import math

import numpy as np
import jax
import jax.numpy as jnp
from jax import lax
from jax.experimental import pallas as pl
from jax.experimental.pallas import tpu as pltpu

BF = jnp.bfloat16
F32 = jnp.float32
I32 = jnp.int32

HEAD_DIM = 64
N_KV = 2
HPG = 8
N_HEADS = N_KV * HPG
D_KV = N_KV * HEAD_DIM
BLK = 64
BLK_SHIFT = 6
TAIL_BLOCKS = 16
XPITCH = 136
XTAIL_PAGES = 8
N_SEL = 16
WINDOW = 512
CMP_HID = 2 * HEAD_DIM
CONV_W = 3
N_BUCKETS = 32
MAX_DIST = 128
EPS = 1e-6
NEG = -1e30
M_INIT = -5e29
VALID_THRESH = -1e29
LOG2E = math.log2(math.e)
QT = 128
KT = 128
FAR_GROUP = 4
ACC_ROWS = HEAD_DIM + 16
NCOL = N_HEADS * QT
GCOL = HPG * QT
LANES = 128
SAMPLE_KEY_TILE = 8192
V7X_VMEM_LIMIT = 56 * 1024 * 1024


def _dot(a, b):
    return jnp.dot(a, b, preferred_element_type=F32)


def _dot_nt(a, b):
    return lax.dot_general(a, b, (((1,), (1,)), ((), ())), preferred_element_type=F32)


def _sigmoid(x):
    return 1.0 / (1.0 + jnp.exp(-x))


def _silu(x):
    return x * _sigmoid(x)


def _ada_kernel(c_ref, w_ref, b_ref, o_ref):
    o_ref[...] = _dot(c_ref[...].astype(BF), w_ref[...].astype(BF)) + b_ref[...]


def _ada(c_all, ada_w, ada_b):
    rows, d = c_all.shape
    n = ada_w.shape[1]
    return pl.pallas_call(
        _ada_kernel,
        grid=(n // d,),
        in_specs=[pl.BlockSpec((rows, d), lambda j: (0, 0)),
                  pl.BlockSpec((d, d), lambda j: (0, j)),
                  pl.BlockSpec((1, d), lambda j: (0, j))],
        out_specs=pl.BlockSpec((rows, d), lambda j: (0, j)),
        out_shape=jax.ShapeDtypeStruct((rows, n), F32),
        name="ada_modulation",
    )(c_all, ada_w, ada_b.reshape(1, n))


def _proj_body(x_ref, sc_ref, sh_ref, ng_ref, wq_ref, wkv_ref, wg_ref, wc_ref, wz_ref, cw_ref):
    x = x_ref[...]
    ms = jnp.mean(x * x, axis=-1, keepdims=True)
    xn = x * lax.rsqrt(ms + EPS) * ng_ref[...]
    h = xn * sc_ref[0] + sh_ref[0]
    hb = h.astype(BF)
    q = _dot(hb, wq_ref[...]) * (HEAD_DIM ** -0.5 * LOG2E)
    kv = _dot(hb, wkv_ref[...])
    gates = _sigmoid(_dot(hb, wg_ref[...]))
    c3 = _dot(hb, wc_ref[...])
    dc = c3.shape[1] // 3
    hc, bc, cc = c3[:, :dc], c3[:, dc:2 * dc], c3[:, 2 * dc:]
    u = cc * hc
    z = _dot(hb, wz_ref[...])
    da = z.shape[1] - dc
    sza = _silu(z[:, :da])
    szc = _silu(z[:, da:])
    return q, kv, gates, u, bc, sza, szc


def _proj_prompt_kernel(x_ref, sc_ref, sh_ref, ng_ref, wq_ref, wkv_ref, wg_ref, wc_ref, wz_ref, cw_ref,
                        qT_ref, kcT_ref, vcT_ref, ksT_ref, vsT_ref, kwT_ref, vwT_ref, kc_ref, vc_ref,
                        kslc_ref, vslcT_ref, kwin_ref, vwinT_ref, gates_ref, yc_ref, sza_ref, cv_ref,
                        uext_ref, *, tiles_per_batch, tm):
    t = pl.program_id(0)
    q, kv, gates, u, bc, sza, szc = _proj_body(
        x_ref, sc_ref, sh_ref, ng_ref, wq_ref, wkv_ref, wg_ref, wc_ref, wz_ref, cw_ref)
    qT_ref[0] = q.T.astype(BF)
    kvT = [kv[:, n * D_KV:(n + 1) * D_KV].T for n in range(6)]
    for n, r in enumerate((kcT_ref, vcT_ref, ksT_ref, vsT_ref, kwT_ref, vwT_ref)):
        r[0] = kvT[n]
    kc_ref[...] = kv[:, 0:D_KV]
    vc_ref[...] = kv[:, D_KV:2 * D_KV]
    kslc_ref[0] = kv[:, 2 * D_KV:3 * D_KV].astype(BF)
    vslcT_ref[0] = kvT[3].astype(BF)
    kwin_ref[0] = kv[:, 4 * D_KV:5 * D_KV].astype(BF)
    vwinT_ref[0] = kvT[5].astype(BF)
    gates_ref[...] = gates

    first = (t % tiles_per_batch) == 0

    @pl.when(first)
    def _():
        uext_ref[0:8, :] = jnp.zeros((8, u.shape[1]), F32)

    @pl.when(jnp.logical_not(first))
    def _():
        uext_ref[0:8, :] = uext_ref[tm:tm + 8, :]

    uext_ref[8:8 + tm, :] = u
    y = (cw_ref[2:3, :] * u + cw_ref[1:2, :] * uext_ref[7:7 + tm, :]
         + cw_ref[0:1, :] * uext_ref[6:6 + tm, :])
    yc_ref[...] = (bc * y * szc).astype(BF)
    sza_ref[...] = sza.astype(BF)
    cv_ref[0] = u[tm - (CONV_W - 1):, :]


def _proj_sample_kernel(x_ref, sc_ref, sh_ref, ng_ref, wq_ref, wkv_ref, wg_ref, wc_ref, wz_ref, cw_ref,
                        st_ref, q_ref, kv_ref, gates_ref, yc_ref, sza_ref, cv_ref, uext_ref, *, nb, nt):
    q, kv, gates, u, bc, sza, szc = _proj_body(
        x_ref, sc_ref, sh_ref, ng_ref, wq_ref, wkv_ref, wg_ref, wc_ref, wz_ref, cw_ref)
    q_ref[...] = q
    kv_ref[...] = kv
    gates_ref[...] = gates
    ns = (CONV_W - 1) * nb
    uext_ref[0:ns, :] = st_ref[...]
    uext_ref[ns:ns + nt * nb, :] = u
    y = cw_ref[2:3, :] * u
    for k in range(CONV_W - 1):
        y = y + cw_ref[k:k + 1, :] * uext_ref[k * nb:k * nb + nt * nb, :]
    yc_ref[...] = (bc * y * szc).astype(BF)
    sza_ref[...] = sza.astype(BF)
    cv_ref[...] = uext_ref[nt * nb:nt * nb + ns, :]


def _const_spec(shape):
    nd = len(shape)
    return pl.BlockSpec(shape, lambda *_: (0,) * nd, pipeline_mode=pl.Buffered(1))


def _proj_prompt(x2d, sc, sh, ng, wts, cw, batch, seq, tm):
    tokens, d = x2d.shape
    tpb = seq // tm
    wq, wkv, wg, wc, wz = wts
    dc = wc.shape[1] // 3
    da = wz.shape[1] - dc
    tok_spec = lambda w: pl.BlockSpec((tm, w), lambda t: (t, 0))
    rowT_spec = lambda r: pl.BlockSpec((1, r, tm), lambda t: (t // tpb, 0, t % tpb))
    row_spec = lambda w: pl.BlockSpec((1, tm, w), lambda t: (t // tpb, t % tpb, 0))
    mod_spec = pl.BlockSpec((1, 1, d), lambda t: (t // tpb, 0, 0))
    out_shape = (
        [jax.ShapeDtypeStruct((batch, wq.shape[1], seq), BF)]
        + [jax.ShapeDtypeStruct((batch, D_KV, seq), F32)] * 6
        + [jax.ShapeDtypeStruct((tokens, D_KV), F32)] * 2
        + [jax.ShapeDtypeStruct((batch, seq, D_KV), BF), jax.ShapeDtypeStruct((batch, D_KV, seq), BF),
           jax.ShapeDtypeStruct((batch, seq, D_KV), BF), jax.ShapeDtypeStruct((batch, D_KV, seq), BF),
           jax.ShapeDtypeStruct((tokens, LANES), F32),
           jax.ShapeDtypeStruct((tokens, dc), BF), jax.ShapeDtypeStruct((tokens, da), BF),
           jax.ShapeDtypeStruct((batch, CONV_W - 1, dc), F32)])
    out_specs = (
        [rowT_spec(wq.shape[1])] + [rowT_spec(D_KV)] * 6 + [tok_spec(D_KV)] * 2
        + [row_spec(D_KV), rowT_spec(D_KV), row_spec(D_KV), rowT_spec(D_KV),
           tok_spec(LANES), tok_spec(dc), tok_spec(da),
           pl.BlockSpec((1, CONV_W - 1, dc), lambda t: (t // tpb, 0, 0))])
    return pl.pallas_call(
        functools.partial(_proj_prompt_kernel, tiles_per_batch=tpb, tm=tm),
        grid=(tokens // tm,),
        in_specs=[tok_spec(d), mod_spec, mod_spec, _const_spec((1, d)),
                  _const_spec(wq.shape), _const_spec(wkv.shape), _const_spec(wg.shape),
                  _const_spec(wc.shape), _const_spec(wz.shape), _const_spec(cw.shape)],
        out_specs=out_specs,
        out_shape=out_shape,
        scratch_shapes=[pltpu.VMEM((tm + 8, dc), F32)],
        compiler_params=pltpu.CompilerParams(dimension_semantics=("arbitrary",),
                                             vmem_limit_bytes=V7X_VMEM_LIMIT),
        name="proj_prompt",
    )(x2d, sc, sh, ng, wq, wkv, wg, wc, wz, cw)


def _proj_sample(x2d, sc, sh, ng, wts, cw, state2d, nb, nt):
    tokens, d = x2d.shape
    wq, wkv, wg, wc, wz = wts
    dc = wc.shape[1] // 3
    da = wz.shape[1] - dc
    ns = (CONV_W - 1) * nb
    full = lambda a: pl.BlockSpec(a.shape, lambda i: (0,) * a.ndim)
    ins = (x2d, sc, sh, ng, wq, wkv, wg, wc, wz, cw, state2d)
    out_shape = [jax.ShapeDtypeStruct((tokens, wq.shape[1]), F32),
                 jax.ShapeDtypeStruct((tokens, wkv.shape[1]), F32),
                 jax.ShapeDtypeStruct((tokens, LANES), F32),
                 jax.ShapeDtypeStruct((tokens, dc), BF), jax.ShapeDtypeStruct((tokens, da), BF),
                 jax.ShapeDtypeStruct((ns, dc), F32)]
    return pl.pallas_call(
        functools.partial(_proj_sample_kernel, nb=nb, nt=nt),
        grid=(1,),
        in_specs=[full(a) for a in ins],
        out_specs=[pl.BlockSpec(s.shape, lambda i: (0, 0)) for s in out_shape],
        out_shape=out_shape,
        scratch_shapes=[pltpu.VMEM((ns + tokens, dc), F32)],
        compiler_params=pltpu.CompilerParams(vmem_limit_bytes=V7X_VMEM_LIMIT),
        name="proj_sample",
    )(*ins)


def _compress(x_ref, nblk, pe_ref, w1_ref, w2_ref):
    def body(pp, acc):
        xp = jnp.concatenate([x_ref[pl.ds(2 * pp, nblk, stride=BLK), :],
                              x_ref[pl.ds(2 * pp + 1, nblk, stride=BLK), :]], axis=1) + pe_ref[pl.ds(pp, 1), :]
        return acc + _dot(xp.astype(BF), w1_ref[pp])
    acc = lax.fori_loop(0, BLK // 2, body, jnp.zeros((nblk, N_KV * CMP_HID), F32), unroll=2)
    return _dot(_silu(acc).astype(BF), w2_ref[...])


def _compress_prompt_kernel(kc_ref, vc_ref, pek_ref, w1k_ref, w2k_ref, pev_ref, w1v_ref, w2v_ref,
                            kcmp_ref, vcmpT_ref, *, nblk):
    kcmp_ref[0] = _compress(kc_ref.at[0], nblk, pek_ref, w1k_ref, w2k_ref).astype(BF)
    vcmpT_ref[0] = _compress(vc_ref.at[0], nblk, pev_ref, w1v_ref, w2v_ref).T.astype(BF)


def _compress_prompt(kc, vc, cw, batch, seq):
    nblk = seq // BLK
    row = pl.BlockSpec((1, seq, D_KV), lambda b: (b, 0, 0))
    return pl.pallas_call(
        functools.partial(_compress_prompt_kernel, nblk=nblk),
        grid=(batch,),
        in_specs=[row, row] + [_const_spec(a.shape) for a in cw],
        out_specs=[pl.BlockSpec((1, nblk, D_KV), lambda b: (b, 0, 0)),
                   pl.BlockSpec((1, D_KV, nblk), lambda b: (b, 0, 0))],
        out_shape=[jax.ShapeDtypeStruct((batch, nblk, D_KV), BF),
                   jax.ShapeDtypeStruct((batch, D_KV, nblk), BF)],
        compiler_params=pltpu.CompilerParams(vmem_limit_bytes=V7X_VMEM_LIMIT),
        name="compress_prompt",
    )(kc, vc, *cw)


SCORE_REMOVED = -3.0


def _select_topk(score, axis, k):
    idx = lax.broadcasted_iota(I32, score.shape, axis)
    past_end = jnp.int32(score.shape[axis])
    picked = jnp.zeros(score.shape, F32)
    s = score
    for _ in range(k):
        m = jnp.max(s, axis=axis, keepdims=True)
        first = jnp.min(jnp.where(s == m, idx, past_end), axis=axis, keepdims=True)
        hit = idx == first
        picked = jnp.where(hit, 1.0, picked)
        s = jnp.where(hit, SCORE_REMOVED, s)
    return picked > 0.5


def _attend(br, lhs, rhs_ref, krows, bias, vT, m_ref, acc_ref):
    s = _dot(lhs, rhs_ref[0:krows, :])
    if bias is not None:
        s = s + bias
    _consume(br, s, vT, m_ref, acc_ref)


SUB = 8


def _col_max(s):
    part = jnp.max(s.reshape(s.shape[0] // SUB, SUB, s.shape[1]), axis=0)
    return jnp.broadcast_to(jnp.max(part, axis=0, keepdims=True), part.shape)


def _rows_op(op, x, r):
    return op(x.reshape(x.shape[0] // SUB, SUB, x.shape[1]), r[None]).reshape(x.shape)


def _pv_update(br, pb, vT, scale_old, scale_new, acc_ref):
    ones = jnp.ones((ACC_ROWS - HEAD_DIM, pb.shape[0]), BF)
    for g in range(N_KV):
        gs = slice(g * GCOL, (g + 1) * GCOL)
        v_aug = jnp.concatenate([vT[g * HEAD_DIM:(g + 1) * HEAD_DIM, :], ones], axis=0)
        acc = acc_ref[br, g]
        if scale_old is not None:
            acc = _rows_op(jnp.multiply, acc, scale_old[:, gs])
        acc = acc + _dot(v_aug, pb[:, gs])
        if scale_new is not None:
            acc = _rows_op(jnp.multiply, acc, scale_new[:, gs])
        acc_ref[br, g] = acc


def _consume(br, s, vT, m_ref, acc_ref):
    m_old = m_ref[br]
    m_new = jnp.maximum(m_old, _col_max(s))
    alpha = jnp.exp2(m_old - m_new)
    pb = jnp.exp2(_rows_op(jnp.subtract, s, m_new)).astype(BF)
    m_ref[br] = m_new
    _pv_update(br, pb, vT, alpha, None, acc_ref)


def _nsa_prompt_kernel(qT_ref, gates_ref, kslc_ref, vslcT_ref, kwin_ref, vwinT_ref, kcmp_ref, vcmpT_ref,
                       cb_ref, sb_ref, w0_ref, o_ref,
                       rhs_ref, m_ref, acc_ref, s0_ref, s1_ref, *, nb):
    i = pl.program_id(1)

    for gh in range(N_HEADS):
        g = gh // HPG
        cols = slice(gh * QT, (gh + 1) * QT)
        rhs_ref[g * HEAD_DIM:(g + 1) * HEAD_DIM, cols] = qT_ref[0, gh * HEAD_DIM:(gh + 1) * HEAD_DIM, :]
        og = 1 - g
        rhs_ref[og * HEAD_DIM:(og + 1) * HEAD_DIM, cols] = jnp.zeros((HEAD_DIM, QT), BF)

    lg = _dot(kcmp_ref[0], rhs_ref[0:D_KV, :])
    rel = 2 * i - lax.broadcasted_iota(I32, (nb, NCOL), 0)
    bias = jnp.where(rel >= 3, 0.0, NEG)
    for rr in range(4):
        bias = jnp.where(rel == rr - 1, cb_ref[rr:rr + 1, :], bias)
    lgb = lg + bias
    mc = jnp.max(lgb, axis=0, keepdims=True)
    pc = jnp.exp2(lgb - mc)
    lc = jnp.sum(pc, axis=0, keepdims=True)
    pn = pc * jnp.where(mc > VALID_THRESH, 1.0 / lc, 0.0)
    pnb = pn.astype(BF)
    for g in range(N_KV):
        acc_ref[0, g, 0:HEAD_DIM, :] = _dot(vcmpT_ref[0, g * HEAD_DIM:(g + 1) * HEAD_DIM, :],
                                            pnb[:, g * GCOL:(g + 1) * GCOL])

    imp, valid = [], []
    for g in range(N_KV):
        a = pn[:, g * GCOL:g * GCOL + QT]
        for h in range(1, HPG):
            a = a + pn[:, g * GCOL + h * QT:g * GCOL + (h + 1) * QT]
        imp.append(a)
        valid.append(lgb[:, g * GCOL:g * GCOL + QT] > VALID_THRESH)
    imp = jnp.concatenate(imp, axis=1)
    valid = jnp.concatenate(valid, axis=1)
    jrow = lax.broadcasted_iota(I32, (nb, N_KV * QT), 0)
    tpos = i * QT + (lax.broadcasted_iota(I32, (nb, N_KV * QT), 1) & (QT - 1))
    score = jnp.where(valid, imp, jnp.where(jrow * BLK <= tpos, 2.0, -1.0))
    sel = _select_topk(score, 0, min(N_SEL, nb))
    selbias = jnp.where(jnp.logical_and(sel, score >= 0.0), 0.0, NEG).astype(BF)
    for gh in range(N_HEADS):
        g = gh // HPG
        rhs_ref[D_KV:D_KV + nb, gh * QT:(gh + 1) * QT] = selbias[:, g * QT:(g + 1) * QT]

    for br in (1, 2):
        m_ref[br] = jnp.full((SUB, NCOL), M_INIT, F32)
        acc_ref[br] = jnp.zeros((N_KV, ACC_ROWS, GCOL), F32)

    def sel_keys(kt, tk):
        k0 = pl.multiple_of(kt * KT, KT)
        k = kslc_ref[0, pl.ds(k0, tk), :]
        jj = lax.broadcasted_iota(I32, (tk, nb), 1)
        kk = lax.broadcasted_iota(I32, (tk, nb), 0)
        onehot = jnp.where(jj == (KT // BLK) * kt + jnp.right_shift(kk, BLK_SHIFT), 1.0, 0.0).astype(BF)
        return jnp.concatenate([k, onehot], axis=1), vslcT_ref[0, :, pl.ds(k0, tk)]

    def sel_tile(kt, tk, bias_tile):
        ka, vT = sel_keys(kt, tk)
        _attend(1, ka, rhs_ref, D_KV + nb, bias_tile, vT, m_ref, acc_ref)

    n_far = jnp.maximum(i - 1, 0)
    n_big = n_far // FAR_GROUP
    n_pairs = n_big // 2
    big = FAR_GROUP * KT

    def far_scores(j):
        ka, _ = sel_keys(j * FAR_GROUP, big)
        return _dot(ka, rhs_ref[...])

    def far_step(cur_ref, nxt_ref, j_cur, j_nxt):
        m_cur = m_ref[1]
        pb = jnp.exp2(_rows_op(jnp.subtract, cur_ref[...], m_cur)).astype(BF)
        s_nxt = far_scores(j_nxt)
        nxt_ref[...] = s_nxt
        m_nxt = jnp.maximum(m_cur, _col_max(s_nxt))
        vT = vslcT_ref[0, :, pl.ds(pl.multiple_of(j_cur * big, big), big)]
        _pv_update(1, pb, vT, None, jnp.exp2(m_cur - m_nxt), acc_ref)
        m_ref[1] = m_nxt

    @pl.when(n_pairs > 0)
    def _():
        s = far_scores(0)
        s0_ref[...] = s
        m_ref[1] = jnp.maximum(m_ref[1], _col_max(s))

    def far_pair(jp, c):
        @pl.when(jp < n_pairs)
        def _():
            far_step(s0_ref, s1_ref, 2 * jp, 2 * jp + 1)

        @pl.when(2 * jp + 1 < n_big)
        def _():
            far_step(s1_ref, s0_ref, 2 * jp + 1, jnp.minimum(2 * jp + 2, 2 * n_pairs - 1))
        return c
    lax.fori_loop(0, n_pairs, far_pair, 0)

    @pl.when(n_big > 2 * n_pairs)
    def _():
        sel_tile((n_big - 1) * FAR_GROUP, big, None)

    def far_small(kt, c):
        sel_tile(kt, KT, None)
        return c
    lax.fori_loop(n_big * FAR_GROUP, n_far, far_small, 0)

    n_win = WINDOW // KT + 1

    @pl.when(i >= n_win - 1)
    def _():
        ka, vT_band = sel_keys(i - 1, 2 * KT)
        s_band = _dot(ka, rhs_ref[...]) + sb_ref[...]
        k0 = pl.multiple_of((i - (n_win - 1)) * KT, KT)
        s_win = _dot(kwin_ref[0, pl.ds(k0, n_win * KT), :], rhs_ref[0:D_KV, :]) + jnp.concatenate(
            [w0_ref[...], jnp.zeros(((n_win - 3) * KT, NCOL), F32), sb_ref[...]], axis=0)
        _consume(1, s_band, vT_band, m_ref, acc_ref)
        _consume(2, s_win, vwinT_ref[0, :, pl.ds(k0, n_win * KT)], m_ref, acc_ref)

    @pl.when(jnp.logical_and(i >= 1, i < n_win - 1))
    def _():
        sel_tile(i - 1, 2 * KT, sb_ref[...])

    @pl.when(i == 0)
    def _():
        sel_tile(0, KT, sb_ref[KT:2 * KT, :])

    @pl.when(i < n_win - 1)
    def _():
        for r in range(1, n_win):
            kt = i - (n_win - 1) + r

            @pl.when(kt >= 0)
            def _():
                k0 = pl.multiple_of(kt * KT, KT)
                r0 = (r - (n_win - 2)) * KT
                bias = sb_ref[r0:r0 + KT, :] if r >= n_win - 2 else None
                _attend(2, kwin_ref[0, pl.ds(k0, KT), :], rhs_ref, D_KV, bias,
                        vwinT_ref[0, :, pl.ds(k0, KT)], m_ref, acc_ref)

    gts = gates_ref[0, 0]
    for g in range(N_KV):
        cs = slice(g * GCOL, (g + 1) * GCOL)
        o = gts[0:1, cs] * acc_ref[0, g, 0:HEAD_DIM, :]
        for br in (1, 2):
            l = acc_ref[br, g, HEAD_DIM:HEAD_DIM + 1, :]
            o = o + (gts[br:br + 1, cs] * jnp.where(l > 0.0, 1.0 / l, 0.0)) * acc_ref[br, g, 0:HEAD_DIM, :]
        for hp in range(HPG // 2):
            pair = jnp.concatenate([o[:, (2 * hp) * QT:(2 * hp + 1) * QT],
                                    o[:, (2 * hp + 1) * QT:(2 * hp + 2) * QT]], axis=0)
            c0 = g * HPG * HEAD_DIM + hp * 2 * HEAD_DIM
            o_ref[0, :, c0:c0 + 2 * HEAD_DIM] = pair.T.astype(BF)


def _nsa_prompt(qT, gates, kslc, vslcT, kwin, vwinT, kcmp, vcmpT, tables, batch, seq):
    nb = seq // BLK
    nq = seq // QT
    cb, sb, w0 = tables
    dq = qT.shape[1]
    per_b = lambda shape: pl.BlockSpec((1,) + shape, lambda b, i: (b, 0, 0))
    return pl.pallas_call(
        functools.partial(_nsa_prompt_kernel, nb=nb),
        grid=(batch, nq),
        in_specs=[pl.BlockSpec((1, dq, QT), lambda b, i: (b, 0, i)),
                  pl.BlockSpec((1, 1, 8, NCOL), lambda b, i: (b, i, 0, 0)),
                  per_b((seq, D_KV)), per_b((D_KV, seq)), per_b((seq, D_KV)), per_b((D_KV, seq)),
                  per_b((nb, D_KV)), per_b((D_KV, nb)),
                  _const_spec(cb.shape), _const_spec(sb.shape), _const_spec(w0.shape)],
        out_specs=pl.BlockSpec((1, QT, dq), lambda b, i: (b, i, 0)),
        out_shape=jax.ShapeDtypeStruct((batch, seq, dq), BF),
        scratch_shapes=[pltpu.VMEM((D_KV + nb, NCOL), BF),
                        pltpu.VMEM((3, SUB, NCOL), F32),
                        pltpu.VMEM((3, N_KV, ACC_ROWS, GCOL), F32),
                        pltpu.VMEM((FAR_GROUP * KT, NCOL), F32),
                        pltpu.VMEM((FAR_GROUP * KT, NCOL), F32)],
        compiler_params=pltpu.CompilerParams(dimension_semantics=("arbitrary", "arbitrary"),
                                             vmem_limit_bytes=V7X_VMEM_LIMIT),
        name="nsa_prompt",
    )(qT, gates, kslc, vslcT, kwin, vwinT, kcmp, vcmpT, cb, sb, w0)


def _merge_kernel(x_ref, o_ref, sza_ref, yc_ref, gate_ref, woa_ref, woc_ref, fg_ref, y_ref):
    og = (o_ref[...].astype(F32) * sza_ref[...].astype(F32)).astype(BF)
    acc = _dot(og, woa_ref[...]) + _dot(yc_ref[...], woc_ref[...])
    xo = x_ref[...] + gate_ref[0] * acc
    ms = jnp.mean(xo * xo, axis=-1, keepdims=True)
    y_ref[...] = xo * lax.rsqrt(ms + EPS) * fg_ref[...]


def _merge(x2d, o2d, sza, yc, gate, woa, woc, fg, tm, rows_per_gate):
    tokens, d = x2d.shape
    tok = lambda w: pl.BlockSpec((tm, w), lambda t: (t, 0))
    gr = gate.shape[1]
    return pl.pallas_call(
        _merge_kernel,
        grid=(tokens // tm,),
        in_specs=[tok(d), tok(o2d.shape[1]), tok(sza.shape[1]), tok(yc.shape[1]),
                  pl.BlockSpec((1, gr, d), lambda t: ((t * tm) // rows_per_gate, 0, 0)),
                  _const_spec(woa.shape), _const_spec(woc.shape), _const_spec(fg.shape)],
        out_specs=tok(d),
        out_shape=jax.ShapeDtypeStruct((tokens, d), F32),
        compiler_params=pltpu.CompilerParams(dimension_semantics=("arbitrary",),
                                             vmem_limit_bytes=V7X_VMEM_LIMIT),
        name="merge_out",
    )(x2d, o2d, sza, yc, gate, woa, woc, fg)


def _softmax_rows(parts):
    m = parts[0].max(axis=1, keepdims=True)
    for s in parts[1:]:
        m = jnp.maximum(m, s.max(axis=1, keepdims=True))
    ps = [jnp.exp2(s - m) for s in parts]
    l = ps[0].sum(axis=1, keepdims=True)
    for p in ps[1:]:
        l = l + p.sum(axis=1, keepdims=True)
    inv = jnp.where(jnp.logical_and(m > VALID_THRESH, l > 0.0), 1.0 / l, 0.0)
    return ps, inv


def _nsa_sample_kernel(pt_ref, ckc_ref, cvc_ref, cks_ref, cvs_ref,
                       qbd_ref, gates_ref, tails_ref, tailsT_ref, bkwT_ref, bvwT_ref,
                       cbs_ref, sbl_ref, sbt_ref, wbl_ref, wbt_ref, elocT_ref,
                       pek_ref, w1k_ref, w2k_ref, pev_ref, w1v_ref, w2v_ref,
                       o_ref, skwT_ref, svwT_ref,
                       buf_ref, sem, xbuf_ref, cmpf_ref, kcmp_ref, vcmp_ref, sbuf_ref, pbuf_ref,
                       *, n_pages, page, n_dec):
    b = pl.program_id(0)
    nbatch = pl.num_programs(0)
    past = n_pages * page
    nbp = past // BLK
    nbc = kcmp_ref.shape[0]
    n_tiles = past // SAMPLE_KEY_TILE
    nrow = qbd_ref.shape[1]
    caches = (ckc_ref, cvc_ref, cks_ref, cvs_ref)

    def page_copy(cache_ref, pg, j, slot):
        return pltpu.make_async_copy(cache_ref.at[pg],
                                     buf_ref.at[slot, :, pl.ds(pl.multiple_of(j * page, page), page)],
                                     sem.at[slot])

    def start_fetch(cache_ref, bb, slot):
        def body(j, c):
            page_copy(cache_ref, pt_ref[bb, j], j, slot).start()
            return c
        lax.fori_loop(0, n_pages, body, 0, unroll=4)

    def wait_fetch(cache_ref, slot):
        def body(j, c):
            page_copy(cache_ref, 0, j, slot).wait()
            return c
        lax.fori_loop(0, n_pages, body, 0)

    def stage(c):
        if c + 1 < len(caches):
            start_fetch(caches[c + 1], b, (c + 1) % 2)
        else:
            @pl.when(b + 1 < nbatch)
            def _():
                start_fetch(caches[0], b + 1, 0)
        wait_fetch(caches[c], c % 2)
        return buf_ref.at[c % 2]

    @pl.when(b == 0)
    def _():
        start_fetch(caches[0], 0, 0)

    qbd = qbd_ref[0]

    def compress_cache(c, slot_ref, pe_ref, w1_ref, w2_ref, dst_ref):
        def xpose(j, carry):
            src = slot_ref[:, pl.ds(pl.multiple_of(j * page, page), page)]
            xbuf_ref[pl.ds(pl.multiple_of(j * XPITCH, 8), page), :] = src.T
            return carry
        lax.fori_loop(0, n_pages, xpose, 0, unroll=8)
        n_extra = XTAIL_PAGES * XPITCH
        xbuf_ref[pl.ds(n_pages * XPITCH, n_extra), :] = jnp.zeros((n_extra, D_KV), F32)
        xbuf_ref[pl.ds(n_pages * XPITCH, 8), :] = tails_ref[0, c]
        n_even = n_pages + XTAIL_PAGES

        def rows(r0, n):
            return jnp.concatenate([xbuf_ref[pl.ds(r0, n, stride=XPITCH), :],
                                    xbuf_ref[pl.ds(r0 + 1, n, stride=XPITCH), :]], axis=1)

        def body(pp, acc):
            acc_e, acc_o = acc
            pe = pe_ref[pl.ds(pp, 1), :]
            w = w1_ref[pp]
            xe = rows(2 * pp, n_even) + pe
            xo = rows(2 * pp + BLK, n_pages) + pe
            return acc_e + _dot(xe.astype(BF), w), acc_o + _dot(xo.astype(BF), w)
        acc_e, acc_o = lax.fori_loop(
            0, BLK // 2, body,
            (jnp.zeros((n_even, N_KV * CMP_HID), F32), jnp.zeros((n_pages, N_KV * CMP_HID), F32)), unroll=2)
        out_e = _dot(_silu(acc_e).astype(BF), w2_ref[...])
        out_o = _dot(_silu(acc_o).astype(BF), w2_ref[...])
        cmpf_ref[pl.ds(0, n_pages, stride=2), :] = out_e[0:n_pages]
        cmpf_ref[pl.ds(1, n_pages, stride=2), :] = out_o
        cmpf_ref[nbp:nbp + XTAIL_PAGES, :] = out_e[n_pages:n_even]
        cmpf_ref[nbp + XTAIL_PAGES:nbp + TAIL_BLOCKS, :] = jnp.zeros((TAIL_BLOCKS - XTAIL_PAGES, D_KV), F32)
        dst_ref[0:nbp + TAIL_BLOCKS, :] = cmpf_ref[...].astype(BF)
        dst_ref[nbp + TAIL_BLOCKS:nbc, :] = jnp.zeros((nbc - nbp - TAIL_BLOCKS, D_KV), BF)

    compress_cache(0, stage(0), pek_ref, w1k_ref, w2k_ref, kcmp_ref)
    compress_cache(1, stage(1), pev_ref, w1v_ref, w2v_ref, vcmp_ref)

    lgc = _dot_nt(qbd, kcmp_ref[...]) + cbs_ref[...]
    (pc,), invc = _softmax_rows([lgc])
    pn = pc * invc
    o_cmp = _dot(pn.astype(BF), vcmp_ref[...])
    imp = jnp.sum(pn.reshape(nrow // HPG, HPG, nbc), axis=1)
    jcol = lax.broadcasted_iota(I32, imp.shape, 1)
    score = jnp.where(jcol < nbp, imp, jnp.where(jcol == nbp, 2.0, -1.0))
    sel = _select_topk(score, 1, N_SEL)
    selbias = jnp.where(jnp.logical_and(sel, score >= 0.0), 0.0, NEG)
    selb = jnp.broadcast_to(selbias[:, None, :], (nrow // HPG, HPG, nbc)).reshape(nrow, nbc).astype(BF)

    kslot = stage(2)
    tile = lambda t: slice(t * SAMPLE_KEY_TILE, (t + 1) * SAMPLE_KEY_TILE)
    for t in range(n_tiles):
        sbuf_ref[t] = (_dot(qbd, kslot[:, tile(t)].astype(BF))
                       + _dot(selb[:, t * LANES:(t + 1) * LANES], elocT_ref[...]))
    last = slice(SAMPLE_KEY_TILE - LANES, SAMPLE_KEY_TILE)
    sbuf_ref[n_tiles - 1, :, last] = sbuf_ref[n_tiles - 1, :, last] + sbl_ref[...]
    s_tail = (_dot(qbd, tailsT_ref[0, 0].astype(BF))
              + _dot(selb[:, nbp:nbp + LANES], elocT_ref[:, 0:LANES]) + sbt_ref[...])
    ps, inv_sel = _softmax_rows([sbuf_ref[t] for t in range(n_tiles)] + [s_tail])
    for t in range(n_tiles):
        pbuf_ref[t] = ps[t].astype(BF)
    p_tail = ps[n_tiles].astype(BF)

    vslot = stage(3)
    o_sel = _dot_nt(p_tail, tailsT_ref[0, 1].astype(BF))
    for t in range(n_tiles):
        o_sel = o_sel + _dot_nt(pbuf_ref[t], vslot[:, tile(t)].astype(BF))
    o_sel = o_sel * inv_sel

    s_w = _dot(qbd, bkwT_ref[0].astype(BF)) + wbl_ref[...]
    s_wt = _dot(qbd, tailsT_ref[0, 2].astype(BF)) + wbt_ref[...]
    (p_w, p_wt), inv_w = _softmax_rows([s_w, s_wt])
    o_win = (_dot_nt(p_w.astype(BF), bvwT_ref[0].astype(BF))
             + _dot_nt(p_wt.astype(BF), tailsT_ref[0, 3].astype(BF))) * inv_w

    o_ref[0] = gates_ref[0, 0] * o_cmp + gates_ref[0, 1] * o_sel + gates_ref[0, 2] * o_win

    nbuf = bkwT_ref.shape[2]
    lane = lax.broadcasted_iota(I32, (D_KV, LANES), 1)
    for n, src, dst in ((2, bkwT_ref, skwT_ref), (3, bvwT_ref, svwT_ref)):
        rolled = pltpu.roll(src[0], nbuf - n_dec, axis=1)
        new = pltpu.roll(tailsT_ref[0, n], LANES - n_dec, axis=1)
        dst[0, :, 0:nbuf - LANES] = rolled[:, 0:nbuf - LANES]
        dst[0, :, nbuf - LANES:nbuf] = jnp.where(lane >= LANES - n_dec, new, rolled[:, nbuf - LANES:nbuf])


def _nsa_sample(page_table, caches, qbd, gates3, tails, tailsT, bkwT, bvwT, tables, cw, n_dec):
    nbatch, n_pages = page_table.shape
    page = caches[0].shape[2]
    past = n_pages * page
    cbs, sbl, sbt, wbl, wbt, elocT = tables
    nbc = cbs.shape[1]
    nrow = qbd.shape[1]
    nbuf = bkwT.shape[2]
    nbp = past // BLK
    n_tiles = past // SAMPLE_KEY_TILE
    assert past % SAMPLE_KEY_TILE == 0 and nbp % LANES == 0 and page == 2 * BLK and nbuf % LANES == 0
    per_b = lambda a: pl.BlockSpec((1,) + a.shape[1:], lambda b, pt: (b,) + (0,) * (a.ndim - 1))
    const = lambda a: pl.BlockSpec(a.shape, lambda b, pt: (0,) * a.ndim, pipeline_mode=pl.Buffered(1))
    hbm = pl.BlockSpec(memory_space=pl.ANY)
    grid_spec = pltpu.PrefetchScalarGridSpec(
        num_scalar_prefetch=1,
        grid=(nbatch,),
        in_specs=[hbm] * 4 + [per_b(a) for a in (qbd, gates3, tails, tailsT, bkwT, bvwT)]
                 + [const(a) for a in tables] + [const(a) for a in cw],
        out_specs=[pl.BlockSpec((1, nrow, D_KV), lambda b, pt: (b, 0, 0)),
                   pl.BlockSpec((1, D_KV, nbuf), lambda b, pt: (b, 0, 0)),
                   pl.BlockSpec((1, D_KV, nbuf), lambda b, pt: (b, 0, 0))],
        scratch_shapes=[pltpu.VMEM((2, D_KV, past), F32),
                        pltpu.SemaphoreType.DMA((2,)),
                        pltpu.VMEM(((n_pages + XTAIL_PAGES) * XPITCH, D_KV), F32),
                        pltpu.VMEM((nbp + TAIL_BLOCKS, D_KV), F32),
                        pltpu.VMEM((nbc, D_KV), BF),
                        pltpu.VMEM((nbc, D_KV), BF),
                        pltpu.VMEM((n_tiles, nrow, SAMPLE_KEY_TILE), F32),
                        pltpu.VMEM((n_tiles, nrow, SAMPLE_KEY_TILE), BF)])
    return pl.pallas_call(
        functools.partial(_nsa_sample_kernel, n_pages=n_pages, page=page, n_dec=n_dec),
        grid_spec=grid_spec,
        out_shape=[jax.ShapeDtypeStruct((nbatch, nrow, D_KV), F32),
                   jax.ShapeDtypeStruct((nbatch, D_KV, nbuf), F32),
                   jax.ShapeDtypeStruct((nbatch, D_KV, nbuf), F32)],
        compiler_params=pltpu.CompilerParams(dimension_semantics=("arbitrary",),
                                             vmem_limit_bytes=V7X_VMEM_LIMIT),
        name="nsa_sample",
    )(page_table, *caches, qbd, gates3, tails, tailsT, bkwT, bvwT, *tables, *cw)


def _t5_bucket_np(n):
    d = np.maximum(np.arange(n), 0)
    max_exact = N_BUCKETS // 2
    nf = np.maximum(d, max_exact).astype(np.float32)
    large = max_exact + (np.log(nf / np.float32(max_exact)) / np.float32(math.log(MAX_DIST / max_exact))
                         * np.float32(N_BUCKETS - max_exact)).astype(np.int32)
    return np.where(d < max_exact, d, np.minimum(large, N_BUCKETS - 1))


N_NEAR = 256
_BUCKETS = _t5_bucket_np(N_NEAR)
assert np.all(_BUCKETS[QT + 1:] == N_BUCKETS - 1)


def _shifted_bias(rel_bias):
    return (rel_bias[_BUCKETS[:N_NEAR]] - rel_bias[N_BUCKETS - 1][None, :]) * LOG2E


def _bias_cols(tp, dist):
    val = tp[np.clip(dist, 0, N_NEAR - 1)]
    val = jnp.where((dist >= 0)[:, :, None], val, NEG)
    return jnp.transpose(val, (0, 2, 1)).reshape(dist.shape[0], -1)


def _prompt_tables(rel_bias, nb):
    tp = _shifted_bias(rel_bias)
    q = np.arange(QT)[None, :]
    r = np.arange(-1, 7)[:, None]
    d_cmp = np.where(r <= 2, BLK * r - (BLK - 1) + q, -1)
    cb = _bias_cols(tp, d_cmp)
    n_neg, rows = 2 * KT, 2 * KT
    w = jnp.concatenate([jnp.full((N_HEADS, n_neg), NEG, F32), tp.T, jnp.zeros((N_HEADS, 1), F32)], axis=1)
    period = w.shape[1]
    skew = jnp.tile(w, (1, rows))[:, :rows * (period - 1)].reshape(N_HEADS, rows, period - 1)
    c0 = n_neg + KT
    sb = jnp.transpose(skew[:, :, c0:c0 + QT], (1, 0, 2)).reshape(rows, N_HEADS * QT)
    kk = np.arange(KT)[:, None]
    w0 = jnp.tile(jnp.asarray(np.where(kk >= q, 0.0, NEG), F32), (1, N_HEADS))
    return cb, sb, w0


def _sample_rows_bias(tp, dist):
    val = tp[np.clip(dist, 0, N_NEAR - 1)]
    val = jnp.where((dist >= 0)[:, :, None], val, NEG)
    val = val.reshape(dist.shape[0], dist.shape[1], N_KV, HPG)
    return jnp.transpose(val, (2, 0, 3, 1)).reshape(-1, dist.shape[1])


def _sample_tables(rel_bias, past, n_dec, nbuf, nbc):
    tp = _shifted_bias(rel_bias)
    q = np.arange(n_dec)[:, None]
    nbp = past // BLK
    j = np.arange(nbc)[None, :]
    cbs = _sample_rows_bias(tp, np.where(j < nbp, past + q - (BLK * j + BLK - 1), -1))
    kk = np.arange(LANES)[None, :]
    sbl = _sample_rows_bias(tp, q + LANES - kk)
    sbt = _sample_rows_bias(tp, q - kk)
    r = np.arange(nbuf)[None, :]
    d_w = nbuf + q - r
    wbl = _sample_rows_bias(tp, np.where(d_w <= WINDOW, d_w, -1))
    wbt = sbt
    kk = np.arange(SAMPLE_KEY_TILE)[None, :]
    elocT = jnp.asarray(kk // BLK == np.arange(LANES)[:, None], BF)
    return cbs, sbl, sbt, wbl, wbt, elocT


def _compress_weights(pe, w1, w2):
    pe2 = jnp.tile(pe, (1, N_KV)).reshape(BLK // 2, 2 * D_KV)
    w1r = w1.reshape(BLK, HEAD_DIM, CMP_HID)
    eye = jnp.eye(N_KV, dtype=w1.dtype)
    w1bd = jnp.einsum("pdh,gk->pgdkh", w1r, eye).reshape(BLK // 2, 2 * D_KV, N_KV * CMP_HID).astype(BF)
    w2bd = jnp.einsum("hd,gk->ghkd", w2, eye).reshape(N_KV * CMP_HID, D_KV).astype(BF)
    return pe2, w1bd, w2bd


def _split_w_in(w_in, d_attn, d_conv, d_mix):
    sizes = (d_attn,) + (D_KV,) * 6 + (3 * N_HEADS,) + (d_conv,) * 3 + (d_mix,)
    offs = np.cumsum((0,) + sizes)
    wq = w_in[:, offs[0]:offs[1]]
    wkv = w_in[:, offs[1]:offs[7]]
    wg = jnp.pad(w_in[:, offs[7]:offs[8]], ((0, 0), (0, LANES - 3 * N_HEADS)))
    wc = w_in[:, offs[8]:offs[11]]
    wz = w_in[:, offs[11]:offs[12]]
    return tuple(w.astype(BF) for w in (wq, wkv, wg, wc, wz))


def kernel(x_prompt, x_sample, cache_k_cmp, cache_v_cmp, cache_k_slc, cache_v_slc, cache_k_win,
           cache_v_win, state_conv, page_table, c_prompt, c_sample, ada_w, ada_b, norm_g, w_in,
           cmp_pe_k, cmp_w1_k, cmp_w2_k, cmp_pe_v, cmp_w1_v, cmp_w2_v, conv_w, w_out, rel_bias, final_g):
    depth = ada_w.shape[0]
    assert depth == 1
    batch, seq, d = x_prompt.shape
    nbatch, n_dec, _ = x_sample.shape
    d_attn = N_HEADS * HEAD_DIM
    d_conv = conv_w.shape[2]
    d_mix = d_attn + d_conv
    n_pages = page_table.shape[1]
    page = cache_k_cmp.shape[2]
    past = n_pages * page
    nbuf = cache_k_win.shape[2]
    tm = 512
    assert seq % tm == 0 and seq >= WINDOW and nbuf == WINDOW

    wts = _split_w_in(w_in[0], d_attn, d_conv, d_mix)
    cwk = _compress_weights(cmp_pe_k[0], cmp_w1_k[0], cmp_w2_k[0])
    cwv = _compress_weights(cmp_pe_v[0], cmp_w1_v[0], cmp_w2_v[0])
    woa = w_out[0, :d_attn].astype(BF)
    woc = w_out[0, d_attn:].astype(BF)
    ng = norm_g[0].reshape(1, d)
    fg = final_g.reshape(1, d)
    cw = conv_w[0]

    n_c = batch + nbatch
    c_all = jnp.pad(jnp.concatenate([c_prompt, c_sample], axis=0), ((0, (-n_c) % 8), (0, 0)))
    mod = _ada(c_all, ada_w[0], ada_b[0])
    shift, scale, gate = mod[:, :d], mod[:, d:2 * d], mod[:, 2 * d:]

    x2d = x_prompt.reshape(batch * seq, d)
    sc_p = (1.0 + scale[:batch]).reshape(batch, 1, d)
    sh_p = shift[:batch].reshape(batch, 1, d)
    (qT, kcT, vcT, ksT, vsT, kwT, vwT, kc, vc, kslc, vslcT, kwin, vwinT, gates_tok, yc, sza, cv_p) = _proj_prompt(
        x2d, sc_p, sh_p, ng, wts, cw, batch, seq, tm)
    kcmp, vcmpT = _compress_prompt(kc.reshape(batch, seq, D_KV), vc.reshape(batch, seq, D_KV),
                                   cwk + cwv, batch, seq)
    nq = seq // QT
    gates = gates_tok[:, :3 * N_HEADS].reshape(batch, nq, QT, 3, N_HEADS)
    gates = jnp.transpose(gates, (0, 1, 3, 4, 2)).reshape(batch, nq, 3, NCOL)
    gates = jnp.pad(gates, ((0, 0), (0, 0), (0, 5), (0, 0)))
    o_p = _nsa_prompt(qT, gates, kslc, vslcT, kwin, vwinT, kcmp, vcmpT,
                      _prompt_tables(rel_bias, seq // BLK), batch, seq)
    y_p = _merge(x2d, o_p.reshape(batch * seq, d_attn), sza, yc, gate[:batch].reshape(batch, 1, d),
                 woa, woc, fg, tm, seq)
    y_prompt = y_p.reshape(batch, seq, d)
    heads = lambda aT: jnp.transpose(aT.reshape(aT.shape[0], N_KV, HEAD_DIM, aT.shape[2]), (0, 3, 1, 2))[None]
    p_states = (heads(kcT), heads(vcT), heads(ksT), heads(vsT),
                heads(kwT[:, :, seq - WINDOW:]), heads(vwT[:, :, seq - WINDOW:]), cv_p[None])

    tok = n_dec * nbatch
    xs2d = jnp.transpose(x_sample, (1, 0, 2)).reshape(tok, d)
    rep = lambda a: jnp.tile(a, (n_dec, 1)).reshape(1, tok, d)
    st2d = jnp.transpose(state_conv[0], (1, 0, 2)).reshape((CONV_W - 1) * nbatch, d_conv)
    q_s, kv_s, gates_s, yc_s, sza_s, cv_s = _proj_sample(
        xs2d, rep(1.0 + scale[batch:n_c]), rep(shift[batch:n_c]), ng, wts, cw, st2d, nbatch, n_dec)

    q5 = jnp.transpose(q_s.reshape(n_dec, nbatch, N_KV, HPG, HEAD_DIM), (1, 2, 0, 3, 4))
    eye = jnp.eye(N_KV, dtype=F32)
    qbd = (q5[:, :, :, :, None, :] * eye[None, :, None, None, :, None]).reshape(
        nbatch, N_KV * n_dec * HPG, D_KV).astype(BF)
    g5 = gates_s[:, :3 * N_HEADS].reshape(n_dec, nbatch, 3, N_KV, HPG)
    g5 = jnp.transpose(g5, (1, 2, 3, 0, 4)).reshape(nbatch, 3, N_KV * n_dec * HPG, 1)
    gates3 = jnp.broadcast_to(g5, (nbatch, 3, N_KV * n_dec * HPG, D_KV))
    kv6 = jnp.transpose(kv_s.reshape(n_dec, nbatch, 6, D_KV), (1, 2, 0, 3))
    tails = jnp.pad(kv6[:, 0:2], ((0, 0), (0, 0), (0, 8 - n_dec), (0, 0)))
    tailsT = jnp.pad(jnp.transpose(kv6[:, 2:6], (0, 1, 3, 2)), ((0, 0), (0, 0), (0, 0), (0, LANES - n_dec)))
    nbc = -(-(past // BLK + TAIL_BLOCKS) // LANES) * LANES
    posT = lambda c: jnp.transpose(c, (0, 2, 3, 1)).reshape(c.shape[0], D_KV, c.shape[1])
    caches = tuple(posT(c[0]) for c in (cache_k_cmp, cache_v_cmp, cache_k_slc, cache_v_slc))
    o_s, s_kwT, s_vwT = _nsa_sample(
        page_table, caches, qbd, gates3, tails, tailsT, posT(cache_k_win[0]), posT(cache_v_win[0]),
        _sample_tables(rel_bias, past, n_dec, nbuf, nbc), cwk + cwv, n_dec)
    o6 = o_s.reshape(nbatch, N_KV, n_dec, HPG, N_KV, HEAD_DIM)
    o_diag = jnp.stack([o6[:, g, :, :, g, :] for g in range(N_KV)], axis=1)
    o_tok = jnp.transpose(o_diag, (2, 0, 1, 3, 4)).reshape(tok, d_attn).astype(BF)
    y_s = _merge(xs2d, o_tok, sza_s, yc_s, rep(gate[batch:n_c]), woa, woc, fg, tok, tok)
    y_sample = jnp.transpose(y_s.reshape(n_dec, nbatch, d), (1, 0, 2))
    new = lambda n: kv6[:, n].reshape(1, nbatch, n_dec, N_KV, HEAD_DIM)
    cv_out = jnp.transpose(cv_s.reshape(CONV_W - 1, nbatch, d_conv), (1, 0, 2))[None]
    s_states = (new(0), new(1), new(2), new(3), heads(s_kwT), heads(s_vwT), cv_out)

    return (y_prompt, y_sample) + p_states + s_states
```

```python
import functools
import math

import numpy as np
import jax
import jax.numpy as jnp
from jax import lax
from jax.experimental import pallas as pl
from jax.experimental.pallas import tpu as pltpu

BF = jnp.bfloat16
F32 = jnp.float32
I32 = jnp.int32

HEAD_DIM = 64
N_KV = 2
HPG = 8
N_HEADS = N_KV * HPG
D_KV = N_KV * HEAD_DIM
BLK = 64
BLK_SHIFT = 6
TAIL_BLOCKS = 16
XPITCH = 136
XTAIL_PAGES = 8
N_SEL = 16
WINDOW = 512
CMP_HID = 2 * HEAD_DIM
CONV_W = 3
N_BUCKETS = 32
MAX_DIST = 128
EPS = 1e-6
NEG = -1e30
M_INIT = -5e29
VALID_THRESH = -1e29
LOG2E = math.log2(math.e)
QT = 128
KT = 128
FAR_GROUP = 4
ACC_ROWS = HEAD_DIM + 16
NCOL = N_HEADS * QT
GCOL = HPG * QT
LANES = 128
SAMPLE_KEY_TILE = 8192
V7X_VMEM_LIMIT = 56 * 1024 * 1024


def _dot(a, b):
    return jnp.dot(a, b, preferred_element_type=F32)


def _dot_nt(a, b):
    return lax.dot_general(a, b, (((1,), (1,)), ((), ())), preferred_element_type=F32)


def _sigmoid(x):
    return 1.0 / (1.0 + jnp.exp(-x))


def _silu(x):
    return x * _sigmoid(x)


def _ada_kernel(c_ref, w_ref, b_ref, o_ref):
    o_ref[...] = _dot(c_ref[...].astype(BF), w_ref[...].astype(BF)) + b_ref[...]


def _ada(c_all, ada_w, ada_b):
    rows, d = c_all.shape
    n = ada_w.shape[1]
    return pl.pallas_call(
        _ada_kernel,
        grid=(n // d,),
        in_specs=[pl.BlockSpec((rows, d), lambda j: (0, 0)),
                  pl.BlockSpec((d, d), lambda j: (0, j)),
                  pl.BlockSpec((1, d), lambda j: (0, j))],
        out_specs=pl.BlockSpec((rows, d), lambda j: (0, j)),
        out_shape=jax.ShapeDtypeStruct((rows, n), F32),
        name="ada_modulation",
    )(c_all, ada_w, ada_b.reshape(1, n))


def _proj_body(x_ref, sc_ref, sh_ref, ng_ref, wq_ref, wkv_ref, wg_ref, wc_ref, wz_ref, cw_ref):
    x = x_ref[...]
    ms = jnp.mean(x * x, axis=-1, keepdims=True)
    xn = x * lax.rsqrt(ms + EPS) * ng_ref[...]
    h = xn * sc_ref[0] + sh_ref[0]
    hb = h.astype(BF)
    q = _dot(hb, wq_ref[...]) * (HEAD_DIM ** -0.5 * LOG2E)
    kv = _dot(hb, wkv_ref[...])
    gates = _sigmoid(_dot(hb, wg_ref[...]))
    c3 = _dot(hb, wc_ref[...])
    dc = c3.shape[1] // 3
    hc, bc, cc = c3[:, :dc], c3[:, dc:2 * dc], c3[:, 2 * dc:]
    u = cc * hc
    z = _dot(hb, wz_ref[...])
    da = z.shape[1] - dc
    sza = _silu(z[:, :da])
    szc = _silu(z[:, da:])
    return q, kv, gates, u, bc, sza, szc


def _proj_prompt_kernel(x_ref, sc_ref, sh_ref, ng_ref, wq_ref, wkv_ref, wg_ref, wc_ref, wz_ref, cw_ref,
                        qT_ref, kcT_ref, vcT_ref, ksT_ref, vsT_ref, kwT_ref, vwT_ref, kc_ref, vc_ref,
                        kslc_ref, vslcT_ref, kwin_ref, vwinT_ref, gates_ref, yc_ref, sza_ref, cv_ref,
                        uext_ref, *, tiles_per_batch, tm):
    t = pl.program_id(0)
    q, kv, gates, u, bc, sza, szc = _proj_body(
        x_ref, sc_ref, sh_ref, ng_ref, wq_ref, wkv_ref, wg_ref, wc_ref, wz_ref, cw_ref)
    qT_ref[0] = q.T.astype(BF)
    kvT = [kv[:, n * D_KV:(n + 1) * D_KV].T for n in range(6)]
    for n, r in enumerate((kcT_ref, vcT_ref, ksT_ref, vsT_ref, kwT_ref, vwT_ref)):
        r[0] = kvT[n]
    kc_ref[...] = kv[:, 0:D_KV]
    vc_ref[...] = kv[:, D_KV:2 * D_KV]
    kslc_ref[0] = kv[:, 2 * D_KV:3 * D_KV].astype(BF)
    vslcT_ref[0] = kvT[3].astype(BF)
    kwin_ref[0] = kv[:, 4 * D_KV:5 * D_KV].astype(BF)
    vwinT_ref[0] = kvT[5].astype(BF)
    gates_ref[...] = gates

    first = (t % tiles_per_batch) == 0

    @pl.when(first)
    def _():
        uext_ref[0:8, :] = jnp.zeros((8, u.shape[1]), F32)

    @pl.when(jnp.logical_not(first))
    def _():
        uext_ref[0:8, :] = uext_ref[tm:tm + 8, :]

    uext_ref[8:8 + tm, :] = u
    y = (cw_ref[2:3, :] * u + cw_ref[1:2, :] * uext_ref[7:7 + tm, :]
         + cw_ref[0:1, :] * uext_ref[6:6 + tm, :])
    yc_ref[...] = (bc * y * szc).astype(BF)
    sza_ref[...] = sza.astype(BF)
    cv_ref[0] = u[tm - (CONV_W - 1):, :]


def _proj_sample_kernel(x_ref, sc_ref, sh_ref, ng_ref, wq_ref, wkv_ref, wg_ref, wc_ref, wz_ref, cw_ref,
                        st_ref, q_ref, kv_ref, gates_ref, yc_ref, sza_ref, cv_ref, uext_ref, *, nb, nt):
    q, kv, gates, u, bc, sza, szc = _proj_body(
        x_ref, sc_ref, sh_ref, ng_ref, wq_ref, wkv_ref, wg_ref, wc_ref, wz_ref, cw_ref)
    q_ref[...] = q
    kv_ref[...] = kv
    gates_ref[...] = gates
    ns = (CONV_W - 1) * nb
    uext_ref[0:ns, :] = st_ref[...]
    uext_ref[ns:ns + nt * nb, :] = u
    y = cw_ref[2:3, :] * u
    for k in range(CONV_W - 1):
        y = y + cw_ref[k:k + 1, :] * uext_ref[k * nb:k * nb + nt * nb, :]
    yc_ref[...] = (bc * y * szc).astype(BF)
    sza_ref[...] = sza.astype(BF)
    cv_ref[...] = uext_ref[nt * nb:nt * nb + ns, :]


def _const_spec(shape):
    nd = len(shape)
    return pl.BlockSpec(shape, lambda *_: (0,) * nd, pipeline_mode=pl.Buffered(1))


def _proj_prompt(x2d, sc, sh, ng, wts, cw, batch, seq, tm):
    tokens, d = x2d.shape
    tpb = seq // tm
    wq, wkv, wg, wc, wz = wts
    dc = wc.shape[1] // 3
    da = wz.shape[1] - dc
    tok_spec = lambda w: pl.BlockSpec((tm, w), lambda t: (t, 0))
    rowT_spec = lambda r: pl.BlockSpec((1, r, tm), lambda t: (t // tpb, 0, t % tpb))
    row_spec = lambda w: pl.BlockSpec((1, tm, w), lambda t: (t // tpb, t % tpb, 0))
    mod_spec = pl.BlockSpec((1, 1, d), lambda t: (t // tpb, 0, 0))
    out_shape = (
        [jax.ShapeDtypeStruct((batch, wq.shape[1], seq), BF)]
        + [jax.ShapeDtypeStruct((batch, D_KV, seq), F32)] * 6
        + [jax.ShapeDtypeStruct((tokens, D_KV), F32)] * 2
        + [jax.ShapeDtypeStruct((batch, seq, D_KV), BF), jax.ShapeDtypeStruct((batch, D_KV, seq), BF),
           jax.ShapeDtypeStruct((batch, seq, D_KV), BF), jax.ShapeDtypeStruct((batch, D_KV, seq), BF),
           jax.ShapeDtypeStruct((tokens, LANES), F32),
           jax.ShapeDtypeStruct((tokens, dc), BF), jax.ShapeDtypeStruct((tokens, da), BF),
           jax.ShapeDtypeStruct((batch, CONV_W - 1, dc), F32)])
    out_specs = (
        [rowT_spec(wq.shape[1])] + [rowT_spec(D_KV)] * 6 + [tok_spec(D_KV)] * 2
        + [row_spec(D_KV), rowT_spec(D_KV), row_spec(D_KV), rowT_spec(D_KV),
           tok_spec(LANES), tok_spec(dc), tok_spec(da),
           pl.BlockSpec((1, CONV_W - 1, dc), lambda t: (t // tpb, 0, 0))])
    return pl.pallas_call(
        functools.partial(_proj_prompt_kernel, tiles_per_batch=tpb, tm=tm),
        grid=(tokens // tm,),
        in_specs=[tok_spec(d), mod_spec, mod_spec, _const_spec((1, d)),
                  _const_spec(wq.shape), _const_spec(wkv.shape), _const_spec(wg.shape),
                  _const_spec(wc.shape), _const_spec(wz.shape), _const_spec(cw.shape)],
        out_specs=out_specs,
        out_shape=out_shape,
        scratch_shapes=[pltpu.VMEM((tm + 8, dc), F32)],
        compiler_params=pltpu.CompilerParams(dimension_semantics=("arbitrary",),
                                             vmem_limit_bytes=V7X_VMEM_LIMIT),
        name="proj_prompt",
    )(x2d, sc, sh, ng, wq, wkv, wg, wc, wz, cw)


def _proj_sample(x2d, sc, sh, ng, wts, cw, state2d, nb, nt):
    tokens, d = x2d.shape
    wq, wkv, wg, wc, wz = wts
    dc = wc.shape[1] // 3
    da = wz.shape[1] - dc
    ns = (CONV_W - 1) * nb
    full = lambda a: pl.BlockSpec(a.shape, lambda i: (0,) * a.ndim)
    ins = (x2d, sc, sh, ng, wq, wkv, wg, wc, wz, cw, state2d)
    out_shape = [jax.ShapeDtypeStruct((tokens, wq.shape[1]), F32),
                 jax.ShapeDtypeStruct((tokens, wkv.shape[1]), F32),
                 jax.ShapeDtypeStruct((tokens, LANES), F32),
                 jax.ShapeDtypeStruct((tokens, dc), BF), jax.ShapeDtypeStruct((tokens, da), BF),
                 jax.ShapeDtypeStruct((ns, dc), F32)]
    return pl.pallas_call(
        functools.partial(_proj_sample_kernel, nb=nb, nt=nt),
        grid=(1,),
        in_specs=[full(a) for a in ins],
        out_specs=[pl.BlockSpec(s.shape, lambda i: (0, 0)) for s in out_shape],
        out_shape=out_shape,
        scratch_shapes=[pltpu.VMEM((ns + tokens, dc), F32)],
        compiler_params=pltpu.CompilerParams(vmem_limit_bytes=V7X_VMEM_LIMIT),
        name="proj_sample",
    )(*ins)


def _compress(x_ref, nblk, pe_ref, w1_ref, w2_ref):
    def body(pp, acc):
        xp = jnp.concatenate([x_ref[pl.ds(2 * pp, nblk, stride=BLK), :],
                              x_ref[pl.ds(2 * pp + 1, nblk, stride=BLK), :]], axis=1) + pe_ref[pl.ds(pp, 1), :]
        return acc + _dot(xp.astype(BF), w1_ref[pp])
    acc = lax.fori_loop(0, BLK // 2, body, jnp.zeros((nblk, N_KV * CMP_HID), F32), unroll=2)
    return _dot(_silu(acc).astype(BF), w2_ref[...])


def _compress_prompt_kernel(kc_ref, vc_ref, pek_ref, w1k_ref, w2k_ref, pev_ref, w1v_ref, w2v_ref,
                            kcmp_ref, vcmpT_ref, *, nblk):
    kcmp_ref[0] = _compress(kc_ref.at[0], nblk, pek_ref, w1k_ref, w2k_ref).astype(BF)
    vcmpT_ref[0] = _compress(vc_ref.at[0], nblk, pev_ref, w1v_ref, w2v_ref).T.astype(BF)


def _compress_prompt(kc, vc, cw, batch, seq):
    nblk = seq // BLK
    row = pl.BlockSpec((1, seq, D_KV), lambda b: (b, 0, 0))
    return pl.pallas_call(
        functools.partial(_compress_prompt_kernel, nblk=nblk),
        grid=(batch,),
        in_specs=[row, row] + [_const_spec(a.shape) for a in cw],
        out_specs=[pl.BlockSpec((1, nblk, D_KV), lambda b: (b, 0, 0)),
                   pl.BlockSpec((1, D_KV, nblk), lambda b: (b, 0, 0))],
        out_shape=[jax.ShapeDtypeStruct((batch, nblk, D_KV), BF),
                   jax.ShapeDtypeStruct((batch, D_KV, nblk), BF)],
        compiler_params=pltpu.CompilerParams(vmem_limit_bytes=V7X_VMEM_LIMIT),
        name="compress_prompt",
    )(kc, vc, *cw)


SCORE_REMOVED = -3.0


def _select_topk(score, axis, k):
    idx = lax.broadcasted_iota(I32, score.shape, axis)
    past_end = jnp.int32(score.shape[axis])
    picked = jnp.zeros(score.shape, F32)
    s = score
    for _ in range(k):
        m = jnp.max(s, axis=axis, keepdims=True)
        first = jnp.min(jnp.where(s == m, idx, past_end), axis=axis, keepdims=True)
        hit = idx == first
        picked = jnp.where(hit, 1.0, picked)
        s = jnp.where(hit, SCORE_REMOVED, s)
    return picked > 0.5


def _attend(br, lhs, rhs_ref, krows, bias, vT, m_ref, acc_ref):
    s = _dot(lhs, rhs_ref[0:krows, :])
    if bias is not None:
        s = s + bias
    _consume(br, s, vT, m_ref, acc_ref)


SUB = 8


def _col_max(s):
    part = jnp.max(s.reshape(s.shape[0] // SUB, SUB, s.shape[1]), axis=0)
    return jnp.broadcast_to(jnp.max(part, axis=0, keepdims=True), part.shape)


def _rows_op(op, x, r):
    return op(x.reshape(x.shape[0] // SUB, SUB, x.shape[1]), r[None]).reshape(x.shape)


def _pv_update(br, pb, vT, scale_old, scale_new, acc_ref):
    ones = jnp.ones((ACC_ROWS - HEAD_DIM, pb.shape[0]), BF)
    for g in range(N_KV):
        gs = slice(g * GCOL, (g + 1) * GCOL)
        v_aug = jnp.concatenate([vT[g * HEAD_DIM:(g + 1) * HEAD_DIM, :], ones], axis=0)
        acc = acc_ref[br, g]
        if scale_old is not None:
            acc = _rows_op(jnp.multiply, acc, scale_old[:, gs])
        acc = acc + _dot(v_aug, pb[:, gs])
        if scale_new is not None:
            acc = _rows_op(jnp.multiply, acc, scale_new[:, gs])
        acc_ref[br, g] = acc


def _consume(br, s, vT, m_ref, acc_ref):
    m_old = m_ref[br]
    m_new = jnp.maximum(m_old, _col_max(s))
    alpha = jnp.exp2(m_old - m_new)
    pb = jnp.exp2(_rows_op(jnp.subtract, s, m_new).astype(BF))
    m_ref[br] = m_new
    _pv_update(br, pb, vT, alpha, None, acc_ref)


def _nsa_prompt_kernel(qT_ref, gates_ref, kslc_ref, vslcT_ref, kwin_ref, vwinT_ref, kcmp_ref, vcmpT_ref,
                       cb_ref, sb_ref, w0_ref, o_ref,
                       rhs_ref, m_ref, acc_ref, s0_ref, s1_ref, *, nb):
    i = pl.program_id(1)

    for gh in range(N_HEADS):
        g = gh // HPG
        cols = slice(gh * QT, (gh + 1) * QT)
        rhs_ref[g * HEAD_DIM:(g + 1) * HEAD_DIM, cols] = qT_ref[0, gh * HEAD_DIM:(gh + 1) * HEAD_DIM, :]
        og = 1 - g
        rhs_ref[og * HEAD_DIM:(og + 1) * HEAD_DIM, cols] = jnp.zeros((HEAD_DIM, QT), BF)

    lg = _dot(kcmp_ref[0], rhs_ref[0:D_KV, :])
    rel = 2 * i - lax.broadcasted_iota(I32, (nb, NCOL), 0)
    bias = jnp.where(rel >= 3, 0.0, NEG)
    for rr in range(4):
        bias = jnp.where(rel == rr - 1, cb_ref[rr:rr + 1, :], bias)
    lgb = lg + bias
    mc = jnp.max(lgb, axis=0, keepdims=True)
    pc = jnp.exp2(lgb - mc)
    lc = jnp.sum(pc, axis=0, keepdims=True)
    pn = pc * jnp.where(mc > VALID_THRESH, 1.0 / lc, 0.0)
    pnb = pn.astype(BF)
    for g in range(N_KV):
        acc_ref[0, g, 0:HEAD_DIM, :] = _dot(vcmpT_ref[0, g * HEAD_DIM:(g + 1) * HEAD_DIM, :],
                                            pnb[:, g * GCOL:(g + 1) * GCOL])

    imp, valid = [], []
    for g in range(N_KV):
        a = pn[:, g * GCOL:g * GCOL + QT]
        for h in range(1, HPG):
            a = a + pn[:, g * GCOL + h * QT:g * GCOL + (h + 1) * QT]
        imp.append(a)
        valid.append(lgb[:, g * GCOL:g * GCOL + QT] > VALID_THRESH)
    imp = jnp.concatenate(imp, axis=1)
    valid = jnp.concatenate(valid, axis=1)
    jrow = lax.broadcasted_iota(I32, (nb, N_KV * QT), 0)
    tpos = i * QT + (lax.broadcasted_iota(I32, (nb, N_KV * QT), 1) & (QT - 1))
    score = jnp.where(valid, imp, jnp.where(jrow * BLK <= tpos, 2.0, -1.0))
    sel = _select_topk(score, 0, min(N_SEL, nb))
    selbias = jnp.where(jnp.logical_and(sel, score >= 0.0), 0.0, NEG).astype(BF)
    for gh in range(N_HEADS):
        g = gh // HPG
        rhs_ref[D_KV:D_KV + nb, gh * QT:(gh + 1) * QT] = selbias[:, g * QT:(g + 1) * QT]

    for br in (1, 2):
        m_ref[br] = jnp.full((SUB, NCOL), M_INIT, F32)
        acc_ref[br] = jnp.zeros((N_KV, ACC_ROWS, GCOL), F32)

    def sel_keys(kt, tk):
        k0 = pl.multiple_of(kt * KT, KT)
        k = kslc_ref[0, pl.ds(k0, tk), :]
        jj = lax.broadcasted_iota(I32, (tk, nb), 1)
        kk = lax.broadcasted_iota(I32, (tk, nb), 0)
        onehot = jnp.where(jj == (KT // BLK) * kt + jnp.right_shift(kk, BLK_SHIFT), 1.0, 0.0).astype(BF)
        return jnp.concatenate([k, onehot], axis=1), vslcT_ref[0, :, pl.ds(k0, tk)]

    def sel_tile(kt, tk, bias_tile):
        ka, vT = sel_keys(kt, tk)
        _attend(1, ka, rhs_ref, D_KV + nb, bias_tile, vT, m_ref, acc_ref)

    n_far = jnp.maximum(i - 1, 0)
    n_big = n_far // FAR_GROUP
    n_pairs = n_big // 2
    big = FAR_GROUP * KT

    def far_scores(j):
        ka, _ = sel_keys(j * FAR_GROUP, big)
        return _dot(ka, rhs_ref[...])

    def far_step(cur_ref, nxt_ref, j_cur, j_nxt):
        m_cur = m_ref[1]
        pb = jnp.exp2(_rows_op(jnp.subtract, cur_ref[...], m_cur).astype(BF))
        s_nxt = far_scores(j_nxt)
        nxt_ref[...] = s_nxt
        m_nxt = jnp.maximum(m_cur, _col_max(s_nxt))
        vT = vslcT_ref[0, :, pl.ds(pl.multiple_of(j_cur * big, big), big)]
        _pv_update(1, pb, vT, None, jnp.exp2(m_cur - m_nxt), acc_ref)
        m_ref[1] = m_nxt

    @pl.when(n_pairs > 0)
    def _():
        s = far_scores(0)
        s0_ref[...] = s
        m_ref[1] = jnp.maximum(m_ref[1], _col_max(s))

    def far_pair(jp, c):
        @pl.when(jp < n_pairs)
        def _():
            far_step(s0_ref, s1_ref, 2 * jp, 2 * jp + 1)

        @pl.when(2 * jp + 1 < n_big)
        def _():
            far_step(s1_ref, s0_ref, 2 * jp + 1, jnp.minimum(2 * jp + 2, 2 * n_pairs - 1))
        return c
    lax.fori_loop(0, n_pairs, far_pair, 0)

    @pl.when(n_big > 2 * n_pairs)
    def _():
        sel_tile((n_big - 1) * FAR_GROUP, big, None)

    def far_small(kt, c):
        sel_tile(kt, KT, None)
        return c
    lax.fori_loop(n_big * FAR_GROUP, n_far, far_small, 0)

    n_win = WINDOW // KT + 1

    @pl.when(i >= n_win - 1)
    def _():
        ka, vT_band = sel_keys(i - 1, 2 * KT)
        s_band = _dot(ka, rhs_ref[...]) + sb_ref[...]
        k0 = pl.multiple_of((i - (n_win - 1)) * KT, KT)
        s_win = _dot(kwin_ref[0, pl.ds(k0, n_win * KT), :], rhs_ref[0:D_KV, :]) + jnp.concatenate(
            [w0_ref[...], jnp.zeros(((n_win - 3) * KT, NCOL), F32), sb_ref[...]], axis=0)
        _consume(1, s_band, vT_band, m_ref, acc_ref)
        _consume(2, s_win, vwinT_ref[0, :, pl.ds(k0, n_win * KT)], m_ref, acc_ref)

    @pl.when(jnp.logical_and(i >= 1, i < n_win - 1))
    def _():
        sel_tile(i - 1, 2 * KT, sb_ref[...])

    @pl.when(i == 0)
    def _():
        sel_tile(0, KT, sb_ref[KT:2 * KT, :])

    @pl.when(i < n_win - 1)
    def _():
        for r in range(1, n_win):
            kt = i - (n_win - 1) + r

            @pl.when(kt >= 0)
            def _():
                k0 = pl.multiple_of(kt * KT, KT)
                r0 = (r - (n_win - 2)) * KT
                bias = sb_ref[r0:r0 + KT, :] if r >= n_win - 2 else None
                _attend(2, kwin_ref[0, pl.ds(k0, KT), :], rhs_ref, D_KV, bias,
                        vwinT_ref[0, :, pl.ds(k0, KT)], m_ref, acc_ref)

    gts = gates_ref[0, 0]
    for g in range(N_KV):
        cs = slice(g * GCOL, (g + 1) * GCOL)
        o = gts[0:1, cs] * acc_ref[0, g, 0:HEAD_DIM, :]
        for br in (1, 2):
            l = acc_ref[br, g, HEAD_DIM:HEAD_DIM + 1, :]
            o = o + (gts[br:br + 1, cs] * jnp.where(l > 0.0, 1.0 / l, 0.0)) * acc_ref[br, g, 0:HEAD_DIM, :]
        for hp in range(HPG // 2):
            pair = jnp.concatenate([o[:, (2 * hp) * QT:(2 * hp + 1) * QT],
                                    o[:, (2 * hp + 1) * QT:(2 * hp + 2) * QT]], axis=0)
            c0 = g * HPG * HEAD_DIM + hp * 2 * HEAD_DIM
            o_ref[0, :, c0:c0 + 2 * HEAD_DIM] = pair.T.astype(BF)


def _nsa_prompt(qT, gates, kslc, vslcT, kwin, vwinT, kcmp, vcmpT, tables, batch, seq):
    nb = seq // BLK
    nq = seq // QT
    cb, sb, w0 = tables
    dq = qT.shape[1]
    per_b = lambda shape: pl.BlockSpec((1,) + shape, lambda b, i: (b, 0, 0))
    return pl.pallas_call(
        functools.partial(_nsa_prompt_kernel, nb=nb),
        grid=(batch, nq),
        in_specs=[pl.BlockSpec((1, dq, QT), lambda b, i: (b, 0, i)),
                  pl.BlockSpec((1, 1, 8, NCOL), lambda b, i: (b, i, 0, 0)),
                  per_b((seq, D_KV)), per_b((D_KV, seq)), per_b((seq, D_KV)), per_b((D_KV, seq)),
                  per_b((nb, D_KV)), per_b((D_KV, nb)),
                  _const_spec(cb.shape), _const_spec(sb.shape), _const_spec(w0.shape)],
        out_specs=pl.BlockSpec((1, QT, dq), lambda b, i: (b, i, 0)),
        out_shape=jax.ShapeDtypeStruct((batch, seq, dq), BF),
        scratch_shapes=[pltpu.VMEM((D_KV + nb, NCOL), BF),
                        pltpu.VMEM((3, SUB, NCOL), F32),
                        pltpu.VMEM((3, N_KV, ACC_ROWS, GCOL), F32),
                        pltpu.VMEM((FAR_GROUP * KT, NCOL), F32),
                        pltpu.VMEM((FAR_GROUP * KT, NCOL), F32)],
        compiler_params=pltpu.CompilerParams(dimension_semantics=("arbitrary", "arbitrary"),
                                             vmem_limit_bytes=V7X_VMEM_LIMIT),
        name="nsa_prompt",
    )(qT, gates, kslc, vslcT, kwin, vwinT, kcmp, vcmpT, cb, sb, w0)


def _merge_kernel(x_ref, o_ref, sza_ref, yc_ref, gate_ref, woa_ref, woc_ref, fg_ref, y_ref):
    og = (o_ref[...].astype(F32) * sza_ref[...].astype(F32)).astype(BF)
    acc = _dot(og, woa_ref[...]) + _dot(yc_ref[...], woc_ref[...])
    xo = x_ref[...] + gate_ref[0] * acc
    ms = jnp.mean(xo * xo, axis=-1, keepdims=True)
    y_ref[...] = xo * lax.rsqrt(ms + EPS) * fg_ref[...]


def _merge(x2d, o2d, sza, yc, gate, woa, woc, fg, tm, rows_per_gate):
    tokens, d = x2d.shape
    tok = lambda w: pl.BlockSpec((tm, w), lambda t: (t, 0))
    gr = gate.shape[1]
    return pl.pallas_call(
        _merge_kernel,
        grid=(tokens // tm,),
        in_specs=[tok(d), tok(o2d.shape[1]), tok(sza.shape[1]), tok(yc.shape[1]),
                  pl.BlockSpec((1, gr, d), lambda t: ((t * tm) // rows_per_gate, 0, 0)),
                  _const_spec(woa.shape), _const_spec(woc.shape), _const_spec(fg.shape)],
        out_specs=tok(d),
        out_shape=jax.ShapeDtypeStruct((tokens, d), F32),
        compiler_params=pltpu.CompilerParams(dimension_semantics=("arbitrary",),
                                             vmem_limit_bytes=V7X_VMEM_LIMIT),
        name="merge_out",
    )(x2d, o2d, sza, yc, gate, woa, woc, fg)


def _softmax_rows(parts):
    m = parts[0].max(axis=1, keepdims=True)
    for s in parts[1:]:
        m = jnp.maximum(m, s.max(axis=1, keepdims=True))
    ps = [jnp.exp2(s - m) for s in parts]
    l = ps[0].sum(axis=1, keepdims=True)
    for p in ps[1:]:
        l = l + p.sum(axis=1, keepdims=True)
    inv = jnp.where(jnp.logical_and(m > VALID_THRESH, l > 0.0), 1.0 / l, 0.0)
    return ps, inv


def _nsa_sample_kernel(pt_ref, ckc_ref, cvc_ref, cks_ref, cvs_ref,
                       qbd_ref, gates_ref, tails_ref, tailsT_ref, bkwT_ref, bvwT_ref,
                       cbs_ref, sbl_ref, sbt_ref, wbl_ref, wbt_ref, elocT_ref,
                       pek_ref, w1k_ref, w2k_ref, pev_ref, w1v_ref, w2v_ref,
                       o_ref, skwT_ref, svwT_ref,
                       buf_ref, sem, xbuf_ref, cmpf_ref, kcmp_ref, vcmp_ref, sbuf_ref, pbuf_ref,
                       *, n_pages, page, n_dec):
    b = pl.program_id(0)
    nbatch = pl.num_programs(0)
    past = n_pages * page
    nbp = past // BLK
    nbc = kcmp_ref.shape[0]
    n_tiles = past // SAMPLE_KEY_TILE
    nrow = qbd_ref.shape[1]
    caches = (ckc_ref, cvc_ref, cks_ref, cvs_ref)

    def page_copy(cache_ref, pg, j, slot):
        return pltpu.make_async_copy(cache_ref.at[pg],
                                     buf_ref.at[slot, :, pl.ds(pl.multiple_of(j * page, page), page)],
                                     sem.at[slot])

    def start_fetch(cache_ref, bb, slot):
        def body(j, c):
            page_copy(cache_ref, pt_ref[bb, j], j, slot).start()
            return c
        lax.fori_loop(0, n_pages, body, 0, unroll=4)

    def wait_fetch(cache_ref, slot):
        def body(j, c):
            page_copy(cache_ref, 0, j, slot).wait()
            return c
        lax.fori_loop(0, n_pages, body, 0)

    def stage(c):
        if c + 1 < len(caches):
            start_fetch(caches[c + 1], b, (c + 1) % 2)
        else:
            @pl.when(b + 1 < nbatch)
            def _():
                start_fetch(caches[0], b + 1, 0)
        wait_fetch(caches[c], c % 2)
        return buf_ref.at[c % 2]

    @pl.when(b == 0)
    def _():
        start_fetch(caches[0], 0, 0)

    qbd = qbd_ref[0]

    def compress_cache(c, slot_ref, pe_ref, w1_ref, w2_ref, dst_ref):
        def xpose(j, carry):
            src = slot_ref[:, pl.ds(pl.multiple_of(j * page, page), page)]
            xbuf_ref[pl.ds(pl.multiple_of(j * XPITCH, 8), page), :] = src.T
            return carry
        lax.fori_loop(0, n_pages, xpose, 0, unroll=8)
        n_extra = XTAIL_PAGES * XPITCH
        xbuf_ref[pl.ds(n_pages * XPITCH, n_extra), :] = jnp.zeros((n_extra, D_KV), F32)
        xbuf_ref[pl.ds(n_pages * XPITCH, 8), :] = tails_ref[0, c]
        n_even = n_pages + XTAIL_PAGES

        def rows(r0, n):
            return jnp.concatenate([xbuf_ref[pl.ds(r0, n, stride=XPITCH), :],
                                    xbuf_ref[pl.ds(r0 + 1, n, stride=XPITCH), :]], axis=1)

        def body(pp, acc):
            acc_e, acc_o = acc
            pe = pe_ref[pl.ds(pp, 1), :]
            w = w1_ref[pp]
            xe = rows(2 * pp, n_even) + pe
            xo = rows(2 * pp + BLK, n_pages) + pe
            return acc_e + _dot(xe.astype(BF), w), acc_o + _dot(xo.astype(BF), w)
        acc_e, acc_o = lax.fori_loop(
            0, BLK // 2, body,
            (jnp.zeros((n_even, N_KV * CMP_HID), F32), jnp.zeros((n_pages, N_KV * CMP_HID), F32)), unroll=2)
        out_e = _dot(_silu(acc_e).astype(BF), w2_ref[...])
        out_o = _dot(_silu(acc_o).astype(BF), w2_ref[...])
        cmpf_ref[pl.ds(0, n_pages, stride=2), :] = out_e[0:n_pages]
        cmpf_ref[pl.ds(1, n_pages, stride=2), :] = out_o
        cmpf_ref[nbp:nbp + XTAIL_PAGES, :] = out_e[n_pages:n_even]
        cmpf_ref[nbp + XTAIL_PAGES:nbp + TAIL_BLOCKS, :] = jnp.zeros((TAIL_BLOCKS - XTAIL_PAGES, D_KV), F32)
        dst_ref[0:nbp + TAIL_BLOCKS, :] = cmpf_ref[...].astype(BF)
        dst_ref[nbp + TAIL_BLOCKS:nbc, :] = jnp.zeros((nbc - nbp - TAIL_BLOCKS, D_KV), BF)

    compress_cache(0, stage(0), pek_ref, w1k_ref, w2k_ref, kcmp_ref)
    compress_cache(1, stage(1), pev_ref, w1v_ref, w2v_ref, vcmp_ref)

    lgc = _dot_nt(qbd, kcmp_ref[...]) + cbs_ref[...]
    (pc,), invc = _softmax_rows([lgc])
    pn = pc * invc
    o_cmp = _dot(pn.astype(BF), vcmp_ref[...])
    imp = jnp.sum(pn.reshape(nrow // HPG, HPG, nbc), axis=1)
    jcol = lax.broadcasted_iota(I32, imp.shape, 1)
    score = jnp.where(jcol < nbp, imp, jnp.where(jcol == nbp, 2.0, -1.0))
    sel = _select_topk(score, 1, N_SEL)
    selbias = jnp.where(jnp.logical_and(sel, score >= 0.0), 0.0, NEG)
    selb = jnp.broadcast_to(selbias[:, None, :], (nrow // HPG, HPG, nbc)).reshape(nrow, nbc).astype(BF)

    kslot = stage(2)
    tile = lambda t: slice(t * SAMPLE_KEY_TILE, (t + 1) * SAMPLE_KEY_TILE)
    for t in range(n_tiles):
        sbuf_ref[t] = (_dot(qbd, kslot[:, tile(t)].astype(BF))
                       + _dot(selb[:, t * LANES:(t + 1) * LANES], elocT_ref[...]))
    last = slice(SAMPLE_KEY_TILE - LANES, SAMPLE_KEY_TILE)
    sbuf_ref[n_tiles - 1, :, last] = sbuf_ref[n_tiles - 1, :, last] + sbl_ref[...]
    s_tail = (_dot(qbd, tailsT_ref[0, 0].astype(BF))
              + _dot(selb[:, nbp:nbp + LANES], elocT_ref[:, 0:LANES]) + sbt_ref[...])
    ps, inv_sel = _softmax_rows([sbuf_ref[t] for t in range(n_tiles)] + [s_tail])
    for t in range(n_tiles):
        pbuf_ref[t] = ps[t].astype(BF)
    p_tail = ps[n_tiles].astype(BF)

    vslot = stage(3)
    o_sel = _dot_nt(p_tail, tailsT_ref[0, 1].astype(BF))
    for t in range(n_tiles):
        o_sel = o_sel + _dot_nt(pbuf_ref[t], vslot[:, tile(t)].astype(BF))
    o_sel = o_sel * inv_sel

    s_w = _dot(qbd, bkwT_ref[0].astype(BF)) + wbl_ref[...]
    s_wt = _dot(qbd, tailsT_ref[0, 2].astype(BF)) + wbt_ref[...]
    (p_w, p_wt), inv_w = _softmax_rows([s_w, s_wt])
    o_win = (_dot_nt(p_w.astype(BF), bvwT_ref[0].astype(BF))
             + _dot_nt(p_wt.astype(BF), tailsT_ref[0, 3].astype(BF))) * inv_w

    o_ref[0] = gates_ref[0, 0] * o_cmp + gates_ref[0, 1] * o_sel + gates_ref[0, 2] * o_win

    nbuf = bkwT_ref.shape[2]
    lane = lax.broadcasted_iota(I32, (D_KV, LANES), 1)
    for n, src, dst in ((2, bkwT_ref, skwT_ref), (3, bvwT_ref, svwT_ref)):
        rolled = pltpu.roll(src[0], nbuf - n_dec, axis=1)
        new = pltpu.roll(tailsT_ref[0, n], LANES - n_dec, axis=1)
        dst[0, :, 0:nbuf - LANES] = rolled[:, 0:nbuf - LANES]
        dst[0, :, nbuf - LANES:nbuf] = jnp.where(lane >= LANES - n_dec, new, rolled[:, nbuf - LANES:nbuf])


def _nsa_sample(page_table, caches, qbd, gates3, tails, tailsT, bkwT, bvwT, tables, cw, n_dec):
    nbatch, n_pages = page_table.shape
    page = caches[0].shape[2]
    past = n_pages * page
    cbs, sbl, sbt, wbl, wbt, elocT = tables
    nbc = cbs.shape[1]
    nrow = qbd.shape[1]
    nbuf = bkwT.shape[2]
    nbp = past // BLK
    n_tiles = past // SAMPLE_KEY_TILE
    assert past % SAMPLE_KEY_TILE == 0 and nbp % LANES == 0 and page == 2 * BLK and nbuf % LANES == 0
    per_b = lambda a: pl.BlockSpec((1,) + a.shape[1:], lambda b, pt: (b,) + (0,) * (a.ndim - 1))
    const = lambda a: pl.BlockSpec(a.shape, lambda b, pt: (0,) * a.ndim, pipeline_mode=pl.Buffered(1))
    hbm = pl.BlockSpec(memory_space=pl.ANY)
    grid_spec = pltpu.PrefetchScalarGridSpec(
        num_scalar_prefetch=1,
        grid=(nbatch,),
        in_specs=[hbm] * 4 + [per_b(a) for a in (qbd, gates3, tails, tailsT, bkwT, bvwT)]
                 + [const(a) for a in tables] + [const(a) for a in cw],
        out_specs=[pl.BlockSpec((1, nrow, D_KV), lambda b, pt: (b, 0, 0)),
                   pl.BlockSpec((1, D_KV, nbuf), lambda b, pt: (b, 0, 0)),
                   pl.BlockSpec((1, D_KV, nbuf), lambda b, pt: (b, 0, 0))],
        scratch_shapes=[pltpu.VMEM((2, D_KV, past), F32),
                        pltpu.SemaphoreType.DMA((2,)),
                        pltpu.VMEM(((n_pages + XTAIL_PAGES) * XPITCH, D_KV), F32),
                        pltpu.VMEM((nbp + TAIL_BLOCKS, D_KV), F32),
                        pltpu.VMEM((nbc, D_KV), BF),
                        pltpu.VMEM((nbc, D_KV), BF),
                        pltpu.VMEM((n_tiles, nrow, SAMPLE_KEY_TILE), F32),
                        pltpu.VMEM((n_tiles, nrow, SAMPLE_KEY_TILE), BF)])
    return pl.pallas_call(
        functools.partial(_nsa_sample_kernel, n_pages=n_pages, page=page, n_dec=n_dec),
        grid_spec=grid_spec,
        out_shape=[jax.ShapeDtypeStruct((nbatch, nrow, D_KV), F32),
                   jax.ShapeDtypeStruct((nbatch, D_KV, nbuf), F32),
                   jax.ShapeDtypeStruct((nbatch, D_KV, nbuf), F32)],
        compiler_params=pltpu.CompilerParams(dimension_semantics=("arbitrary",),
                                             vmem_limit_bytes=V7X_VMEM_LIMIT),
        name="nsa_sample",
    )(page_table, *caches, qbd, gates3, tails, tailsT, bkwT, bvwT, *tables, *cw)


def _t5_bucket_np(n):
    d = np.maximum(np.arange(n), 0)
    max_exact = N_BUCKETS // 2
    nf = np.maximum(d, max_exact).astype(np.float32)
    large = max_exact + (np.log(nf / np.float32(max_exact)) / np.float32(math.log(MAX_DIST / max_exact))
                         * np.float32(N_BUCKETS - max_exact)).astype(np.int32)
    return np.where(d < max_exact, d, np.minimum(large, N_BUCKETS - 1))


N_NEAR = 256
_BUCKETS = _t5_bucket_np(N_NEAR)
assert np.all(_BUCKETS[QT + 1:] == N_BUCKETS - 1)


def _shifted_bias(rel_bias):
    return (rel_bias[_BUCKETS[:N_NEAR]] - rel_bias[N_BUCKETS - 1][None, :]) * LOG2E


def _bias_cols(tp, dist):
    val = tp[np.clip(dist, 0, N_NEAR - 1)]
    val = jnp.where((dist >= 0)[:, :, None], val, NEG)
    return jnp.transpose(val, (0, 2, 1)).reshape(dist.shape[0], -1)


def _prompt_tables(rel_bias, nb):
    tp = _shifted_bias(rel_bias)
    q = np.arange(QT)[None, :]
    r = np.arange(-1, 7)[:, None]
    d_cmp = np.where(r <= 2, BLK * r - (BLK - 1) + q, -1)
    cb = _bias_cols(tp, d_cmp)
    n_neg, rows = 2 * KT, 2 * KT
    w = jnp.concatenate([jnp.full((N_HEADS, n_neg), NEG, F32), tp.T, jnp.zeros((N_HEADS, 1), F32)], axis=1)
    period = w.shape[1]
    skew = jnp.tile(w, (1, rows))[:, :rows * (period - 1)].reshape(N_HEADS, rows, period - 1)
    c0 = n_neg + KT
    sb = jnp.transpose(skew[:, :, c0:c0 + QT], (1, 0, 2)).reshape(rows, N_HEADS * QT)
    kk = np.arange(KT)[:, None]
    w0 = jnp.tile(jnp.asarray(np.where(kk >= q, 0.0, NEG), F32), (1, N_HEADS))
    return cb, sb, w0


def _sample_rows_bias(tp, dist):
    val = tp[np.clip(dist, 0, N_NEAR - 1)]
    val = jnp.where((dist >= 0)[:, :, None], val, NEG)
    val = val.reshape(dist.shape[0], dist.shape[1], N_KV, HPG)
    return jnp.transpose(val, (2, 0, 3, 1)).reshape(-1, dist.shape[1])


def _sample_tables(rel_bias, past, n_dec, nbuf, nbc):
    tp = _shifted_bias(rel_bias)
    q = np.arange(n_dec)[:, None]
    nbp = past // BLK
    j = np.arange(nbc)[None, :]
    cbs = _sample_rows_bias(tp, np.where(j < nbp, past + q - (BLK * j + BLK - 1), -1))
    kk = np.arange(LANES)[None, :]
    sbl = _sample_rows_bias(tp, q + LANES - kk)
    sbt = _sample_rows_bias(tp, q - kk)
    r = np.arange(nbuf)[None, :]
    d_w = nbuf + q - r
    wbl = _sample_rows_bias(tp, np.where(d_w <= WINDOW, d_w, -1))
    wbt = sbt
    kk = np.arange(SAMPLE_KEY_TILE)[None, :]
    elocT = jnp.asarray(kk // BLK == np.arange(LANES)[:, None], BF)
    return cbs, sbl, sbt, wbl, wbt, elocT


def _compress_weights(pe, w1, w2):
    pe2 = jnp.tile(pe, (1, N_KV)).reshape(BLK // 2, 2 * D_KV)
    w1r = w1.reshape(BLK, HEAD_DIM, CMP_HID)
    eye = jnp.eye(N_KV, dtype=w1.dtype)
    w1bd = jnp.einsum("pdh,gk->pgdkh", w1r, eye).reshape(BLK // 2, 2 * D_KV, N_KV * CMP_HID).astype(BF)
    w2bd = jnp.einsum("hd,gk->ghkd", w2, eye).reshape(N_KV * CMP_HID, D_KV).astype(BF)
    return pe2, w1bd, w2bd


def _split_w_in(w_in, d_attn, d_conv, d_mix):
    sizes = (d_attn,) + (D_KV,) * 6 + (3 * N_HEADS,) + (d_conv,) * 3 + (d_mix,)
    offs = np.cumsum((0,) + sizes)
    wq = w_in[:, offs[0]:offs[1]]
    wkv = w_in[:, offs[1]:offs[7]]
    wg = jnp.pad(w_in[:, offs[7]:offs[8]], ((0, 0), (0, LANES - 3 * N_HEADS)))
    wc = w_in[:, offs[8]:offs[11]]
    wz = w_in[:, offs[11]:offs[12]]
    return tuple(w.astype(BF) for w in (wq, wkv, wg, wc, wz))


def kernel(x_prompt, x_sample, cache_k_cmp, cache_v_cmp, cache_k_slc, cache_v_slc, cache_k_win,
           cache_v_win, state_conv, page_table, c_prompt, c_sample, ada_w, ada_b, norm_g, w_in,
           cmp_pe_k, cmp_w1_k, cmp_w2_k, cmp_pe_v, cmp_w1_v, cmp_w2_v, conv_w, w_out, rel_bias, final_g):
    depth = ada_w.shape[0]
    assert depth == 1
    batch, seq, d = x_prompt.shape
    nbatch, n_dec, _ = x_sample.shape
    d_attn = N_HEADS * HEAD_DIM
    d_conv = conv_w.shape[2]
    d_mix = d_attn + d_conv
    n_pages = page_table.shape[1]
    page = cache_k_cmp.shape[2]
    past = n_pages * page
    nbuf = cache_k_win.shape[2]
    tm = 512
    assert seq % tm == 0 and seq >= WINDOW and nbuf == WINDOW

    wts = _split_w_in(w_in[0], d_attn, d_conv, d_mix)
    cwk = _compress_weights(cmp_pe_k[0], cmp_w1_k[0], cmp_w2_k[0])
    cwv = _compress_weights(cmp_pe_v[0], cmp_w1_v[0], cmp_w2_v[0])
    woa = w_out[0, :d_attn].astype(BF)
    woc = w_out[0, d_attn:].astype(BF)
    ng = norm_g[0].reshape(1, d)
    fg = final_g.reshape(1, d)
    cw = conv_w[0]

    n_c = batch + nbatch
    c_all = jnp.pad(jnp.concatenate([c_prompt, c_sample], axis=0), ((0, (-n_c) % 8), (0, 0)))
    mod = _ada(c_all, ada_w[0], ada_b[0])
    shift, scale, gate = mod[:, :d], mod[:, d:2 * d], mod[:, 2 * d:]

    x2d = x_prompt.reshape(batch * seq, d)
    sc_p = (1.0 + scale[:batch]).reshape(batch, 1, d)
    sh_p = shift[:batch].reshape(batch, 1, d)
    (qT, kcT, vcT, ksT, vsT, kwT, vwT, kc, vc, kslc, vslcT, kwin, vwinT, gates_tok, yc, sza, cv_p) = _proj_prompt(
        x2d, sc_p, sh_p, ng, wts, cw, batch, seq, tm)
    kcmp, vcmpT = _compress_prompt(kc.reshape(batch, seq, D_KV), vc.reshape(batch, seq, D_KV),
                                   cwk + cwv, batch, seq)
    nq = seq // QT
    gates = gates_tok[:, :3 * N_HEADS].reshape(batch, nq, QT, 3, N_HEADS)
    gates = jnp.transpose(gates, (0, 1, 3, 4, 2)).reshape(batch, nq, 3, NCOL)
    gates = jnp.pad(gates, ((0, 0), (0, 0), (0, 5), (0, 0)))
    o_p = _nsa_prompt(qT, gates, kslc, vslcT, kwin, vwinT, kcmp, vcmpT,
                      _prompt_tables(rel_bias, seq // BLK), batch, seq)
    y_p = _merge(x2d, o_p.reshape(batch * seq, d_attn), sza, yc, gate[:batch].reshape(batch, 1, d),
                 woa, woc, fg, tm, seq)
    y_prompt = y_p.reshape(batch, seq, d)
    heads = lambda aT: jnp.transpose(aT.reshape(aT.shape[0], N_KV, HEAD_DIM, aT.shape[2]), (0, 3, 1, 2))[None]
    p_states = (heads(kcT), heads(vcT), heads(ksT), heads(vsT),
                heads(kwT[:, :, seq - WINDOW:]), heads(vwT[:, :, seq - WINDOW:]), cv_p[None])

    tok = n_dec * nbatch
    xs2d = jnp.transpose(x_sample, (1, 0, 2)).reshape(tok, d)
    rep = lambda a: jnp.tile(a, (n_dec, 1)).reshape(1, tok, d)
    st2d = jnp.transpose(state_conv[0], (1, 0, 2)).reshape((CONV_W - 1) * nbatch, d_conv)
    q_s, kv_s, gates_s, yc_s, sza_s, cv_s = _proj_sample(
        xs2d, rep(1.0 + scale[batch:n_c]), rep(shift[batch:n_c]), ng, wts, cw, st2d, nbatch, n_dec)

    q5 = jnp.transpose(q_s.reshape(n_dec, nbatch, N_KV, HPG, HEAD_DIM), (1, 2, 0, 3, 4))
    eye = jnp.eye(N_KV, dtype=F32)
    qbd = (q5[:, :, :, :, None, :] * eye[None, :, None, None, :, None]).reshape(
        nbatch, N_KV * n_dec * HPG, D_KV).astype(BF)
    g5 = gates_s[:, :3 * N_HEADS].reshape(n_dec, nbatch, 3, N_KV, HPG)
    g5 = jnp.transpose(g5, (1, 2, 3, 0, 4)).reshape(nbatch, 3, N_KV * n_dec * HPG, 1)
    gates3 = jnp.broadcast_to(g5, (nbatch, 3, N_KV * n_dec * HPG, D_KV))
    kv6 = jnp.transpose(kv_s.reshape(n_dec, nbatch, 6, D_KV), (1, 2, 0, 3))
    tails = jnp.pad(kv6[:, 0:2], ((0, 0), (0, 0), (0, 8 - n_dec), (0, 0)))
    tailsT = jnp.pad(jnp.transpose(kv6[:, 2:6], (0, 1, 3, 2)), ((0, 0), (0, 0), (0, 0), (0, LANES - n_dec)))
    nbc = -(-(past // BLK + TAIL_BLOCKS) // LANES) * LANES
    posT = lambda c: jnp.transpose(c, (0, 2, 3, 1)).reshape(c.shape[0], D_KV, c.shape[1])
    caches = tuple(posT(c[0]) for c in (cache_k_cmp, cache_v_cmp, cache_k_slc, cache_v_slc))
    o_s, s_kwT, s_vwT = _nsa_sample(
        page_table, caches, qbd, gates3, tails, tailsT, posT(cache_k_win[0]), posT(cache_v_win[0]),
        _sample_tables(rel_bias, past, n_dec, nbuf, nbc), cwk + cwv, n_dec)
    o6 = o_s.reshape(nbatch, N_KV, n_dec, HPG, N_KV, HEAD_DIM)
    o_diag = jnp.stack([o6[:, g, :, :, g, :] for g in range(N_KV)], axis=1)
    o_tok = jnp.transpose(o_diag, (2, 0, 1, 3, 4)).reshape(tok, d_attn).astype(BF)
    y_s = _merge(xs2d, o_tok, sza_s, yc_s, rep(gate[batch:n_c]), woa, woc, fg, tok, tok)
    y_sample = jnp.transpose(y_s.reshape(n_dec, nbatch, d), (1, 0, 2))
    new = lambda n: kv6[:, n].reshape(1, nbatch, n_dec, N_KV, HEAD_DIM)
    cv_out = jnp.transpose(cv_s.reshape(CONV_W - 1, nbatch, d_conv), (1, 0, 2))[None]
    s_states = (new(0), new(1), new(2), new(3), heads(s_kwT), heads(s_vwT), cv_out)

    return (y_prompt, y_sample) + p_states + s_states
```

```python
import functools
import math

import numpy as np
import jax
import jax.numpy as jnp
from jax import lax
from jax.experimental import pallas as pl
from jax.experimental.pallas import tpu as pltpu

BF = jnp.bfloat16
F32 = jnp.float32
I32 = jnp.int32

HEAD_DIM = 64
N_KV = 2
HPG = 8
N_HEADS = N_KV * HPG
D_KV = N_KV * HEAD_DIM
BLK = 64
BLK_SHIFT = 6
TAIL_BLOCKS = 16
XPITCH = 136
XTAIL_PAGES = 8
N_SEL = 16
WINDOW = 512
CMP_HID = 2 * HEAD_DIM
CONV_W = 3
N_BUCKETS = 32
MAX_DIST = 128
EPS = 1e-6
NEG = -1e30
M_INIT = -5e29
VALID_THRESH = -1e29
LOG2E = math.log2(math.e)
QT = 128
KT = 128
FAR_GROUP = 4
ACC_ROWS = HEAD_DIM + 16
NCOL = N_HEADS * QT
GCOL = HPG * QT
LANES = 128
SAMPLE_KEY_TILE = 8192
V7X_VMEM_LIMIT = 56 * 1024 * 1024


def _dot(a, b):
    return jnp.dot(a, b, preferred_element_type=F32)


def _dot_nt(a, b):
    return lax.dot_general(a, b, (((1,), (1,)), ((), ())), preferred_element_type=F32)


def _sigmoid(x):
    return 1.0 / (1.0 + jnp.exp(-x))


def _silu(x):
    return x * _sigmoid(x)


def _ada_kernel(c_ref, w_ref, b_ref, o_ref):
    o_ref[...] = _dot(c_ref[...].astype(BF), w_ref[...].astype(BF)) + b_ref[...]


def _ada(c_all, ada_w, ada_b):
    rows, d = c_all.shape
    n = ada_w.shape[1]
    return pl.pallas_call(
        _ada_kernel,
        grid=(n // d,),
        in_specs=[pl.BlockSpec((rows, d), lambda j: (0, 0)),
                  pl.BlockSpec((d, d), lambda j: (0, j)),
                  pl.BlockSpec((1, d), lambda j: (0, j))],
        out_specs=pl.BlockSpec((rows, d), lambda j: (0, j)),
        out_shape=jax.ShapeDtypeStruct((rows, n), F32),
        name="ada_modulation",
    )(c_all, ada_w, ada_b.reshape(1, n))


def _proj_body(x_ref, sc_ref, sh_ref, ng_ref, wq_ref, wkv_ref, wg_ref, wc_ref, wz_ref, cw_ref):
    x = x_ref[...]
    ms = jnp.mean(x * x, axis=-1, keepdims=True)
    xn = x * lax.rsqrt(ms + EPS) * ng_ref[...]
    h = xn * sc_ref[0] + sh_ref[0]
    hb = h.astype(BF)
    q = _dot(hb, wq_ref[...]) * (HEAD_DIM ** -0.5 * LOG2E)
    kv = _dot(hb, wkv_ref[...])
    gates = _sigmoid(_dot(hb, wg_ref[...]))
    c3 = _dot(hb, wc_ref[...])
    dc = c3.shape[1] // 3
    hc, bc, cc = c3[:, :dc], c3[:, dc:2 * dc], c3[:, 2 * dc:]
    u = cc * hc
    z = _dot(hb, wz_ref[...])
    da = z.shape[1] - dc
    sza = _silu(z[:, :da])
    szc = _silu(z[:, da:])
    return q, kv, gates, u, bc, sza, szc


def _proj_prompt_kernel(x_ref, sc_ref, sh_ref, ng_ref, wq_ref, wkv_ref, wg_ref, wc_ref, wz_ref, cw_ref,
                        qT_ref, kcT_ref, vcT_ref, ksT_ref, vsT_ref, kwT_ref, vwT_ref, kc_ref, vc_ref,
                        kslc_ref, vslcT_ref, kwin_ref, vwinT_ref, gates_ref, yc_ref, sza_ref, cv_ref,
                        uext_ref, *, tiles_per_batch, tm):
    t = pl.program_id(0)
    q, kv, gates, u, bc, sza, szc = _proj_body(
        x_ref, sc_ref, sh_ref, ng_ref, wq_ref, wkv_ref, wg_ref, wc_ref, wz_ref, cw_ref)
    qT_ref[0] = q.T.astype(BF)
    kvT = [kv[:, n * D_KV:(n + 1) * D_KV].T for n in range(6)]
    for n, r in enumerate((kcT_ref, vcT_ref, ksT_ref, vsT_ref, kwT_ref, vwT_ref)):
        r[0] = kvT[n]
    kc_ref[...] = kv[:, 0:D_KV]
    vc_ref[...] = kv[:, D_KV:2 * D_KV]
    kslc_ref[0] = kv[:, 2 * D_KV:3 * D_KV].astype(BF)
    vslcT_ref[0] = kvT[3].astype(BF)
    kwin_ref[0] = kv[:, 4 * D_KV:5 * D_KV].astype(BF)
    vwinT_ref[0] = kvT[5].astype(BF)
    gates_ref[...] = gates

    first = (t % tiles_per_batch) == 0

    @pl.when(first)
    def _():
        uext_ref[0:8, :] = jnp.zeros((8, u.shape[1]), F32)

    @pl.when(jnp.logical_not(first))
    def _():
        uext_ref[0:8, :] = uext_ref[tm:tm + 8, :]

    uext_ref[8:8 + tm, :] = u
    y = (cw_ref[2:3, :] * u + cw_ref[1:2, :] * uext_ref[7:7 + tm, :]
         + cw_ref[0:1, :] * uext_ref[6:6 + tm, :])
    yc_ref[...] = (bc * y * szc).astype(BF)
    sza_ref[...] = sza.astype(BF)
    cv_ref[0] = u[tm - (CONV_W - 1):, :]


def _proj_sample_kernel(x_ref, sc_ref, sh_ref, ng_ref, wq_ref, wkv_ref, wg_ref, wc_ref, wz_ref, cw_ref,
                        st_ref, q_ref, kv_ref, gates_ref, yc_ref, sza_ref, cv_ref, uext_ref, *, nb, nt):
    q, kv, gates, u, bc, sza, szc = _proj_body(
        x_ref, sc_ref, sh_ref, ng_ref, wq_ref, wkv_ref, wg_ref, wc_ref, wz_ref, cw_ref)
    q_ref[...] = q
    kv_ref[...] = kv
    gates_ref[...] = gates
    ns = (CONV_W - 1) * nb
    uext_ref[0:ns, :] = st_ref[...]
    uext_ref[ns:ns + nt * nb, :] = u
    y = cw_ref[2:3, :] * u
    for k in range(CONV_W - 1):
        y = y + cw_ref[k:k + 1, :] * uext_ref[k * nb:k * nb + nt * nb, :]
    yc_ref[...] = (bc * y * szc).astype(BF)
    sza_ref[...] = sza.astype(BF)
    cv_ref[...] = uext_ref[nt * nb:nt * nb + ns, :]


def _const_spec(shape):
    nd = len(shape)
    return pl.BlockSpec(shape, lambda *_: (0,) * nd, pipeline_mode=pl.Buffered(1))


def _proj_prompt(x2d, sc, sh, ng, wts, cw, batch, seq, tm):
    tokens, d = x2d.shape
    tpb = seq // tm
    wq, wkv, wg, wc, wz = wts
    dc = wc.shape[1] // 3
    da = wz.shape[1] - dc
    tok_spec = lambda w: pl.BlockSpec((tm, w), lambda t: (t, 0))
    rowT_spec = lambda r: pl.BlockSpec((1, r, tm), lambda t: (t // tpb, 0, t % tpb))
    row_spec = lambda w: pl.BlockSpec((1, tm, w), lambda t: (t // tpb, t % tpb, 0))
    mod_spec = pl.BlockSpec((1, 1, d), lambda t: (t // tpb, 0, 0))
    out_shape = (
        [jax.ShapeDtypeStruct((batch, wq.shape[1], seq), BF)]
        + [jax.ShapeDtypeStruct((batch, D_KV, seq), F32)] * 6
        + [jax.ShapeDtypeStruct((tokens, D_KV), F32)] * 2
        + [jax.ShapeDtypeStruct((batch, seq, D_KV), BF), jax.ShapeDtypeStruct((batch, D_KV, seq), BF),
           jax.ShapeDtypeStruct((batch, seq, D_KV), BF), jax.ShapeDtypeStruct((batch, D_KV, seq), BF),
           jax.ShapeDtypeStruct((tokens, LANES), F32),
           jax.ShapeDtypeStruct((tokens, dc), BF), jax.ShapeDtypeStruct((tokens, da), BF),
           jax.ShapeDtypeStruct((batch, CONV_W - 1, dc), F32)])
    out_specs = (
        [rowT_spec(wq.shape[1])] + [rowT_spec(D_KV)] * 6 + [tok_spec(D_KV)] * 2
        + [row_spec(D_KV), rowT_spec(D_KV), row_spec(D_KV), rowT_spec(D_KV),
           tok_spec(LANES), tok_spec(dc), tok_spec(da),
           pl.BlockSpec((1, CONV_W - 1, dc), lambda t: (t // tpb, 0, 0))])
    return pl.pallas_call(
        functools.partial(_proj_prompt_kernel, tiles_per_batch=tpb, tm=tm),
        grid=(tokens // tm,),
        in_specs=[tok_spec(d), mod_spec, mod_spec, _const_spec((1, d)),
                  _const_spec(wq.shape), _const_spec(wkv.shape), _const_spec(wg.shape),
                  _const_spec(wc.shape), _const_spec(wz.shape), _const_spec(cw.shape)],
        out_specs=out_specs,
        out_shape=out_shape,
        scratch_shapes=[pltpu.VMEM((tm + 8, dc), F32)],
        compiler_params=pltpu.CompilerParams(dimension_semantics=("arbitrary",),
                                             vmem_limit_bytes=V7X_VMEM_LIMIT),
        name="proj_prompt",
    )(x2d, sc, sh, ng, wq, wkv, wg, wc, wz, cw)


def _proj_sample(x2d, sc, sh, ng, wts, cw, state2d, nb, nt):
    tokens, d = x2d.shape
    wq, wkv, wg, wc, wz = wts
    dc = wc.shape[1] // 3
    da = wz.shape[1] - dc
    ns = (CONV_W - 1) * nb
    full = lambda a: pl.BlockSpec(a.shape, lambda i: (0,) * a.ndim)
    ins = (x2d, sc, sh, ng, wq, wkv, wg, wc, wz, cw, state2d)
    out_shape = [jax.ShapeDtypeStruct((tokens, wq.shape[1]), F32),
                 jax.ShapeDtypeStruct((tokens, wkv.shape[1]), F32),
                 jax.ShapeDtypeStruct((tokens, LANES), F32),
                 jax.ShapeDtypeStruct((tokens, dc), BF), jax.ShapeDtypeStruct((tokens, da), BF),
                 jax.ShapeDtypeStruct((ns, dc), F32)]
    return pl.pallas_call(
        functools.partial(_proj_sample_kernel, nb=nb, nt=nt),
        grid=(1,),
        in_specs=[full(a) for a in ins],
        out_specs=[pl.BlockSpec(s.shape, lambda i: (0, 0)) for s in out_shape],
        out_shape=out_shape,
        scratch_shapes=[pltpu.VMEM((ns + tokens, dc), F32)],
        compiler_params=pltpu.CompilerParams(vmem_limit_bytes=V7X_VMEM_LIMIT),
        name="proj_sample",
    )(*ins)


def _compress(x_ref, nblk, pe_ref, w1_ref, w2_ref):
    def body(pp, acc):
        xp = jnp.concatenate([x_ref[pl.ds(2 * pp, nblk, stride=BLK), :],
                              x_ref[pl.ds(2 * pp + 1, nblk, stride=BLK), :]], axis=1) + pe_ref[pl.ds(pp, 1), :]
        return acc + _dot(xp.astype(BF), w1_ref[pp])
    acc = lax.fori_loop(0, BLK // 2, body, jnp.zeros((nblk, N_KV * CMP_HID), F32), unroll=2)
    return _dot(_silu(acc).astype(BF), w2_ref[...])


def _compress_prompt_kernel(kc_ref, vc_ref, pek_ref, w1k_ref, w2k_ref, pev_ref, w1v_ref, w2v_ref,
                            kcmp_ref, vcmpT_ref, *, nblk):
    kcmp_ref[0] = _compress(kc_ref.at[0], nblk, pek_ref, w1k_ref, w2k_ref).astype(BF)
    vcmpT_ref[0] = _compress(vc_ref.at[0], nblk, pev_ref, w1v_ref, w2v_ref).T.astype(BF)


def _compress_prompt(kc, vc, cw, batch, seq):
    nblk = seq // BLK
    row = pl.BlockSpec((1, seq, D_KV), lambda b: (b, 0, 0))
    return pl.pallas_call(
        functools.partial(_compress_prompt_kernel, nblk=nblk),
        grid=(batch,),
        in_specs=[row, row] + [_const_spec(a.shape) for a in cw],
        out_specs=[pl.BlockSpec((1, nblk, D_KV), lambda b: (b, 0, 0)),
                   pl.BlockSpec((1, D_KV, nblk), lambda b: (b, 0, 0))],
        out_shape=[jax.ShapeDtypeStruct((batch, nblk, D_KV), BF),
                   jax.ShapeDtypeStruct((batch, D_KV, nblk), BF)],
        compiler_params=pltpu.CompilerParams(vmem_limit_bytes=V7X_VMEM_LIMIT),
        name="compress_prompt",
    )(kc, vc, *cw)


SCORE_REMOVED = -3.0


def _select_topk(score, axis, k):
    idx = lax.broadcasted_iota(I32, score.shape, axis)
    past_end = jnp.int32(score.shape[axis])
    picked = jnp.zeros(score.shape, F32)
    s = score
    for _ in range(k):
        m = jnp.max(s, axis=axis, keepdims=True)
        first = jnp.min(jnp.where(s == m, idx, past_end), axis=axis, keepdims=True)
        hit = idx == first
        picked = jnp.where(hit, 1.0, picked)
        s = jnp.where(hit, SCORE_REMOVED, s)
    return picked > 0.5


def _attend(br, lhs, rhs_ref, krows, bias, vT, m_ref, acc_ref):
    s = _dot(lhs, rhs_ref[0:krows, :])
    if bias is not None:
        s = s + bias
    _consume(br, s, vT, m_ref, acc_ref)


SUB = 8


def _col_max(s):
    part = jnp.max(s.reshape(s.shape[0] // SUB, SUB, s.shape[1]), axis=0)
    return jnp.broadcast_to(jnp.max(part, axis=0, keepdims=True), part.shape)


def _rows_op(op, x, r):
    return op(x.reshape(x.shape[0] // SUB, SUB, x.shape[1]), r[None]).reshape(x.shape)


def _pv_update(br, pb, vT, scale_old, scale_new, acc_ref):
    ones = jnp.ones((ACC_ROWS - HEAD_DIM, pb.shape[0]), BF)
    for g in range(N_KV):
        gs = slice(g * GCOL, (g + 1) * GCOL)
        v_aug = jnp.concatenate([vT[g * HEAD_DIM:(g + 1) * HEAD_DIM, :], ones], axis=0)
        acc = acc_ref[br, g]
        if scale_old is not None:
            acc = _rows_op(jnp.multiply, acc, scale_old[:, gs])
        acc = acc + _dot(v_aug, pb[:, gs])
        if scale_new is not None:
            acc = _rows_op(jnp.multiply, acc, scale_new[:, gs])
        acc_ref[br, g] = acc


def _consume(br, s, vT, m_ref, acc_ref):
    m_old = m_ref[br]
    m_new = jnp.maximum(m_old, _col_max(s))
    alpha = jnp.exp2(m_old - m_new)
    pb = jnp.exp2(_rows_op(jnp.subtract, s, m_new)).astype(BF)
    m_ref[br] = m_new
    _pv_update(br, pb, vT, alpha, None, acc_ref)


def _nsa_prompt_kernel(qT_ref, gates_ref, kslc_ref, vslcT_ref, kwin_ref, vwinT_ref, kcmp_ref, vcmpT_ref,
                       cb_ref, sb_ref, w0_ref, o_ref,
                       rhs_ref, m_ref, acc_ref, s0_ref, s1_ref, *, nb):
    i = pl.program_id(1)

    for gh in range(N_HEADS):
        g = gh // HPG
        cols = slice(gh * QT, (gh + 1) * QT)
        rhs_ref[g * HEAD_DIM:(g + 1) * HEAD_DIM, cols] = qT_ref[0, gh * HEAD_DIM:(gh + 1) * HEAD_DIM, :]
        og = 1 - g
        rhs_ref[og * HEAD_DIM:(og + 1) * HEAD_DIM, cols] = jnp.zeros((HEAD_DIM, QT), BF)

    lg = _dot(kcmp_ref[0], rhs_ref[0:D_KV, :])
    rel = 2 * i - lax.broadcasted_iota(I32, (nb, NCOL), 0)
    bias = jnp.where(rel >= 3, 0.0, NEG)
    for rr in range(4):
        bias = jnp.where(rel == rr - 1, cb_ref[rr:rr + 1, :], bias)
    lgb = lg + bias
    mc = jnp.max(lgb, axis=0, keepdims=True)
    pc = jnp.exp2(lgb - mc)
    lc = jnp.sum(pc, axis=0, keepdims=True)
    pn = pc * jnp.where(mc > VALID_THRESH, 1.0 / lc, 0.0)
    pnb = pn.astype(BF)
    for g in range(N_KV):
        acc_ref[0, g, 0:HEAD_DIM, :] = _dot(vcmpT_ref[0, g * HEAD_DIM:(g + 1) * HEAD_DIM, :],
                                            pnb[:, g * GCOL:(g + 1) * GCOL])

    imp, valid = [], []
    for g in range(N_KV):
        a = pn[:, g * GCOL:g * GCOL + QT]
        for h in range(1, HPG):
            a = a + pn[:, g * GCOL + h * QT:g * GCOL + (h + 1) * QT]
        imp.append(a)
        valid.append(lgb[:, g * GCOL:g * GCOL + QT] > VALID_THRESH)
    imp = jnp.concatenate(imp, axis=1)
    valid = jnp.concatenate(valid, axis=1)
    jrow = lax.broadcasted_iota(I32, (nb, N_KV * QT), 0)
    tpos = i * QT + (lax.broadcasted_iota(I32, (nb, N_KV * QT), 1) & (QT - 1))
    score = jnp.where(valid, imp, jnp.where(jrow * BLK <= tpos, 2.0, -1.0))
    sel = _select_topk(score, 0, min(N_SEL, nb))
    selbias = jnp.where(jnp.logical_and(sel, score >= 0.0), 0.0, NEG).astype(BF)
    for gh in range(N_HEADS):
        g = gh // HPG
        rhs_ref[D_KV:D_KV + nb, gh * QT:(gh + 1) * QT] = selbias[:, g * QT:(g + 1) * QT]

    for br in (1, 2):
        m_ref[br] = jnp.full((SUB, NCOL), M_INIT, F32)
        acc_ref[br] = jnp.zeros((N_KV, ACC_ROWS, GCOL), F32)

    def sel_keys(kt, tk):
        k0 = pl.multiple_of(kt * KT, KT)
        k = kslc_ref[0, pl.ds(k0, tk), :]
        jj = lax.broadcasted_iota(I32, (tk, nb), 1)
        kk = lax.broadcasted_iota(I32, (tk, nb), 0)
        onehot = jnp.where(jj == (KT // BLK) * kt + jnp.right_shift(kk, BLK_SHIFT), 1.0, 0.0).astype(BF)
        return jnp.concatenate([k, onehot], axis=1), vslcT_ref[0, :, pl.ds(k0, tk)]

    def sel_tile(kt, tk, bias_tile):
        ka, vT = sel_keys(kt, tk)
        _attend(1, ka, rhs_ref, D_KV + nb, bias_tile, vT, m_ref, acc_ref)

    n_far = jnp.maximum(i - 1, 0)
    n_big = n_far // FAR_GROUP
    n_pairs = n_big // 2
    big = FAR_GROUP * KT

    def far_scores(j):
        ka, _ = sel_keys(j * FAR_GROUP, big)
        return _dot(ka, rhs_ref[...])

    def far_step(cur_ref, nxt_ref, j_cur, j_nxt):
        m_cur = m_ref[1]
        pb = jnp.exp2(_rows_op(jnp.subtract, cur_ref[...], m_cur)).astype(BF)
        s_nxt = far_scores(j_nxt)
        nxt_ref[...] = s_nxt
        m_nxt = jnp.maximum(m_cur, _col_max(s_nxt))
        vT = vslcT_ref[0, :, pl.ds(pl.multiple_of(j_cur * big, big), big)]
        _pv_update(1, pb, vT, None, jnp.exp2(m_cur - m_nxt), acc_ref)
        m_ref[1] = m_nxt

    @pl.when(n_pairs > 0)
    def _():
        s = far_scores(0)
        s0_ref[...] = s
        m_ref[1] = jnp.maximum(m_ref[1], _col_max(s))

    def far_pair(jp, c):
        @pl.when(jp < n_pairs)
        def _():
            far_step(s0_ref, s1_ref, 2 * jp, 2 * jp + 1)

        @pl.when(2 * jp + 1 < n_big)
        def _():
            far_step(s1_ref, s0_ref, 2 * jp + 1, 2 * jp + 2)
        return c
    lax.fori_loop(0, n_pairs - 1, far_pair, 0)

    @pl.when(n_pairs > 0)
    def _():
        far_step(s0_ref, s1_ref, 2 * n_pairs - 2, 2 * n_pairs - 1)

    @pl.when(n_big > 1)
    def _():
        j_last = 2 * n_pairs - 1
        pb = jnp.exp2(_rows_op(jnp.subtract, s1_ref[...], m_ref[1])).astype(BF)
        vT = vslcT_ref[0, :, pl.ds(pl.multiple_of(j_last * big, big), big)]
        _pv_update(1, pb, vT, None, None, acc_ref)

    @pl.when(n_big > 2 * n_pairs)
    def _():
        sel_tile((n_big - 1) * FAR_GROUP, big, None)

    def far_small(kt, c):
        sel_tile(kt, KT, None)
        return c
    lax.fori_loop(n_big * FAR_GROUP, n_far, far_small, 0)

    n_win = WINDOW // KT + 1

    @pl.when(i >= n_win - 1)
    def _():
        ka, vT_band = sel_keys(i - 1, 2 * KT)
        s_band = _dot(ka, rhs_ref[...]) + sb_ref[...]
        k0 = pl.multiple_of((i - (n_win - 1)) * KT, KT)
        s_win = _dot(kwin_ref[0, pl.ds(k0, n_win * KT), :], rhs_ref[0:D_KV, :]) + jnp.concatenate(
            [w0_ref[...], jnp.zeros(((n_win - 3) * KT, NCOL), F32), sb_ref[...]], axis=0)
        _consume(1, s_band, vT_band, m_ref, acc_ref)
        _consume(2, s_win, vwinT_ref[0, :, pl.ds(k0, n_win * KT)], m_ref, acc_ref)

    @pl.when(jnp.logical_and(i >= 1, i < n_win - 1))
    def _():
        sel_tile(i - 1, 2 * KT, sb_ref[...])

    @pl.when(i == 0)
    def _():
        sel_tile(0, KT, sb_ref[KT:2 * KT, :])

    @pl.when(i < n_win - 1)
    def _():
        for r in range(1, n_win):
            kt = i - (n_win - 1) + r

            @pl.when(kt >= 0)
            def _():
                k0 = pl.multiple_of(kt * KT, KT)
                r0 = (r - (n_win - 2)) * KT
                bias = sb_ref[r0:r0 + KT, :] if r >= n_win - 2 else None
                _attend(2, kwin_ref[0, pl.ds(k0, KT), :], rhs_ref, D_KV, bias,
                        vwinT_ref[0, :, pl.ds(k0, KT)], m_ref, acc_ref)

    gts = gates_ref[0, 0]
    for g in range(N_KV):
        cs = slice(g * GCOL, (g + 1) * GCOL)
        o = gts[0:1, cs] * acc_ref[0, g, 0:HEAD_DIM, :]
        for br in (1, 2):
            l = acc_ref[br, g, HEAD_DIM:HEAD_DIM + 1, :]
            o = o + (gts[br:br + 1, cs] * jnp.where(l > 0.0, 1.0 / l, 0.0)) * acc_ref[br, g, 0:HEAD_DIM, :]
        for hp in range(HPG // 2):
            pair = jnp.concatenate([o[:, (2 * hp) * QT:(2 * hp + 1) * QT],
                                    o[:, (2 * hp + 1) * QT:(2 * hp + 2) * QT]], axis=0)
            c0 = g * HPG * HEAD_DIM + hp * 2 * HEAD_DIM
            o_ref[0, :, c0:c0 + 2 * HEAD_DIM] = pair.T.astype(BF)


def _nsa_prompt(qT, gates, kslc, vslcT, kwin, vwinT, kcmp, vcmpT, tables, batch, seq):
    nb = seq // BLK
    nq = seq // QT
    cb, sb, w0 = tables
    dq = qT.shape[1]
    per_b = lambda shape: pl.BlockSpec((1,) + shape, lambda b, i: (b, 0, 0))
    return pl.pallas_call(
        functools.partial(_nsa_prompt_kernel, nb=nb),
        grid=(batch, nq),
        in_specs=[pl.BlockSpec((1, dq, QT), lambda b, i: (b, 0, i)),
                  pl.BlockSpec((1, 1, 8, NCOL), lambda b, i: (b, i, 0, 0)),
                  per_b((seq, D_KV)), per_b((D_KV, seq)), per_b((seq, D_KV)), per_b((D_KV, seq)),
                  per_b((nb, D_KV)), per_b((D_KV, nb)),
                  _const_spec(cb.shape), _const_spec(sb.shape), _const_spec(w0.shape)],
        out_specs=pl.BlockSpec((1, QT, dq), lambda b, i: (b, i, 0)),
        out_shape=jax.ShapeDtypeStruct((batch, seq, dq), BF),
        scratch_shapes=[pltpu.VMEM((D_KV + nb, NCOL), BF),
                        pltpu.VMEM((3, SUB, NCOL), F32),
                        pltpu.VMEM((3, N_KV, ACC_ROWS, GCOL), F32),
                        pltpu.VMEM((FAR_GROUP * KT, NCOL), F32),
                        pltpu.VMEM((FAR_GROUP * KT, NCOL), F32)],
        compiler_params=pltpu.CompilerParams(dimension_semantics=("arbitrary", "arbitrary"),
                                             vmem_limit_bytes=V7X_VMEM_LIMIT),
        name="nsa_prompt",
    )(qT, gates, kslc, vslcT, kwin, vwinT, kcmp, vcmpT, cb, sb, w0)


def _merge_kernel(x_ref, o_ref, sza_ref, yc_ref, gate_ref, woa_ref, woc_ref, fg_ref, y_ref):
    og = (o_ref[...].astype(F32) * sza_ref[...].astype(F32)).astype(BF)
    acc = _dot(og, woa_ref[...]) + _dot(yc_ref[...], woc_ref[...])
    xo = x_ref[...] + gate_ref[0] * acc
    ms = jnp.mean(xo * xo, axis=-1, keepdims=True)
    y_ref[...] = xo * lax.rsqrt(ms + EPS) * fg_ref[...]


def _merge(x2d, o2d, sza, yc, gate, woa, woc, fg, tm, rows_per_gate):
    tokens, d = x2d.shape
    tok = lambda w: pl.BlockSpec((tm, w), lambda t: (t, 0))
    gr = gate.shape[1]
    return pl.pallas_call(
        _merge_kernel,
        grid=(tokens // tm,),
        in_specs=[tok(d), tok(o2d.shape[1]), tok(sza.shape[1]), tok(yc.shape[1]),
                  pl.BlockSpec((1, gr, d), lambda t: ((t * tm) // rows_per_gate, 0, 0)),
                  _const_spec(woa.shape), _const_spec(woc.shape), _const_spec(fg.shape)],
        out_specs=tok(d),
        out_shape=jax.ShapeDtypeStruct((tokens, d), F32),
        compiler_params=pltpu.CompilerParams(dimension_semantics=("arbitrary",),
                                             vmem_limit_bytes=V7X_VMEM_LIMIT),
        name="merge_out",
    )(x2d, o2d, sza, yc, gate, woa, woc, fg)


def _softmax_rows(parts):
    m = parts[0].max(axis=1, keepdims=True)
    for s in parts[1:]:
        m = jnp.maximum(m, s.max(axis=1, keepdims=True))
    ps = [jnp.exp2(s - m) for s in parts]
    l = ps[0].sum(axis=1, keepdims=True)
    for p in ps[1:]:
        l = l + p.sum(axis=1, keepdims=True)
    inv = jnp.where(jnp.logical_and(m > VALID_THRESH, l > 0.0), 1.0 / l, 0.0)
    return ps, inv


def _nsa_sample_kernel(pt_ref, ckc_ref, cvc_ref, cks_ref, cvs_ref,
                       qbd_ref, gates_ref, tails_ref, tailsT_ref, bkwT_ref, bvwT_ref,
                       cbs_ref, sbl_ref, sbt_ref, wbl_ref, wbt_ref, elocT_ref,
                       pek_ref, w1k_ref, w2k_ref, pev_ref, w1v_ref, w2v_ref,
                       o_ref, skwT_ref, svwT_ref,
                       buf_ref, sem, xbuf_ref, cmpf_ref, kcmp_ref, vcmp_ref, sbuf_ref, pbuf_ref,
                       *, n_pages, page, n_dec):
    b = pl.program_id(0)
    nbatch = pl.num_programs(0)
    past = n_pages * page
    nbp = past // BLK
    nbc = kcmp_ref.shape[0]
    n_tiles = past // SAMPLE_KEY_TILE
    nrow = qbd_ref.shape[1]
    caches = (ckc_ref, cvc_ref, cks_ref, cvs_ref)

    def page_copy(cache_ref, pg, j, slot):
        return pltpu.make_async_copy(cache_ref.at[pg],
                                     buf_ref.at[slot, :, pl.ds(pl.multiple_of(j * page, page), page)],
                                     sem.at[slot])

    def start_fetch(cache_ref, bb, slot):
        def body(j, c):
            page_copy(cache_ref, pt_ref[bb, j], j, slot).start()
            return c
        lax.fori_loop(0, n_pages, body, 0, unroll=4)

    def wait_fetch(cache_ref, slot):
        def body(j, c):
            page_copy(cache_ref, 0, j, slot).wait()
            return c
        lax.fori_loop(0, n_pages, body, 0)

    def stage(c):
        if c + 1 < len(caches):
            start_fetch(caches[c + 1], b, (c + 1) % 2)
        else:
            @pl.when(b + 1 < nbatch)
            def _():
                start_fetch(caches[0], b + 1, 0)
        wait_fetch(caches[c], c % 2)
        return buf_ref.at[c % 2]

    @pl.when(b == 0)
    def _():
        start_fetch(caches[0], 0, 0)

    qbd = qbd_ref[0]

    def compress_cache(c, slot_ref, pe_ref, w1_ref, w2_ref, dst_ref):
        def xpose(j, carry):
            src = slot_ref[:, pl.ds(pl.multiple_of(j * page, page), page)]
            xbuf_ref[pl.ds(pl.multiple_of(j * XPITCH, 8), page), :] = src.T
            return carry
        lax.fori_loop(0, n_pages, xpose, 0, unroll=8)
        n_extra = XTAIL_PAGES * XPITCH
        xbuf_ref[pl.ds(n_pages * XPITCH, n_extra), :] = jnp.zeros((n_extra, D_KV), F32)
        xbuf_ref[pl.ds(n_pages * XPITCH, 8), :] = tails_ref[0, c]
        n_even = n_pages + XTAIL_PAGES

        def rows(r0, n):
            return jnp.concatenate([xbuf_ref[pl.ds(r0, n, stride=XPITCH), :],
                                    xbuf_ref[pl.ds(r0 + 1, n, stride=XPITCH), :]], axis=1)

        def body(pp, acc):
            acc_e, acc_o = acc
            pe = pe_ref[pl.ds(pp, 1), :]
            w = w1_ref[pp]
            xe = rows(2 * pp, n_even) + pe
            xo = rows(2 * pp + BLK, n_pages) + pe
            return acc_e + _dot(xe.astype(BF), w), acc_o + _dot(xo.astype(BF), w)
        acc_e, acc_o = lax.fori_loop(
            0, BLK // 2, body,
            (jnp.zeros((n_even, N_KV * CMP_HID), F32), jnp.zeros((n_pages, N_KV * CMP_HID), F32)), unroll=2)
        out_e = _dot(_silu(acc_e).astype(BF), w2_ref[...])
        out_o = _dot(_silu(acc_o).astype(BF), w2_ref[...])
        cmpf_ref[pl.ds(0, n_pages, stride=2), :] = out_e[0:n_pages]
        cmpf_ref[pl.ds(1, n_pages, stride=2), :] = out_o
        cmpf_ref[nbp:nbp + XTAIL_PAGES, :] = out_e[n_pages:n_even]
        cmpf_ref[nbp + XTAIL_PAGES:nbp + TAIL_BLOCKS, :] = jnp.zeros((TAIL_BLOCKS - XTAIL_PAGES, D_KV), F32)
        dst_ref[0:nbp + TAIL_BLOCKS, :] = cmpf_ref[...].astype(BF)
        dst_ref[nbp + TAIL_BLOCKS:nbc, :] = jnp.zeros((nbc - nbp - TAIL_BLOCKS, D_KV), BF)

    compress_cache(0, stage(0), pek_ref, w1k_ref, w2k_ref, kcmp_ref)
    compress_cache(1, stage(1), pev_ref, w1v_ref, w2v_ref, vcmp_ref)

    lgc = _dot_nt(qbd, kcmp_ref[...]) + cbs_ref[...]
    (pc,), invc = _softmax_rows([lgc])
    pn = pc * invc
    o_cmp = _dot(pn.astype(BF), vcmp_ref[...])
    imp = jnp.sum(pn.reshape(nrow // HPG, HPG, nbc), axis=1)
    jcol = lax.broadcasted_iota(I32, imp.shape, 1)
    score = jnp.where(jcol < nbp, imp, jnp.where(jcol == nbp, 2.0, -1.0))
    sel = _select_topk(score, 1, N_SEL)
    selbias = jnp.where(jnp.logical_and(sel, score >= 0.0), 0.0, NEG)
    selb = jnp.broadcast_to(selbias[:, None, :], (nrow // HPG, HPG, nbc)).reshape(nrow, nbc).astype(BF)

    kslot = stage(2)
    tile = lambda t: slice(t * SAMPLE_KEY_TILE, (t + 1) * SAMPLE_KEY_TILE)
    for t in range(n_tiles):
        sbuf_ref[t] = (_dot(qbd, kslot[:, tile(t)].astype(BF))
                       + _dot(selb[:, t * LANES:(t + 1) * LANES], elocT_ref[...]))
    last = slice(SAMPLE_KEY_TILE - LANES, SAMPLE_KEY_TILE)
    sbuf_ref[n_tiles - 1, :, last] = sbuf_ref[n_tiles - 1, :, last] + sbl_ref[...]
    s_tail = (_dot(qbd, tailsT_ref[0, 0].astype(BF))
              + _dot(selb[:, nbp:nbp + LANES], elocT_ref[:, 0:LANES]) + sbt_ref[...])
    ps, inv_sel = _softmax_rows([sbuf_ref[t] for t in range(n_tiles)] + [s_tail])
    for t in range(n_tiles):
        pbuf_ref[t] = ps[t].astype(BF)
    p_tail = ps[n_tiles].astype(BF)

    vslot = stage(3)
    o_sel = _dot_nt(p_tail, tailsT_ref[0, 1].astype(BF))
    for t in range(n_tiles):
        o_sel = o_sel + _dot_nt(pbuf_ref[t], vslot[:, tile(t)].astype(BF))
    o_sel = o_sel * inv_sel

    s_w = _dot(qbd, bkwT_ref[0].astype(BF)) + wbl_ref[...]
    s_wt = _dot(qbd, tailsT_ref[0, 2].astype(BF)) + wbt_ref[...]
    (p_w, p_wt), inv_w = _softmax_rows([s_w, s_wt])
    o_win = (_dot_nt(p_w.astype(BF), bvwT_ref[0].astype(BF))
             + _dot_nt(p_wt.astype(BF), tailsT_ref[0, 3].astype(BF))) * inv_w

    o_ref[0] = gates_ref[0, 0] * o_cmp + gates_ref[0, 1] * o_sel + gates_ref[0, 2] * o_win

    nbuf = bkwT_ref.shape[2]
    lane = lax.broadcasted_iota(I32, (D_KV, LANES), 1)
    for n, src, dst in ((2, bkwT_ref, skwT_ref), (3, bvwT_ref, svwT_ref)):
        rolled = pltpu.roll(src[0], nbuf - n_dec, axis=1)
        new = pltpu.roll(tailsT_ref[0, n], LANES - n_dec, axis=1)
        dst[0, :, 0:nbuf - LANES] = rolled[:, 0:nbuf - LANES]
        dst[0, :, nbuf - LANES:nbuf] = jnp.where(lane >= LANES - n_dec, new, rolled[:, nbuf - LANES:nbuf])


def _nsa_sample(page_table, caches, qbd, gates3, tails, tailsT, bkwT, bvwT, tables, cw, n_dec):
    nbatch, n_pages = page_table.shape
    page = caches[0].shape[2]
    past = n_pages * page
    cbs, sbl, sbt, wbl, wbt, elocT = tables
    nbc = cbs.shape[1]
    nrow = qbd.shape[1]
    nbuf = bkwT.shape[2]
    nbp = past // BLK
    n_tiles = past // SAMPLE_KEY_TILE
    assert past % SAMPLE_KEY_TILE == 0 and nbp % LANES == 0 and page == 2 * BLK and nbuf % LANES == 0
    per_b = lambda a: pl.BlockSpec((1,) + a.shape[1:], lambda b, pt: (b,) + (0,) * (a.ndim - 1))
    const = lambda a: pl.BlockSpec(a.shape, lambda b, pt: (0,) * a.ndim, pipeline_mode=pl.Buffered(1))
    hbm = pl.BlockSpec(memory_space=pl.ANY)
    grid_spec = pltpu.PrefetchScalarGridSpec(
        num_scalar_prefetch=1,
        grid=(nbatch,),
        in_specs=[hbm] * 4 + [per_b(a) for a in (qbd, gates3, tails, tailsT, bkwT, bvwT)]
                 + [const(a) for a in tables] + [const(a) for a in cw],
        out_specs=[pl.BlockSpec((1, nrow, D_KV), lambda b, pt: (b, 0, 0)),
                   pl.BlockSpec((1, D_KV, nbuf), lambda b, pt: (b, 0, 0)),
                   pl.BlockSpec((1, D_KV, nbuf), lambda b, pt: (b, 0, 0))],
        scratch_shapes=[pltpu.VMEM((2, D_KV, past), F32),
                        pltpu.SemaphoreType.DMA((2,)),
                        pltpu.VMEM(((n_pages + XTAIL_PAGES) * XPITCH, D_KV), F32),
                        pltpu.VMEM((nbp + TAIL_BLOCKS, D_KV), F32),
                        pltpu.VMEM((nbc, D_KV), BF),
                        pltpu.VMEM((nbc, D_KV), BF),
                        pltpu.VMEM((n_tiles, nrow, SAMPLE_KEY_TILE), F32),
                        pltpu.VMEM((n_tiles, nrow, SAMPLE_KEY_TILE), BF)])
    return pl.pallas_call(
        functools.partial(_nsa_sample_kernel, n_pages=n_pages, page=page, n_dec=n_dec),
        grid_spec=grid_spec,
        out_shape=[jax.ShapeDtypeStruct((nbatch, nrow, D_KV), F32),
                   jax.ShapeDtypeStruct((nbatch, D_KV, nbuf), F32),
                   jax.ShapeDtypeStruct((nbatch, D_KV, nbuf), F32)],
        compiler_params=pltpu.CompilerParams(dimension_semantics=("arbitrary",),
                                             vmem_limit_bytes=V7X_VMEM_LIMIT),
        name="nsa_sample",
    )(page_table, *caches, qbd, gates3, tails, tailsT, bkwT, bvwT, *tables, *cw)


def _t5_bucket_np(n):
    d = np.maximum(np.arange(n), 0)
    max_exact = N_BUCKETS // 2
    nf = np.maximum(d, max_exact).astype(np.float32)
    large = max_exact + (np.log(nf / np.float32(max_exact)) / np.float32(math.log(MAX_DIST / max_exact))
                         * np.float32(N_BUCKETS - max_exact)).astype(np.int32)
    return np.where(d < max_exact, d, np.minimum(large, N_BUCKETS - 1))


N_NEAR = 256
_BUCKETS = _t5_bucket_np(N_NEAR)
assert np.all(_BUCKETS[QT + 1:] == N_BUCKETS - 1)


def _shifted_bias(rel_bias):
    return (rel_bias[_BUCKETS[:N_NEAR]] - rel_bias[N_BUCKETS - 1][None, :]) * LOG2E


def _bias_cols(tp, dist):
    val = tp[np.clip(dist, 0, N_NEAR - 1)]
    val = jnp.where((dist >= 0)[:, :, None], val, NEG)
    return jnp.transpose(val, (0, 2, 1)).reshape(dist.shape[0], -1)


def _prompt_tables(rel_bias, nb):
    tp = _shifted_bias(rel_bias)
    q = np.arange(QT)[None, :]
    r = np.arange(-1, 7)[:, None]
    d_cmp = np.where(r <= 2, BLK * r - (BLK - 1) + q, -1)
    cb = _bias_cols(tp, d_cmp)
    n_neg, rows = 2 * KT, 2 * KT
    w = jnp.concatenate([jnp.full((N_HEADS, n_neg), NEG, F32), tp.T, jnp.zeros((N_HEADS, 1), F32)], axis=1)
    period = w.shape[1]
    skew = jnp.tile(w, (1, rows))[:, :rows * (period - 1)].reshape(N_HEADS, rows, period - 1)
    c0 = n_neg + KT
    sb = jnp.transpose(skew[:, :, c0:c0 + QT], (1, 0, 2)).reshape(rows, N_HEADS * QT)
    kk = np.arange(KT)[:, None]
    w0 = jnp.tile(jnp.asarray(np.where(kk >= q, 0.0, NEG), F32), (1, N_HEADS))
    return cb, sb, w0


def _sample_rows_bias(tp, dist):
    val = tp[np.clip(dist, 0, N_NEAR - 1)]
    val = jnp.where((dist >= 0)[:, :, None], val, NEG)
    val = val.reshape(dist.shape[0], dist.shape[1], N_KV, HPG)
    return jnp.transpose(val, (2, 0, 3, 1)).reshape(-1, dist.shape[1])


def _sample_tables(rel_bias, past, n_dec, nbuf, nbc):
    tp = _shifted_bias(rel_bias)
    q = np.arange(n_dec)[:, None]
    nbp = past // BLK
    j = np.arange(nbc)[None, :]
    cbs = _sample_rows_bias(tp, np.where(j < nbp, past + q - (BLK * j + BLK - 1), -1))
    kk = np.arange(LANES)[None, :]
    sbl = _sample_rows_bias(tp, q + LANES - kk)
    sbt = _sample_rows_bias(tp, q - kk)
    r = np.arange(nbuf)[None, :]
    d_w = nbuf + q - r
    wbl = _sample_rows_bias(tp, np.where(d_w <= WINDOW, d_w, -1))
    wbt = sbt
    kk = np.arange(SAMPLE_KEY_TILE)[None, :]
    elocT = jnp.asarray(kk // BLK == np.arange(LANES)[:, None], BF)
    return cbs, sbl, sbt, wbl, wbt, elocT


def _compress_weights(pe, w1, w2):
    pe2 = jnp.tile(pe, (1, N_KV)).reshape(BLK // 2, 2 * D_KV)
    w1r = w1.reshape(BLK, HEAD_DIM, CMP_HID)
    eye = jnp.eye(N_KV, dtype=w1.dtype)
    w1bd = jnp.einsum("pdh,gk->pgdkh", w1r, eye).reshape(BLK // 2, 2 * D_KV, N_KV * CMP_HID).astype(BF)
    w2bd = jnp.einsum("hd,gk->ghkd", w2, eye).reshape(N_KV * CMP_HID, D_KV).astype(BF)
    return pe2, w1bd, w2bd


def _split_w_in(w_in, d_attn, d_conv, d_mix):
    sizes = (d_attn,) + (D_KV,) * 6 + (3 * N_HEADS,) + (d_conv,) * 3 + (d_mix,)
    offs = np.cumsum((0,) + sizes)
    wq = w_in[:, offs[0]:offs[1]]
    wkv = w_in[:, offs[1]:offs[7]]
    wg = jnp.pad(w_in[:, offs[7]:offs[8]], ((0, 0), (0, LANES - 3 * N_HEADS)))
    wc = w_in[:, offs[8]:offs[11]]
    wz = w_in[:, offs[11]:offs[12]]
    return tuple(w.astype(BF) for w in (wq, wkv, wg, wc, wz))


def kernel(x_prompt, x_sample, cache_k_cmp, cache_v_cmp, cache_k_slc, cache_v_slc, cache_k_win,
           cache_v_win, state_conv, page_table, c_prompt, c_sample, ada_w, ada_b, norm_g, w_in,
           cmp_pe_k, cmp_w1_k, cmp_w2_k, cmp_pe_v, cmp_w1_v, cmp_w2_v, conv_w, w_out, rel_bias, final_g):
    depth = ada_w.shape[0]
    assert depth == 1
    batch, seq, d = x_prompt.shape
    nbatch, n_dec, _ = x_sample.shape
    d_attn = N_HEADS * HEAD_DIM
    d_conv = conv_w.shape[2]
    d_mix = d_attn + d_conv
    n_pages = page_table.shape[1]
    page = cache_k_cmp.shape[2]
    past = n_pages * page
    nbuf = cache_k_win.shape[2]
    tm = 512
    assert seq % tm == 0 and seq >= WINDOW and nbuf == WINDOW

    wts = _split_w_in(w_in[0], d_attn, d_conv, d_mix)
    cwk = _compress_weights(cmp_pe_k[0], cmp_w1_k[0], cmp_w2_k[0])
    cwv = _compress_weights(cmp_pe_v[0], cmp_w1_v[0], cmp_w2_v[0])
    woa = w_out[0, :d_attn].astype(BF)
    woc = w_out[0, d_attn:].astype(BF)
    ng = norm_g[0].reshape(1, d)
    fg = final_g.reshape(1, d)
    cw = conv_w[0]

    n_c = batch + nbatch
    c_all = jnp.pad(jnp.concatenate([c_prompt, c_sample], axis=0), ((0, (-n_c) % 8), (0, 0)))
    mod = _ada(c_all, ada_w[0], ada_b[0])
    shift, scale, gate = mod[:, :d], mod[:, d:2 * d], mod[:, 2 * d:]

    x2d = x_prompt.reshape(batch * seq, d)
    sc_p = (1.0 + scale[:batch]).reshape(batch, 1, d)
    sh_p = shift[:batch].reshape(batch, 1, d)
    (qT, kcT, vcT, ksT, vsT, kwT, vwT, kc, vc, kslc, vslcT, kwin, vwinT, gates_tok, yc, sza, cv_p) = _proj_prompt(
        x2d, sc_p, sh_p, ng, wts, cw, batch, seq, tm)
    kcmp, vcmpT = _compress_prompt(kc.reshape(batch, seq, D_KV), vc.reshape(batch, seq, D_KV),
                                   cwk + cwv, batch, seq)
    nq = seq // QT
    gates = gates_tok[:, :3 * N_HEADS].reshape(batch, nq, QT, 3, N_HEADS)
    gates = jnp.transpose(gates, (0, 1, 3, 4, 2)).reshape(batch, nq, 3, NCOL)
    gates = jnp.pad(gates, ((0, 0), (0, 0), (0, 5), (0, 0)))
    o_p = _nsa_prompt(qT, gates, kslc, vslcT, kwin, vwinT, kcmp, vcmpT,
                      _prompt_tables(rel_bias, seq // BLK), batch, seq)
    y_p = _merge(x2d, o_p.reshape(batch * seq, d_attn), sza, yc, gate[:batch].reshape(batch, 1, d),
                 woa, woc, fg, tm, seq)
    y_prompt = y_p.reshape(batch, seq, d)
    heads = lambda aT: jnp.transpose(aT.reshape(aT.shape[0], N_KV, HEAD_DIM, aT.shape[2]), (0, 3, 1, 2))[None]
    p_states = (heads(kcT), heads(vcT), heads(ksT), heads(vsT),
                heads(kwT[:, :, seq - WINDOW:]), heads(vwT[:, :, seq - WINDOW:]), cv_p[None])

    tok = n_dec * nbatch
    xs2d = jnp.transpose(x_sample, (1, 0, 2)).reshape(tok, d)
    rep = lambda a: jnp.tile(a, (n_dec, 1)).reshape(1, tok, d)
    st2d = jnp.transpose(state_conv[0], (1, 0, 2)).reshape((CONV_W - 1) * nbatch, d_conv)
    q_s, kv_s, gates_s, yc_s, sza_s, cv_s = _proj_sample(
        xs2d, rep(1.0 + scale[batch:n_c]), rep(shift[batch:n_c]), ng, wts, cw, st2d, nbatch, n_dec)

    q5 = jnp.transpose(q_s.reshape(n_dec, nbatch, N_KV, HPG, HEAD_DIM), (1, 2, 0, 3, 4))
    eye = jnp.eye(N_KV, dtype=F32)
    qbd = (q5[:, :, :, :, None, :] * eye[None, :, None, None, :, None]).reshape(
        nbatch, N_KV * n_dec * HPG, D_KV).astype(BF)
    g5 = gates_s[:, :3 * N_HEADS].reshape(n_dec, nbatch, 3, N_KV, HPG)
    g5 = jnp.transpose(g5, (1, 2, 3, 0, 4)).reshape(nbatch, 3, N_KV * n_dec * HPG, 1)
    gates3 = jnp.broadcast_to(g5, (nbatch, 3, N_KV * n_dec * HPG, D_KV))
    kv6 = jnp.transpose(kv_s.reshape(n_dec, nbatch, 6, D_KV), (1, 2, 0, 3))
    tails = jnp.pad(kv6[:, 0:2], ((0, 0), (0, 0), (0, 8 - n_dec), (0, 0)))
    tailsT = jnp.pad(jnp.transpose(kv6[:, 2:6], (0, 1, 3, 2)), ((0, 0), (0, 0), (0, 0), (0, LANES - n_dec)))
    nbc = -(-(past // BLK + TAIL_BLOCKS) // LANES) * LANES
    posT = lambda c: jnp.transpose(c, (0, 2, 3, 1)).reshape(c.shape[0], D_KV, c.shape[1])
    caches = tuple(posT(c[0]) for c in (cache_k_cmp, cache_v_cmp, cache_k_slc, cache_v_slc))
    o_s, s_kwT, s_vwT = _nsa_sample(
        page_table, caches, qbd, gates3, tails, tailsT, posT(cache_k_win[0]), posT(cache_v_win[0]),
        _sample_tables(rel_bias, past, n_dec, nbuf, nbc), cwk + cwv, n_dec)
    o6 = o_s.reshape(nbatch, N_KV, n_dec, HPG, N_KV, HEAD_DIM)
    o_diag = jnp.stack([o6[:, g, :, :, g, :] for g in range(N_KV)], axis=1)
    o_tok = jnp.transpose(o_diag, (2, 0, 1, 3, 4)).reshape(tok, d_attn).astype(BF)
    y_s = _merge(xs2d, o_tok, sza_s, yc_s, rep(gate[batch:n_c]), woa, woc, fg, tok, tok)
    y_sample = jnp.transpose(y_s.reshape(n_dec, nbatch, d), (1, 0, 2))
    new = lambda n: kv6[:, n].reshape(1, nbatch, n_dec, N_KV, HEAD_DIM)
    cv_out = jnp.transpose(cv_s.reshape(CONV_W - 1, nbatch, d_conv), (1, 0, 2))[None]
    s_states = (new(0), new(1), new(2), new(3), heads(s_kwT), heads(s_vwT), cv_out)

    return (y_prompt, y_sample) + p_states + s_states
```

```python
import functools
import math

import numpy as np
import jax
import jax.numpy as jnp
from jax import lax
from jax.experimental import pallas as pl
from jax.experimental.pallas import tpu as pltpu

BF = jnp.bfloat16
F32 = jnp.float32
I32 = jnp.int32

HEAD_DIM = 64
N_KV = 2
HPG = 8
N_HEADS = N_KV * HPG
D_KV = N_KV * HEAD_DIM
BLK = 64
BLK_SHIFT = 6
TAIL_BLOCKS = 16
XPITCH = 136
XTAIL_PAGES = 8
N_SEL = 16
WINDOW = 512
CMP_HID = 2 * HEAD_DIM
CONV_W = 3
N_BUCKETS = 32
MAX_DIST = 128
EPS = 1e-6
NEG = -1e30
M_INIT = -5e29
VALID_THRESH = -1e29
LOG2E = math.log2(math.e)
QT = 128
KT = 128
FAR_GROUP = 4
ACC_ROWS = HEAD_DIM + 16
NCOL = N_HEADS * QT
GCOL = HPG * QT
LANES = 128
SAMPLE_KEY_TILE = 8192
V7X_VMEM_LIMIT = 56 * 1024 * 1024


def _dot(a, b):
    return jnp.dot(a, b, preferred_element_type=F32)


def _dot_nt(a, b):
    return lax.dot_general(a, b, (((1,), (1,)), ((), ())), preferred_element_type=F32)


def _sigmoid(x):
    return 1.0 / (1.0 + jnp.exp(-x))


def _silu(x):
    return x * _sigmoid(x)


def _ada_kernel(c_ref, w_ref, b_ref, o_ref):
    o_ref[...] = _dot(c_ref[...].astype(BF), w_ref[...].astype(BF)) + b_ref[...]


def _ada(c_all, ada_w, ada_b):
    rows, d = c_all.shape
    n = ada_w.shape[1]
    return pl.pallas_call(
        _ada_kernel,
        grid=(n // d,),
        in_specs=[pl.BlockSpec((rows, d), lambda j: (0, 0)),
                  pl.BlockSpec((d, d), lambda j: (0, j)),
                  pl.BlockSpec((1, d), lambda j: (0, j))],
        out_specs=pl.BlockSpec((rows, d), lambda j: (0, j)),
        out_shape=jax.ShapeDtypeStruct((rows, n), F32),
        name="ada_modulation",
    )(c_all, ada_w, ada_b.reshape(1, n))


def _proj_body(x_ref, sc_ref, sh_ref, ng_ref, wq_ref, wkv_ref, wg_ref, wc_ref, wz_ref, cw_ref):
    x = x_ref[...]
    ms = jnp.mean(x * x, axis=-1, keepdims=True)
    xn = x * lax.rsqrt(ms + EPS) * ng_ref[...]
    h = xn * sc_ref[0] + sh_ref[0]
    hb = h.astype(BF)
    q = _dot(hb, wq_ref[...]) * (HEAD_DIM ** -0.5 * LOG2E)
    kv = _dot(hb, wkv_ref[...])
    gates = _sigmoid(_dot(hb, wg_ref[...]))
    c3 = _dot(hb, wc_ref[...])
    dc = c3.shape[1] // 3
    hc, bc, cc = c3[:, :dc], c3[:, dc:2 * dc], c3[:, 2 * dc:]
    u = cc * hc
    z = _dot(hb, wz_ref[...])
    da = z.shape[1] - dc
    sza = _silu(z[:, :da])
    szc = _silu(z[:, da:])
    return q, kv, gates, u, bc, sza, szc


def _proj_prompt_kernel(x_ref, sc_ref, sh_ref, ng_ref, wq_ref, wkv_ref, wg_ref, wc_ref, wz_ref, cw_ref,
                        qT_ref, kcT_ref, vcT_ref, ksT_ref, vsT_ref, kwT_ref, vwT_ref, kc_ref, vc_ref,
                        kslc_ref, vslcT_ref, kwin_ref, vwinT_ref, gates_ref, yc_ref, sza_ref, cv_ref,
                        uext_ref, *, tiles_per_batch, tm):
    t = pl.program_id(0)
    q, kv, gates, u, bc, sza, szc = _proj_body(
        x_ref, sc_ref, sh_ref, ng_ref, wq_ref, wkv_ref, wg_ref, wc_ref, wz_ref, cw_ref)
    qT_ref[0] = q.T.astype(BF)
    kvT = [kv[:, n * D_KV:(n + 1) * D_KV].T for n in range(6)]
    for n, r in enumerate((kcT_ref, vcT_ref, ksT_ref, vsT_ref, kwT_ref, vwT_ref)):
        r[0] = kvT[n]
    kc_ref[...] = kv[:, 0:D_KV]
    vc_ref[...] = kv[:, D_KV:2 * D_KV]
    kslc_ref[0] = kv[:, 2 * D_KV:3 * D_KV].astype(BF)
    vslcT_ref[0] = kvT[3].astype(BF)
    kwin_ref[0] = kv[:, 4 * D_KV:5 * D_KV].astype(BF)
    vwinT_ref[0] = kvT[5].astype(BF)
    gates_ref[...] = gates

    first = (t % tiles_per_batch) == 0

    @pl.when(first)
    def _():
        uext_ref[0:8, :] = jnp.zeros((8, u.shape[1]), F32)

    @pl.when(jnp.logical_not(first))
    def _():
        uext_ref[0:8, :] = uext_ref[tm:tm + 8, :]

    uext_ref[8:8 + tm, :] = u
    y = (cw_ref[2:3, :] * u + cw_ref[1:2, :] * uext_ref[7:7 + tm, :]
         + cw_ref[0:1, :] * uext_ref[6:6 + tm, :])
    yc_ref[...] = (bc * y * szc).astype(BF)
    sza_ref[...] = sza.astype(BF)
    cv_ref[0] = u[tm - (CONV_W - 1):, :]


def _proj_sample_kernel(x_ref, sc_ref, sh_ref, ng_ref, wq_ref, wkv_ref, wg_ref, wc_ref, wz_ref, cw_ref,
                        st_ref, q_ref, kv_ref, gates_ref, yc_ref, sza_ref, cv_ref, uext_ref, *, nb, nt):
    q, kv, gates, u, bc, sza, szc = _proj_body(
        x_ref, sc_ref, sh_ref, ng_ref, wq_ref, wkv_ref, wg_ref, wc_ref, wz_ref, cw_ref)
    q_ref[...] = q
    kv_ref[...] = kv
    gates_ref[...] = gates
    ns = (CONV_W - 1) * nb
    uext_ref[0:ns, :] = st_ref[...]
    uext_ref[ns:ns + nt * nb, :] = u
    y = cw_ref[2:3, :] * u
    for k in range(CONV_W - 1):
        y = y + cw_ref[k:k + 1, :] * uext_ref[k * nb:k * nb + nt * nb, :]
    yc_ref[...] = (bc * y * szc).astype(BF)
    sza_ref[...] = sza.astype(BF)
    cv_ref[...] = uext_ref[nt * nb:nt * nb + ns, :]


def _const_spec(shape):
    nd = len(shape)
    return pl.BlockSpec(shape, lambda *_: (0,) * nd, pipeline_mode=pl.Buffered(1))


def _proj_prompt(x2d, sc, sh, ng, wts, cw, batch, seq, tm):
    tokens, d = x2d.shape
    tpb = seq // tm
    wq, wkv, wg, wc, wz = wts
    dc = wc.shape[1] // 3
    da = wz.shape[1] - dc
    tok_spec = lambda w: pl.BlockSpec((tm, w), lambda t: (t, 0))
    rowT_spec = lambda r: pl.BlockSpec((1, r, tm), lambda t: (t // tpb, 0, t % tpb))
    row_spec = lambda w: pl.BlockSpec((1, tm, w), lambda t: (t // tpb, t % tpb, 0))
    mod_spec = pl.BlockSpec((1, 1, d), lambda t: (t // tpb, 0, 0))
    out_shape = (
        [jax.ShapeDtypeStruct((batch, wq.shape[1], seq), BF)]
        + [jax.ShapeDtypeStruct((batch, D_KV, seq), F32)] * 6
        + [jax.ShapeDtypeStruct((tokens, D_KV), F32)] * 2
        + [jax.ShapeDtypeStruct((batch, seq, D_KV), BF), jax.ShapeDtypeStruct((batch, D_KV, seq), BF),
           jax.ShapeDtypeStruct((batch, seq, D_KV), BF), jax.ShapeDtypeStruct((batch, D_KV, seq), BF),
           jax.ShapeDtypeStruct((tokens, LANES), F32),
           jax.ShapeDtypeStruct((tokens, dc), BF), jax.ShapeDtypeStruct((tokens, da), BF),
           jax.ShapeDtypeStruct((batch, CONV_W - 1, dc), F32)])
    out_specs = (
        [rowT_spec(wq.shape[1])] + [rowT_spec(D_KV)] * 6 + [tok_spec(D_KV)] * 2
        + [row_spec(D_KV), rowT_spec(D_KV), row_spec(D_KV), rowT_spec(D_KV),
           tok_spec(LANES), tok_spec(dc), tok_spec(da),
           pl.BlockSpec((1, CONV_W - 1, dc), lambda t: (t // tpb, 0, 0))])
    return pl.pallas_call(
        functools.partial(_proj_prompt_kernel, tiles_per_batch=tpb, tm=tm),
        grid=(tokens // tm,),
        in_specs=[tok_spec(d), mod_spec, mod_spec, _const_spec((1, d)),
                  _const_spec(wq.shape), _const_spec(wkv.shape), _const_spec(wg.shape),
                  _const_spec(wc.shape), _const_spec(wz.shape), _const_spec(cw.shape)],
        out_specs=out_specs,
        out_shape=out_shape,
        scratch_shapes=[pltpu.VMEM((tm + 8, dc), F32)],
        compiler_params=pltpu.CompilerParams(dimension_semantics=("arbitrary",),
                                             vmem_limit_bytes=V7X_VMEM_LIMIT),
        name="proj_prompt",
    )(x2d, sc, sh, ng, wq, wkv, wg, wc, wz, cw)


def _proj_sample(x2d, sc, sh, ng, wts, cw, state2d, nb, nt):
    tokens, d = x2d.shape
    wq, wkv, wg, wc, wz = wts
    dc = wc.shape[1] // 3
    da = wz.shape[1] - dc
    ns = (CONV_W - 1) * nb
    full = lambda a: pl.BlockSpec(a.shape, lambda i: (0,) * a.ndim)
    ins = (x2d, sc, sh, ng, wq, wkv, wg, wc, wz, cw, state2d)
    out_shape = [jax.ShapeDtypeStruct((tokens, wq.shape[1]), F32),
                 jax.ShapeDtypeStruct((tokens, wkv.shape[1]), F32),
                 jax.ShapeDtypeStruct((tokens, LANES), F32),
                 jax.ShapeDtypeStruct((tokens, dc), BF), jax.ShapeDtypeStruct((tokens, da), BF),
                 jax.ShapeDtypeStruct((ns, dc), F32)]
    return pl.pallas_call(
        functools.partial(_proj_sample_kernel, nb=nb, nt=nt),
        grid=(1,),
        in_specs=[full(a) for a in ins],
        out_specs=[pl.BlockSpec(s.shape, lambda i: (0, 0)) for s in out_shape],
        out_shape=out_shape,
        scratch_shapes=[pltpu.VMEM((ns + tokens, dc), F32)],
        compiler_params=pltpu.CompilerParams(vmem_limit_bytes=V7X_VMEM_LIMIT),
        name="proj_sample",
    )(*ins)


def _compress(x_ref, nblk, pe_ref, w1_ref, w2_ref):
    def body(pp, acc):
        xp = jnp.concatenate([x_ref[pl.ds(2 * pp, nblk, stride=BLK), :],
                              x_ref[pl.ds(2 * pp + 1, nblk, stride=BLK), :]], axis=1) + pe_ref[pl.ds(pp, 1), :]
        return acc + _dot(xp.astype(BF), w1_ref[pp])
    acc = lax.fori_loop(0, BLK // 2, body, jnp.zeros((nblk, N_KV * CMP_HID), F32), unroll=2)
    return _dot(_silu(acc).astype(BF), w2_ref[...])


def _compress_prompt_kernel(kc_ref, vc_ref, pek_ref, w1k_ref, w2k_ref, pev_ref, w1v_ref, w2v_ref,
                            kcmp_ref, vcmpT_ref, *, nblk):
    kcmp_ref[0] = _compress(kc_ref.at[0], nblk, pek_ref, w1k_ref, w2k_ref).astype(BF)
    vcmpT_ref[0] = _compress(vc_ref.at[0], nblk, pev_ref, w1v_ref, w2v_ref).T.astype(BF)


def _compress_prompt(kc, vc, cw, batch, seq):
    nblk = seq // BLK
    row = pl.BlockSpec((1, seq, D_KV), lambda b: (b, 0, 0))
    return pl.pallas_call(
        functools.partial(_compress_prompt_kernel, nblk=nblk),
        grid=(batch,),
        in_specs=[row, row] + [_const_spec(a.shape) for a in cw],
        out_specs=[pl.BlockSpec((1, nblk, D_KV), lambda b: (b, 0, 0)),
                   pl.BlockSpec((1, D_KV, nblk), lambda b: (b, 0, 0))],
        out_shape=[jax.ShapeDtypeStruct((batch, nblk, D_KV), BF),
                   jax.ShapeDtypeStruct((batch, D_KV, nblk), BF)],
        compiler_params=pltpu.CompilerParams(vmem_limit_bytes=V7X_VMEM_LIMIT),
        name="compress_prompt",
    )(kc, vc, *cw)


SCORE_REMOVED = -3.0


def _select_topk(score, axis, k):
    idx = lax.broadcasted_iota(I32, score.shape, axis)
    past_end = jnp.int32(score.shape[axis])
    picked = jnp.zeros(score.shape, F32)
    s = score
    for _ in range(k):
        m = jnp.max(s, axis=axis, keepdims=True)
        first = jnp.min(jnp.where(s == m, idx, past_end), axis=axis, keepdims=True)
        hit = idx == first
        picked = jnp.where(hit, 1.0, picked)
        s = jnp.where(hit, SCORE_REMOVED, s)
    return picked > 0.5


def _attend(br, lhs, rhs_ref, krows, bias, vT, m_ref, acc_ref):
    s = _dot(lhs, rhs_ref[0:krows, :])
    if bias is not None:
        s = s + bias
    _consume(br, s, vT, m_ref, acc_ref)


SUB = 8


def _col_max(s):
    part = jnp.max(s.reshape(s.shape[0] // SUB, SUB, s.shape[1]), axis=0)
    return jnp.broadcast_to(jnp.max(part, axis=0, keepdims=True), part.shape)


def _rows_op(op, x, r):
    return op(x.reshape(x.shape[0] // SUB, SUB, x.shape[1]), r[None]).reshape(x.shape)


def _pv_update(br, pb, vT, scale_old, scale_new, acc_ref):
    ones = jnp.ones((ACC_ROWS - HEAD_DIM, pb.shape[0]), BF)
    for g in range(N_KV):
        gs = slice(g * GCOL, (g + 1) * GCOL)
        v_aug = jnp.concatenate([vT[g * HEAD_DIM:(g + 1) * HEAD_DIM, :], ones], axis=0)
        acc = acc_ref[br, g]
        if scale_old is not None:
            acc = _rows_op(jnp.multiply, acc, scale_old[:, gs])
        acc = acc + _dot(v_aug, pb[:, gs])
        if scale_new is not None:
            acc = _rows_op(jnp.multiply, acc, scale_new[:, gs])
        acc_ref[br, g] = acc


def _consume(br, s, vT, m_ref, acc_ref):
    m_old = m_ref[br]
    m_new = jnp.maximum(m_old, _col_max(s))
    alpha = jnp.exp2(m_old - m_new)
    pb = jnp.exp2(_rows_op(jnp.subtract, s, m_new)).astype(BF)
    m_ref[br] = m_new
    _pv_update(br, pb, vT, alpha, None, acc_ref)


def _nsa_prompt_kernel(qT_ref, gates_ref, kslc_ref, vslcT_ref, kwin_ref, vwinT_ref, kcmp_ref, vcmpT_ref,
                       cb_ref, sb_ref, w0_ref, o_ref,
                       rhs_ref, m_ref, acc_ref, s0_ref, s1_ref, *, nb):
    i = pl.program_id(1)

    for gh in range(N_HEADS):
        g = gh // HPG
        cols = slice(gh * QT, (gh + 1) * QT)
        rhs_ref[g * HEAD_DIM:(g + 1) * HEAD_DIM, cols] = qT_ref[0, gh * HEAD_DIM:(gh + 1) * HEAD_DIM, :]
        og = 1 - g
        rhs_ref[og * HEAD_DIM:(og + 1) * HEAD_DIM, cols] = jnp.zeros((HEAD_DIM, QT), BF)

    lg = _dot(kcmp_ref[0], rhs_ref[0:D_KV, :])
    rel = 2 * i - lax.broadcasted_iota(I32, (nb, NCOL), 0)
    bias = jnp.where(rel >= 3, 0.0, NEG)
    for rr in range(4):
        bias = jnp.where(rel == rr - 1, cb_ref[rr:rr + 1, :], bias)
    lgb = lg + bias
    mc = jnp.max(lgb, axis=0, keepdims=True)
    pc = jnp.exp2(lgb - mc)
    lc = jnp.sum(pc, axis=0, keepdims=True)
    pn = pc * jnp.where(mc > VALID_THRESH, 1.0 / lc, 0.0)
    pnb = pn.astype(BF)
    for g in range(N_KV):
        acc_ref[0, g, 0:HEAD_DIM, :] = _dot(vcmpT_ref[0, g * HEAD_DIM:(g + 1) * HEAD_DIM, :],
                                            pnb[:, g * GCOL:(g + 1) * GCOL])

    imp, valid = [], []
    for g in range(N_KV):
        a = pn[:, g * GCOL:g * GCOL + QT]
        for h in range(1, HPG):
            a = a + pn[:, g * GCOL + h * QT:g * GCOL + (h + 1) * QT]
        imp.append(a)
        valid.append(lgb[:, g * GCOL:g * GCOL + QT] > VALID_THRESH)
    imp = jnp.concatenate(imp, axis=1)
    valid = jnp.concatenate(valid, axis=1)
    jrow = lax.broadcasted_iota(I32, (nb, N_KV * QT), 0)
    tpos = i * QT + (lax.broadcasted_iota(I32, (nb, N_KV * QT), 1) & (QT - 1))
    score = jnp.where(valid, imp, jnp.where(jrow * BLK <= tpos, 2.0, -1.0))
    sel = _select_topk(score, 0, min(N_SEL, nb))
    selbias = jnp.where(jnp.logical_and(sel, score >= 0.0), 0.0, NEG).astype(BF)
    for gh in range(N_HEADS):
        g = gh // HPG
        rhs_ref[D_KV:D_KV + nb, gh * QT:(gh + 1) * QT] = selbias[:, g * QT:(g + 1) * QT]

    for br in (1, 2):
        m_ref[br] = jnp.full((SUB, NCOL), M_INIT, F32)
        acc_ref[br] = jnp.zeros((N_KV, ACC_ROWS, GCOL), F32)

    def sel_keys(kt, tk):
        k0 = pl.multiple_of(kt * KT, KT)
        k = kslc_ref[0, pl.ds(k0, tk), :]
        jj = lax.broadcasted_iota(I32, (tk, nb), 1)
        kk = lax.broadcasted_iota(I32, (tk, nb), 0)
        onehot = jnp.where(jj == (KT // BLK) * kt + jnp.right_shift(kk, BLK_SHIFT), 1.0, 0.0).astype(BF)
        return jnp.concatenate([k, onehot], axis=1), vslcT_ref[0, :, pl.ds(k0, tk)]

    def sel_tile(kt, tk, bias_tile):
        ka, vT = sel_keys(kt, tk)
        _attend(1, ka, rhs_ref, D_KV + nb, bias_tile, vT, m_ref, acc_ref)

    n_far = jnp.maximum(i - 1, 0)
    n_big = n_far // FAR_GROUP
    n_pairs = n_big // 2
    big = FAR_GROUP * KT

    def far_scores(j):
        ka, _ = sel_keys(j * FAR_GROUP, big)
        return _dot(ka, rhs_ref[...])

    def far_step(cur_ref, nxt_ref, j_cur, j_nxt):
        m_cur = m_ref[1]
        pb = jnp.exp2(_rows_op(jnp.subtract, cur_ref[...], m_cur)).astype(BF)
        s_nxt = far_scores(j_nxt)
        nxt_ref[...] = s_nxt
        m_nxt = jnp.maximum(m_cur, _col_max(s_nxt))
        vT = vslcT_ref[0, :, pl.ds(pl.multiple_of(j_cur * big, big), big)]
        _pv_update(1, pb, vT, None, jnp.exp2(m_cur - m_nxt), acc_ref)
        m_ref[1] = m_nxt

    @pl.when(n_pairs > 0)
    def _():
        s = far_scores(0)
        s0_ref[...] = s
        m_ref[1] = jnp.maximum(m_ref[1], _col_max(s))

    def far_pair(jp, c):
        @pl.when(jp < n_pairs)
        def _():
            far_step(s0_ref, s1_ref, 2 * jp, 2 * jp + 1)

        @pl.when(2 * jp + 1 < n_big)
        def _():
            far_step(s1_ref, s0_ref, 2 * jp + 1, 2 * jp + 2)
        return c
    lax.fori_loop(0, n_pairs - 1, far_pair, 0)

    @pl.when(n_pairs > 0)
    def _():
        far_step(s0_ref, s1_ref, 2 * n_pairs - 2, 2 * n_pairs - 1)

    @pl.when(n_big > 1)
    def _():
        j_last = 2 * n_pairs - 1
        pb = jnp.exp2(_rows_op(jnp.subtract, s1_ref[...], m_ref[1])).astype(BF)
        vT = vslcT_ref[0, :, pl.ds(pl.multiple_of(j_last * big, big), big)]
        _pv_update(1, pb, vT, None, None, acc_ref)

    @pl.when(n_big > 2 * n_pairs)
    def _():
        sel_tile((n_big - 1) * FAR_GROUP, big, None)

    def far_small(kt, c):
        sel_tile(kt, KT, None)
        return c
    lax.fori_loop(n_big * FAR_GROUP, n_far, far_small, 0)

    n_win = WINDOW // KT + 1

    @pl.when(i >= n_win - 1)
    def _():
        ka, vT_band = sel_keys(i - 1, 2 * KT)
        s_band = _dot(ka, rhs_ref[...]) + sb_ref[...]
        k0 = pl.multiple_of((i - (n_win - 1)) * KT, KT)
        s_win = _dot(kwin_ref[0, pl.ds(k0, n_win * KT), :], rhs_ref[0:D_KV, :]) + jnp.concatenate(
            [w0_ref[...], jnp.zeros(((n_win - 3) * KT, NCOL), F32), sb_ref[...]], axis=0)
        _consume(1, s_band, vT_band, m_ref, acc_ref)
        _consume(2, s_win, vwinT_ref[0, :, pl.ds(k0, n_win * KT)], m_ref, acc_ref)

    @pl.when(jnp.logical_and(i >= 1, i < n_win - 1))
    def _():
        sel_tile(i - 1, 2 * KT, sb_ref[...])

    @pl.when(i == 0)
    def _():
        sel_tile(0, KT, sb_ref[KT:2 * KT, :])

    @pl.when(i < n_win - 1)
    def _():
        for r in range(1, n_win):
            kt = i - (n_win - 1) + r

            @pl.when(kt >= 0)
            def _():
                k0 = pl.multiple_of(kt * KT, KT)
                r0 = (r - (n_win - 2)) * KT
                bias = sb_ref[r0:r0 + KT, :] if r >= n_win - 2 else None
                _attend(2, kwin_ref[0, pl.ds(k0, KT), :], rhs_ref, D_KV, bias,
                        vwinT_ref[0, :, pl.ds(k0, KT)], m_ref, acc_ref)

    gts = gates_ref[0, 0]
    for g in range(N_KV):
        cs = slice(g * GCOL, (g + 1) * GCOL)
        o = gts[0:1, cs] * acc_ref[0, g, 0:HEAD_DIM, :]
        for br in (1, 2):
            l = acc_ref[br, g, HEAD_DIM:HEAD_DIM + 1, :]
            o = o + (gts[br:br + 1, cs] * jnp.where(l > 0.0, 1.0 / l, 0.0)) * acc_ref[br, g, 0:HEAD_DIM, :]
        for hp in range(HPG // 2):
            pair = jnp.concatenate([o[:, (2 * hp) * QT:(2 * hp + 1) * QT],
                                    o[:, (2 * hp + 1) * QT:(2 * hp + 2) * QT]], axis=0)
            c0 = g * HPG * HEAD_DIM + hp * 2 * HEAD_DIM
            o_ref[0, :, c0:c0 + 2 * HEAD_DIM] = pair.T.astype(BF)


def _nsa_prompt(qT, gates, kslc, vslcT, kwin, vwinT, kcmp, vcmpT, tables, batch, seq):
    nb = seq // BLK
    nq = seq // QT
    cb, sb, w0 = tables
    dq = qT.shape[1]
    per_b = lambda shape: pl.BlockSpec((1,) + shape, lambda b, i: (b, 0, 0))
    return pl.pallas_call(
        functools.partial(_nsa_prompt_kernel, nb=nb),
        grid=(batch, nq),
        in_specs=[pl.BlockSpec((1, dq, QT), lambda b, i: (b, 0, i)),
                  pl.BlockSpec((1, 1, 8, NCOL), lambda b, i: (b, i, 0, 0)),
                  per_b((seq, D_KV)), per_b((D_KV, seq)), per_b((seq, D_KV)), per_b((D_KV, seq)),
                  per_b((nb, D_KV)), per_b((D_KV, nb)),
                  _const_spec(cb.shape), _const_spec(sb.shape), _const_spec(w0.shape)],
        out_specs=pl.BlockSpec((1, QT, dq), lambda b, i: (b, i, 0)),
        out_shape=jax.ShapeDtypeStruct((batch, seq, dq), BF),
        scratch_shapes=[pltpu.VMEM((D_KV + nb, NCOL), BF),
                        pltpu.VMEM((3, SUB, NCOL), F32),
                        pltpu.VMEM((3, N_KV, ACC_ROWS, GCOL), F32),
                        pltpu.VMEM((FAR_GROUP * KT, NCOL), F32),
                        pltpu.VMEM((FAR_GROUP * KT, NCOL), F32)],
        compiler_params=pltpu.CompilerParams(dimension_semantics=("arbitrary", "arbitrary"),
                                             vmem_limit_bytes=V7X_VMEM_LIMIT),
        name="nsa_prompt",
    )(qT, gates, kslc, vslcT, kwin, vwinT, kcmp, vcmpT, cb, sb, w0)


def _merge_kernel(x_ref, o_ref, sza_ref, yc_ref, gate_ref, woa_ref, woc_ref, fg_ref, y_ref):
    og = (o_ref[...].astype(F32) * sza_ref[...].astype(F32)).astype(BF)
    acc = _dot(og, woa_ref[...]) + _dot(yc_ref[...], woc_ref[...])
    xo = x_ref[...] + gate_ref[0] * acc
    ms = jnp.mean(xo * xo, axis=-1, keepdims=True)
    y_ref[...] = xo * lax.rsqrt(ms + EPS) * fg_ref[...]


def _merge(x2d, o2d, sza, yc, gate, woa, woc, fg, tm, rows_per_gate):
    tokens, d = x2d.shape
    tok = lambda w: pl.BlockSpec((tm, w), lambda t: (t, 0))
    gr = gate.shape[1]
    return pl.pallas_call(
        _merge_kernel,
        grid=(tokens // tm,),
        in_specs=[tok(d), tok(o2d.shape[1]), tok(sza.shape[1]), tok(yc.shape[1]),
                  pl.BlockSpec((1, gr, d), lambda t: ((t * tm) // rows_per_gate, 0, 0)),
                  _const_spec(woa.shape), _const_spec(woc.shape), _const_spec(fg.shape)],
        out_specs=tok(d),
        out_shape=jax.ShapeDtypeStruct((tokens, d), F32),
        compiler_params=pltpu.CompilerParams(dimension_semantics=("arbitrary",),
                                             vmem_limit_bytes=V7X_VMEM_LIMIT),
        name="merge_out",
    )(x2d, o2d, sza, yc, gate, woa, woc, fg)


def _softmax_rows(parts):
    m = parts[0].max(axis=1, keepdims=True)
    for s in parts[1:]:
        m = jnp.maximum(m, s.max(axis=1, keepdims=True))
    ps = [jnp.exp2(s - m) for s in parts]
    l = ps[0].sum(axis=1, keepdims=True)
    for p in ps[1:]:
        l = l + p.sum(axis=1, keepdims=True)
    inv = jnp.where(jnp.logical_and(m > VALID_THRESH, l > 0.0), 1.0 / l, 0.0)
    return ps, inv


def _nsa_sample_kernel(pt_ref, ckc_ref, cvc_ref, cks_ref, cvs_ref,
                       qbd_ref, gates_ref, tails_ref, tailsT_ref, bkwT_ref, bvwT_ref,
                       cbs_ref, sbl_ref, sbt_ref, wbl_ref, wbt_ref, elocT_ref,
                       pek_ref, w1k_ref, w2k_ref, pev_ref, w1v_ref, w2v_ref,
                       o_ref, skwT_ref, svwT_ref,
                       buf_ref, sem, xbuf_ref, cmpf_ref, kcmp_ref, vcmp_ref, sbuf_ref, pbuf_ref,
                       *, n_pages, page, n_dec):
    b = pl.program_id(0)
    nbatch = pl.num_programs(0)
    past = n_pages * page
    nbp = past // BLK
    nbc = kcmp_ref.shape[0]
    n_tiles = past // SAMPLE_KEY_TILE
    nrow = qbd_ref.shape[1]
    caches = (ckc_ref, cvc_ref, cks_ref, cvs_ref)

    def page_copy(cache_ref, pg, j, slot):
        return pltpu.make_async_copy(cache_ref.at[pg],
                                     buf_ref.at[slot, :, pl.ds(pl.multiple_of(j * page, page), page)],
                                     sem.at[slot])

    def start_fetch(cache_ref, bb, slot):
        def body(jj, c):
            for prio in range(2):
                j = 2 * jj + prio
                page_copy(cache_ref, pt_ref[bb, j], j, slot).start(priority=prio)
            return c
        lax.fori_loop(0, n_pages // 2, body, 0, unroll=2)

    def wait_fetch(cache_ref, slot):
        def body(j, c):
            page_copy(cache_ref, 0, j, slot).wait()
            return c
        lax.fori_loop(0, n_pages, body, 0)

    def stage(c):
        if c + 1 < len(caches):
            start_fetch(caches[c + 1], b, (c + 1) % 2)
        else:
            @pl.when(b + 1 < nbatch)
            def _():
                start_fetch(caches[0], b + 1, 0)
        wait_fetch(caches[c], c % 2)
        return buf_ref.at[c % 2]

    @pl.when(b == 0)
    def _():
        start_fetch(caches[0], 0, 0)

    qbd = qbd_ref[0]

    def compress_cache(c, slot_ref, pe_ref, w1_ref, w2_ref, dst_ref):
        def xpose(j, carry):
            src = slot_ref[:, pl.ds(pl.multiple_of(j * page, page), page)]
            xbuf_ref[pl.ds(pl.multiple_of(j * XPITCH, 8), page), :] = src.T
            return carry
        lax.fori_loop(0, n_pages, xpose, 0, unroll=8)
        n_extra = XTAIL_PAGES * XPITCH
        xbuf_ref[pl.ds(n_pages * XPITCH, n_extra), :] = jnp.zeros((n_extra, D_KV), F32)
        xbuf_ref[pl.ds(n_pages * XPITCH, 8), :] = tails_ref[0, c]
        n_even = n_pages + XTAIL_PAGES

        def rows(r0, n):
            return jnp.concatenate([xbuf_ref[pl.ds(r0, n, stride=XPITCH), :],
                                    xbuf_ref[pl.ds(r0 + 1, n, stride=XPITCH), :]], axis=1)

        def body(pp, acc):
            acc_e, acc_o = acc
            pe = pe_ref[pl.ds(pp, 1), :]
            w = w1_ref[pp]
            xe = rows(2 * pp, n_even) + pe
            xo = rows(2 * pp + BLK, n_pages) + pe
            return acc_e + _dot(xe.astype(BF), w), acc_o + _dot(xo.astype(BF), w)
        acc_e, acc_o = lax.fori_loop(
            0, BLK // 2, body,
            (jnp.zeros((n_even, N_KV * CMP_HID), F32), jnp.zeros((n_pages, N_KV * CMP_HID), F32)), unroll=2)
        out_e = _dot(_silu(acc_e).astype(BF), w2_ref[...])
        out_o = _dot(_silu(acc_o).astype(BF), w2_ref[...])
        cmpf_ref[pl.ds(0, n_pages, stride=2), :] = out_e[0:n_pages]
        cmpf_ref[pl.ds(1, n_pages, stride=2), :] = out_o
        cmpf_ref[nbp:nbp + XTAIL_PAGES, :] = out_e[n_pages:n_even]
        cmpf_ref[nbp + XTAIL_PAGES:nbp + TAIL_BLOCKS, :] = jnp.zeros((TAIL_BLOCKS - XTAIL_PAGES, D_KV), F32)
        dst_ref[0:nbp + TAIL_BLOCKS, :] = cmpf_ref[...].astype(BF)
        dst_ref[nbp + TAIL_BLOCKS:nbc, :] = jnp.zeros((nbc - nbp - TAIL_BLOCKS, D_KV), BF)

    compress_cache(0, stage(0), pek_ref, w1k_ref, w2k_ref, kcmp_ref)
    compress_cache(1, stage(1), pev_ref, w1v_ref, w2v_ref, vcmp_ref)

    lgc = _dot_nt(qbd, kcmp_ref[...]) + cbs_ref[...]
    (pc,), invc = _softmax_rows([lgc])
    pn = pc * invc
    o_cmp = _dot(pn.astype(BF), vcmp_ref[...])
    imp = jnp.sum(pn.reshape(nrow // HPG, HPG, nbc), axis=1)
    jcol = lax.broadcasted_iota(I32, imp.shape, 1)
    score = jnp.where(jcol < nbp, imp, jnp.where(jcol == nbp, 2.0, -1.0))
    sel = _select_topk(score, 1, N_SEL)
    selbias = jnp.where(jnp.logical_and(sel, score >= 0.0), 0.0, NEG)
    selb = jnp.broadcast_to(selbias[:, None, :], (nrow // HPG, HPG, nbc)).reshape(nrow, nbc).astype(BF)

    kslot = stage(2)
    tile = lambda t: slice(t * SAMPLE_KEY_TILE, (t + 1) * SAMPLE_KEY_TILE)
    for t in range(n_tiles):
        sbuf_ref[t] = (_dot(qbd, kslot[:, tile(t)].astype(BF))
                       + _dot(selb[:, t * LANES:(t + 1) * LANES], elocT_ref[...]))
    last = slice(SAMPLE_KEY_TILE - LANES, SAMPLE_KEY_TILE)
    sbuf_ref[n_tiles - 1, :, last] = sbuf_ref[n_tiles - 1, :, last] + sbl_ref[...]
    s_tail = (_dot(qbd, tailsT_ref[0, 0].astype(BF))
              + _dot(selb[:, nbp:nbp + LANES], elocT_ref[:, 0:LANES]) + sbt_ref[...])
    ps, inv_sel = _softmax_rows([sbuf_ref[t] for t in range(n_tiles)] + [s_tail])
    for t in range(n_tiles):
        pbuf_ref[t] = ps[t].astype(BF)
    p_tail = ps[n_tiles].astype(BF)

    vslot = stage(3)
    o_sel = _dot_nt(p_tail, tailsT_ref[0, 1].astype(BF))
    for t in range(n_tiles):
        o_sel = o_sel + _dot_nt(pbuf_ref[t], vslot[:, tile(t)].astype(BF))
    o_sel = o_sel * inv_sel

    s_w = _dot(qbd, bkwT_ref[0].astype(BF)) + wbl_ref[...]
    s_wt = _dot(qbd, tailsT_ref[0, 2].astype(BF)) + wbt_ref[...]
    (p_w, p_wt), inv_w = _softmax_rows([s_w, s_wt])
    o_win = (_dot_nt(p_w.astype(BF), bvwT_ref[0].astype(BF))
             + _dot_nt(p_wt.astype(BF), tailsT_ref[0, 3].astype(BF))) * inv_w

    o_ref[0] = gates_ref[0, 0] * o_cmp + gates_ref[0, 1] * o_sel + gates_ref[0, 2] * o_win

    nbuf = bkwT_ref.shape[2]
    lane = lax.broadcasted_iota(I32, (D_KV, LANES), 1)
    for n, src, dst in ((2, bkwT_ref, skwT_ref), (3, bvwT_ref, svwT_ref)):
        rolled = pltpu.roll(src[0], nbuf - n_dec, axis=1)
        new = pltpu.roll(tailsT_ref[0, n], LANES - n_dec, axis=1)
        dst[0, :, 0:nbuf - LANES] = rolled[:, 0:nbuf - LANES]
        dst[0, :, nbuf - LANES:nbuf] = jnp.where(lane >= LANES - n_dec, new, rolled[:, nbuf - LANES:nbuf])


def _nsa_sample(page_table, caches, qbd, gates3, tails, tailsT, bkwT, bvwT, tables, cw, n_dec):
    nbatch, n_pages = page_table.shape
    page = caches[0].shape[2]
    past = n_pages * page
    cbs, sbl, sbt, wbl, wbt, elocT = tables
    nbc = cbs.shape[1]
    nrow = qbd.shape[1]
    nbuf = bkwT.shape[2]
    nbp = past // BLK
    n_tiles = past // SAMPLE_KEY_TILE
    assert past % SAMPLE_KEY_TILE == 0 and nbp % LANES == 0 and page == 2 * BLK and nbuf % LANES == 0
    per_b = lambda a: pl.BlockSpec((1,) + a.shape[1:], lambda b, pt: (b,) + (0,) * (a.ndim - 1))
    const = lambda a: pl.BlockSpec(a.shape, lambda b, pt: (0,) * a.ndim, pipeline_mode=pl.Buffered(1))
    hbm = pl.BlockSpec(memory_space=pl.ANY)
    grid_spec = pltpu.PrefetchScalarGridSpec(
        num_scalar_prefetch=1,
        grid=(nbatch,),
        in_specs=[hbm] * 4 + [per_b(a) for a in (qbd, gates3, tails, tailsT, bkwT, bvwT)]
                 + [const(a) for a in tables] + [const(a) for a in cw],
        out_specs=[pl.BlockSpec((1, nrow, D_KV), lambda b, pt: (b, 0, 0)),
                   pl.BlockSpec((1, D_KV, nbuf), lambda b, pt: (b, 0, 0)),
                   pl.BlockSpec((1, D_KV, nbuf), lambda b, pt: (b, 0, 0))],
        scratch_shapes=[pltpu.VMEM((2, D_KV, past), F32),
                        pltpu.SemaphoreType.DMA((2,)),
                        pltpu.VMEM(((n_pages + XTAIL_PAGES) * XPITCH, D_KV), F32),
                        pltpu.VMEM((nbp + TAIL_BLOCKS, D_KV), F32),
                        pltpu.VMEM((nbc, D_KV), BF),
                        pltpu.VMEM((nbc, D_KV), BF),
                        pltpu.VMEM((n_tiles, nrow, SAMPLE_KEY_TILE), F32),
                        pltpu.VMEM((n_tiles, nrow, SAMPLE_KEY_TILE), BF)])
    return pl.pallas_call(
        functools.partial(_nsa_sample_kernel, n_pages=n_pages, page=page, n_dec=n_dec),
        grid_spec=grid_spec,
        out_shape=[jax.ShapeDtypeStruct((nbatch, nrow, D_KV), F32),
                   jax.ShapeDtypeStruct((nbatch, D_KV, nbuf), F32),
                   jax.ShapeDtypeStruct((nbatch, D_KV, nbuf), F32)],
        compiler_params=pltpu.CompilerParams(dimension_semantics=("arbitrary",),
                                             vmem_limit_bytes=V7X_VMEM_LIMIT),
        name="nsa_sample",
    )(page_table, *caches, qbd, gates3, tails, tailsT, bkwT, bvwT, *tables, *cw)


def _t5_bucket_np(n):
    d = np.maximum(np.arange(n), 0)
    max_exact = N_BUCKETS // 2
    nf = np.maximum(d, max_exact).astype(np.float32)
    large = max_exact + (np.log(nf / np.float32(max_exact)) / np.float32(math.log(MAX_DIST / max_exact))
                         * np.float32(N_BUCKETS - max_exact)).astype(np.int32)
    return np.where(d < max_exact, d, np.minimum(large, N_BUCKETS - 1))


N_NEAR = 256
_BUCKETS = _t5_bucket_np(N_NEAR)
assert np.all(_BUCKETS[QT + 1:] == N_BUCKETS - 1)


def _shifted_bias(rel_bias):
    return (rel_bias[_BUCKETS[:N_NEAR]] - rel_bias[N_BUCKETS - 1][None, :]) * LOG2E


def _bias_cols(tp, dist):
    val = tp[np.clip(dist, 0, N_NEAR - 1)]
    val = jnp.where((dist >= 0)[:, :, None], val, NEG)
    return jnp.transpose(val, (0, 2, 1)).reshape(dist.shape[0], -1)


def _prompt_tables(rel_bias, nb):
    tp = _shifted_bias(rel_bias)
    q = np.arange(QT)[None, :]
    r = np.arange(-1, 7)[:, None]
    d_cmp = np.where(r <= 2, BLK * r - (BLK - 1) + q, -1)
    cb = _bias_cols(tp, d_cmp)
    n_neg, rows = 2 * KT, 2 * KT
    w = jnp.concatenate([jnp.full((N_HEADS, n_neg), NEG, F32), tp.T, jnp.zeros((N_HEADS, 1), F32)], axis=1)
    period = w.shape[1]
    skew = jnp.tile(w, (1, rows))[:, :rows * (period - 1)].reshape(N_HEADS, rows, period - 1)
    c0 = n_neg + KT
    sb = jnp.transpose(skew[:, :, c0:c0 + QT], (1, 0, 2)).reshape(rows, N_HEADS * QT)
    kk = np.arange(KT)[:, None]
    w0 = jnp.tile(jnp.asarray(np.where(kk >= q, 0.0, NEG), F32), (1, N_HEADS))
    return cb, sb, w0


def _sample_rows_bias(tp, dist):
    val = tp[np.clip(dist, 0, N_NEAR - 1)]
    val = jnp.where((dist >= 0)[:, :, None], val, NEG)
    val = val.reshape(dist.shape[0], dist.shape[1], N_KV, HPG)
    return jnp.transpose(val, (2, 0, 3, 1)).reshape(-1, dist.shape[1])


def _sample_tables(rel_bias, past, n_dec, nbuf, nbc):
    tp = _shifted_bias(rel_bias)
    q = np.arange(n_dec)[:, None]
    nbp = past // BLK
    j = np.arange(nbc)[None, :]
    cbs = _sample_rows_bias(tp, np.where(j < nbp, past + q - (BLK * j + BLK - 1), -1))
    kk = np.arange(LANES)[None, :]
    sbl = _sample_rows_bias(tp, q + LANES - kk)
    sbt = _sample_rows_bias(tp, q - kk)
    r = np.arange(nbuf)[None, :]
    d_w = nbuf + q - r
    wbl = _sample_rows_bias(tp, np.where(d_w <= WINDOW, d_w, -1))
    wbt = sbt
    kk = np.arange(SAMPLE_KEY_TILE)[None, :]
    elocT = jnp.asarray(kk // BLK == np.arange(LANES)[:, None], BF)
    return cbs, sbl, sbt, wbl, wbt, elocT


def _compress_weights(pe, w1, w2):
    pe2 = jnp.tile(pe, (1, N_KV)).reshape(BLK // 2, 2 * D_KV)
    w1r = w1.reshape(BLK, HEAD_DIM, CMP_HID)
    eye = jnp.eye(N_KV, dtype=w1.dtype)
    w1bd = jnp.einsum("pdh,gk->pgdkh", w1r, eye).reshape(BLK // 2, 2 * D_KV, N_KV * CMP_HID).astype(BF)
    w2bd = jnp.einsum("hd,gk->ghkd", w2, eye).reshape(N_KV * CMP_HID, D_KV).astype(BF)
    return pe2, w1bd, w2bd


def _split_w_in(w_in, d_attn, d_conv, d_mix):
    sizes = (d_attn,) + (D_KV,) * 6 + (3 * N_HEADS,) + (d_conv,) * 3 + (d_mix,)
    offs = np.cumsum((0,) + sizes)
    wq = w_in[:, offs[0]:offs[1]]
    wkv = w_in[:, offs[1]:offs[7]]
    wg = jnp.pad(w_in[:, offs[7]:offs[8]], ((0, 0), (0, LANES - 3 * N_HEADS)))
    wc = w_in[:, offs[8]:offs[11]]
    wz = w_in[:, offs[11]:offs[12]]
    return tuple(w.astype(BF) for w in (wq, wkv, wg, wc, wz))


def kernel(x_prompt, x_sample, cache_k_cmp, cache_v_cmp, cache_k_slc, cache_v_slc, cache_k_win,
           cache_v_win, state_conv, page_table, c_prompt, c_sample, ada_w, ada_b, norm_g, w_in,
           cmp_pe_k, cmp_w1_k, cmp_w2_k, cmp_pe_v, cmp_w1_v, cmp_w2_v, conv_w, w_out, rel_bias, final_g):
    depth = ada_w.shape[0]
    assert depth == 1
    batch, seq, d = x_prompt.shape
    nbatch, n_dec, _ = x_sample.shape
    d_attn = N_HEADS * HEAD_DIM
    d_conv = conv_w.shape[2]
    d_mix = d_attn + d_conv
    n_pages = page_table.shape[1]
    page = cache_k_cmp.shape[2]
    past = n_pages * page
    nbuf = cache_k_win.shape[2]
    tm = 512
    assert seq % tm == 0 and seq >= WINDOW and nbuf == WINDOW

    wts = _split_w_in(w_in[0], d_attn, d_conv, d_mix)
    cwk = _compress_weights(cmp_pe_k[0], cmp_w1_k[0], cmp_w2_k[0])
    cwv = _compress_weights(cmp_pe_v[0], cmp_w1_v[0], cmp_w2_v[0])
    woa = w_out[0, :d_attn].astype(BF)
    woc = w_out[0, d_attn:].astype(BF)
    ng = norm_g[0].reshape(1, d)
    fg = final_g.reshape(1, d)
    cw = conv_w[0]

    n_c = batch + nbatch
    c_all = jnp.pad(jnp.concatenate([c_prompt, c_sample], axis=0), ((0, (-n_c) % 8), (0, 0)))
    mod = _ada(c_all, ada_w[0], ada_b[0])
    shift, scale, gate = mod[:, :d], mod[:, d:2 * d], mod[:, 2 * d:]

    x2d = x_prompt.reshape(batch * seq, d)
    sc_p = (1.0 + scale[:batch]).reshape(batch, 1, d)
    sh_p = shift[:batch].reshape(batch, 1, d)
    (qT, kcT, vcT, ksT, vsT, kwT, vwT, kc, vc, kslc, vslcT, kwin, vwinT, gates_tok, yc, sza, cv_p) = _proj_prompt(
        x2d, sc_p, sh_p, ng, wts, cw, batch, seq, tm)
    kcmp, vcmpT = _compress_prompt(kc.reshape(batch, seq, D_KV), vc.reshape(batch, seq, D_KV),
                                   cwk + cwv, batch, seq)
    nq = seq // QT
    gates = gates_tok[:, :3 * N_HEADS].reshape(batch, nq, QT, 3, N_HEADS)
    gates = jnp.transpose(gates, (0, 1, 3, 4, 2)).reshape(batch, nq, 3, NCOL)
    gates = jnp.pad(gates, ((0, 0), (0, 0), (0, 5), (0, 0)))
    o_p = _nsa_prompt(qT, gates, kslc, vslcT, kwin, vwinT, kcmp, vcmpT,
                      _prompt_tables(rel_bias, seq // BLK), batch, seq)
    y_p = _merge(x2d, o_p.reshape(batch * seq, d_attn), sza, yc, gate[:batch].reshape(batch, 1, d),
                 woa, woc, fg, tm, seq)
    y_prompt = y_p.reshape(batch, seq, d)
    heads = lambda aT: jnp.transpose(aT.reshape(aT.shape[0], N_KV, HEAD_DIM, aT.shape[2]), (0, 3, 1, 2))[None]
    p_states = (heads(kcT), heads(vcT), heads(ksT), heads(vsT),
                heads(kwT[:, :, seq - WINDOW:]), heads(vwT[:, :, seq - WINDOW:]), cv_p[None])

    tok = n_dec * nbatch
    xs2d = jnp.transpose(x_sample, (1, 0, 2)).reshape(tok, d)
    rep = lambda a: jnp.tile(a, (n_dec, 1)).reshape(1, tok, d)
    st2d = jnp.transpose(state_conv[0], (1, 0, 2)).reshape((CONV_W - 1) * nbatch, d_conv)
    q_s, kv_s, gates_s, yc_s, sza_s, cv_s = _proj_sample(
        xs2d, rep(1.0 + scale[batch:n_c]), rep(shift[batch:n_c]), ng, wts, cw, st2d, nbatch, n_dec)

    q5 = jnp.transpose(q_s.reshape(n_dec, nbatch, N_KV, HPG, HEAD_DIM), (1, 2, 0, 3, 4))
    eye = jnp.eye(N_KV, dtype=F32)
    qbd = (q5[:, :, :, :, None, :] * eye[None, :, None, None, :, None]).reshape(
        nbatch, N_KV * n_dec * HPG, D_KV).astype(BF)
    g5 = gates_s[:, :3 * N_HEADS].reshape(n_dec, nbatch, 3, N_KV, HPG)
    g5 = jnp.transpose(g5, (1, 2, 3, 0, 4)).reshape(nbatch, 3, N_KV * n_dec * HPG, 1)
    gates3 = jnp.broadcast_to(g5, (nbatch, 3, N_KV * n_dec * HPG, D_KV))
    kv6 = jnp.transpose(kv_s.reshape(n_dec, nbatch, 6, D_KV), (1, 2, 0, 3))
    tails = jnp.pad(kv6[:, 0:2], ((0, 0), (0, 0), (0, 8 - n_dec), (0, 0)))
    tailsT = jnp.pad(jnp.transpose(kv6[:, 2:6], (0, 1, 3, 2)), ((0, 0), (0, 0), (0, 0), (0, LANES - n_dec)))
    nbc = -(-(past // BLK + TAIL_BLOCKS) // LANES) * LANES
    posT = lambda c: jnp.transpose(c, (0, 2, 3, 1)).reshape(c.shape[0], D_KV, c.shape[1])
    caches = tuple(posT(c[0]) for c in (cache_k_cmp, cache_v_cmp, cache_k_slc, cache_v_slc))
    o_s, s_kwT, s_vwT = _nsa_sample(
        page_table, caches, qbd, gates3, tails, tailsT, posT(cache_k_win[0]), posT(cache_v_win[0]),
        _sample_tables(rel_bias, past, n_dec, nbuf, nbc), cwk + cwv, n_dec)
    o6 = o_s.reshape(nbatch, N_KV, n_dec, HPG, N_KV, HEAD_DIM)
    o_diag = jnp.stack([o6[:, g, :, :, g, :] for g in range(N_KV)], axis=1)
    o_tok = jnp.transpose(o_diag, (2, 0, 1, 3, 4)).reshape(tok, d_attn).astype(BF)
    y_s = _merge(xs2d, o_tok, sza_s, yc_s, rep(gate[batch:n_c]), woa, woc, fg, tok, tok)
    y_sample = jnp.transpose(y_s.reshape(n_dec, nbatch, d), (1, 0, 2))
    new = lambda n: kv6[:, n].reshape(1, nbatch, n_dec, N_KV, HEAD_DIM)
    cv_out = jnp.transpose(cv_s.reshape(CONV_W - 1, nbatch, d_conv), (1, 0, 2))[None]
    s_states = (new(0), new(1), new(2), new(3), heads(s_kwT), heads(s_vwT), cv_out)

    return (y_prompt, y_sample) + p_states + s_states
```
